```python
import jax
import jax.numpy as jnp
from jax import lax
import numpy as np

D_MODEL = 2048
BATCH = 4
SEQ = 4096
DEPTH = 2

GRID_W = 64
CTX_LEN = 256
EPS = 1e-6
ROPE_THETA = 10000.0
Q_BLOCK = 128
MOD_CHUNKS = 6

GQA_HEADS = 8
GQA_KV_HEADS = 2
GQA_GROUP = GQA_HEADS // GQA_KV_HEADS
GQA_HEAD_DIM = 128
MLA_HEADS = 8
MLA_Q_RANK = 512
MLA_KV_RANK = 512
MLA_NOPE = 128
MLA_ROPE = 64
MLA_V = 128
ATTN_SIZES = (GQA_HEADS * GQA_HEAD_DIM, GQA_KV_HEADS * GQA_HEAD_DIM, GQA_KV_HEADS * GQA_HEAD_DIM,
              MLA_Q_RANK, MLA_KV_RANK, MLA_ROPE)
ATTN_IN_WIDTH = sum(ATTN_SIZES)
ATTN_SPLIT_IDX = tuple(int(v) for v in np.cumsum(ATTN_SIZES)[:-1])

LRU_WIDTH = 1024
LRU_BLOCKS = 8
LRU_BLOCK_W = LRU_WIDTH // LRU_BLOCKS
LRU_CONV = 4
LRU_CONV_PAD = (2, 1)
LRU_C = 8.0
HG_HEADS = 8
HG_HEAD_DIM = 128
HG_WIDTH = HG_HEADS * HG_HEAD_DIM
HG_CHUNK = 64
REC_SIZES = (LRU_WIDTH, LRU_WIDTH, HG_WIDTH, HG_WIDTH, HG_WIDTH, HG_WIDTH, HG_WIDTH)
REC_IN_WIDTH = sum(REC_SIZES)
REC_SPLIT_IDX = tuple(int(v) for v in np.cumsum(REC_SIZES)[:-1])

MIX_WIDTH = GQA_HEADS * GQA_HEAD_DIM + MLA_HEADS * MLA_V
FFN_HIDDEN = -(-8 * D_MODEL // (3 * 256)) * 256
N_ATTN = (DEPTH + 1) // 2
N_REC = DEPTH // 2

kernel_name = "hybrid_gqa_mla_rglru_hgrn2_dit_block"


def rms_norm(x, w):
    xf = x.astype(jnp.float32)
    y = xf * lax.rsqrt(jnp.mean(xf * xf, axis=-1, keepdims=True) + EPS)
    return (y * w.astype(jnp.float32)).astype(x.dtype)


def modulate(x, shift, scale):
    return x * (1 + scale) + shift


def axial_rope(row_id, col_id, dim):
    quarter = dim // 4
    inv_freq = ROPE_THETA ** (-jnp.arange(quarter, dtype=jnp.float32) / quarter)
    ang = jnp.concatenate([row_id[:, None].astype(jnp.float32) * inv_freq,
                           col_id[:, None].astype(jnp.float32) * inv_freq], axis=-1)
    return jnp.cos(ang), jnp.sin(ang)


def apply_rope(x, cos, sin):
    shape = (1, x.shape[1]) + (1,) * (x.ndim - 3) + (cos.shape[-1],)
    cos = cos.reshape(shape).astype(x.dtype)
    sin = sin.reshape(shape).astype(x.dtype)
    x1, x2 = jnp.split(x, 2, axis=-1)
    return jnp.concatenate([x1 * cos - x2 * sin, x2 * cos + x1 * sin], axis=-1)


def block_attention(q, k, v, scale):
    b, s, hk, g, dq = q.shape
    nb = s // Q_BLOCK
    qb = jnp.moveaxis(q.reshape(b, nb, Q_BLOCK, hk, g, dq), 1, 0)

    def one_block(qi):
        sc = jnp.einsum('bqhgd,bkhd->bhgqk', qi, k).astype(jnp.float32) * scale
        p = jax.nn.softmax(sc, axis=-1).astype(v.dtype)
        return jnp.einsum('bhgqk,bkhd->bqhgd', p, v)

    out = lax.map(one_block, qb)
    return jnp.moveaxis(out, 0, 1).reshape(b, s, hk, g, v.shape[-1])


def attention_heads(u, w, rope_a, rope_b):
    w_in, q_norm_w, k_norm_w, mq_norm_w, m_uq_w, mkv_norm_w, m_ukv_w = w
    b, s, _ = u.shape
    qa, ka, va, cq, ckv, kr = jnp.split(u @ w_in, ATTN_SPLIT_IDX, axis=-1)
    qa = rms_norm(qa.reshape(b, s, GQA_KV_HEADS, GQA_GROUP, GQA_HEAD_DIM), q_norm_w)
    ka = rms_norm(ka.reshape(b, s, GQA_KV_HEADS, GQA_HEAD_DIM), k_norm_w)
    va = va.reshape(b, s, GQA_KV_HEADS, GQA_HEAD_DIM)
    qb = (rms_norm(cq, mq_norm_w) @ m_uq_w).reshape(b, s, MLA_HEADS, MLA_NOPE + MLA_ROPE)
    kvb = (rms_norm(ckv, mkv_norm_w) @ m_ukv_w).reshape(b, s, MLA_HEADS, MLA_NOPE + MLA_V)
    qb_nope, qb_rope = jnp.split(qb, [MLA_NOPE], axis=-1)
    kb_nope, vb = jnp.split(kvb, [MLA_NOPE], axis=-1)
    kr = kr[:, :, None, :]
    if rope_a is not None:
        qa = apply_rope(qa, *rope_a)
        ka = apply_rope(ka, *rope_a)
        qb_rope = apply_rope(qb_rope, *rope_b)
        kr = apply_rope(kr, *rope_b)
    qb = jnp.concatenate([qb_nope, qb_rope], axis=-1)[:, :, :, None, :]
    kb = jnp.concatenate([kb_nope, jnp.broadcast_to(kr, (b, s, MLA_HEADS, MLA_ROPE))], axis=-1)
    return qa, ka, va, qb, kb, vb


def attention_mixer(u_lat, u_ctx, rope_a, rope_b, w, need_ctx):
    qa_l, ka_l, va_l, qb_l, kb_l, vb_l = attention_heads(u_lat, w, rope_a, rope_b)
    qa_c, ka_c, va_c, qb_c, kb_c, vb_c = attention_heads(u_ctx, w, None, None)
    sa = GQA_HEAD_DIM ** -0.5
    sb = (MLA_NOPE + MLA_ROPE) ** -0.5
    b, s, _ = u_lat.shape

    def cat(c_, l_):
        return jnp.concatenate([c_, l_], axis=1)

    o_lat = jnp.concatenate([
        block_attention(qa_l, cat(ka_c, ka_l), cat(va_c, va_l), sa).reshape(b, s, -1),
        block_attention(qb_l, cat(kb_c, kb_l), cat(vb_c, vb_l), sb).reshape(b, s, -1)], axis=-1)
    o_ctx = None
    if need_ctx:
        sc = u_ctx.shape[1]
        o_ctx = jnp.concatenate([
            block_attention(qa_c, ka_c, va_c, sa).reshape(b, sc, -1),
            block_attention(qb_c, kb_c, vb_c, sb).reshape(b, sc, -1)], axis=-1)
    return o_lat, o_ctx


def short_conv(x, w, bias):
    y = lax.conv_general_dilated(x, w[:, None, :].astype(x.dtype), window_strides=(1,),
                                 padding=[LRU_CONV_PAD], dimension_numbers=('NWC', 'WIO', 'NWC'),
                                 feature_group_count=x.shape[-1])
    return y + bias


def rglru_coeffs(xc, ra_w, ra_b, ix_w, ix_b, lam):
    b, s, _ = xc.shape
    xb = xc.reshape(b, s, LRU_BLOCKS, LRU_BLOCK_W)
    r = jax.nn.sigmoid(jnp.einsum('bsnc,ncd->bsnd', xb, ra_w) + ra_b).reshape(b, s, LRU_WIDTH)
    i = jax.nn.sigmoid(jnp.einsum('bsnc,ncd->bsnd', xb, ix_w) + ix_b).reshape(b, s, LRU_WIDTH)
    log_a = -LRU_C * r.astype(jnp.float32) * jax.nn.softplus(-lam.astype(jnp.float32))
    gated_x = (i * xc).astype(jnp.float32)
    return jnp.exp(log_a), jnp.sqrt(-jnp.expm1(2.0 * log_a)) * gated_x


def _lin_combine(e1, e2):
    a1, b1 = e1
    a2, b2 = e2
    return a1 * a2, a2 * b1 + b2


def linear_scan(a, b, h0, reverse):
    if h0 is not None:
        edge = -1 if reverse else 0
        b = b.at[:, edge].add(a[:, edge] * h0)
    _, h = lax.associative_scan(_lin_combine, (a, b), axis=1, reverse=reverse)
    return h


def hg_heads(a):
    b, s, _ = a.shape
    return a.reshape(b, s, HG_HEADS, HG_HEAD_DIM)


def hg_forget(pf, lb):
    f = lb + (1.0 - lb) * jax.nn.sigmoid(pf.astype(jnp.float32))
    return hg_heads(1.0 - f), hg_heads(jnp.log(f))


def hgrn2_chunk_scan(q, k, v, logf, s0, with_output):
    q, k, v, logf = (a.astype(jnp.float32) for a in (q, k, v, logf))
    b, t, h, dk = q.shape
    n = t // HG_CHUNK

    def chunks(a):
        return jnp.moveaxis(a.reshape(b, n, HG_CHUNK, h, a.shape[-1]), 1, 0)

    causal = jnp.tril(jnp.ones((HG_CHUNK, HG_CHUNK), dtype=bool))[None, :, :, None, None]

    def step(S, inp):
        qc, kc, vc, gc = inp
        cum = jnp.cumsum(gc, axis=1)
        last = cum[:, -1]
        S_new = jnp.exp(last)[..., None] * S + jnp.einsum('bshd,bshv->bhdv', kc * jnp.exp(last[:, None] - cum), vc)
        if not with_output:
            return S_new, None
        decay = jnp.exp(jnp.where(causal, cum[:, :, None] - cum[:, None], -jnp.inf))
        scores = jnp.einsum('bthd,bshd,btshd->bhts', qc, kc, decay)
        o = jnp.einsum('bhts,bshv->bthv', scores, vc) + jnp.einsum('bthd,bhdv->bthv', qc * jnp.exp(cum), S)
        return S_new, o

    if s0 is None:
        s0 = jnp.zeros((b, h, dk, v.shape[-1]), jnp.float32)
    S, o = lax.scan(step, s0, (chunks(q), chunks(k), chunks(v), chunks(logf)))
    if with_output:
        o = jnp.moveaxis(o, 0, 1).reshape(b, t, h, v.shape[-1])
    return S, o


def hgrn2_direction(q, k, v, logf, s0, reverse, with_output):
    if reverse:
        q, k, v, logf = (jnp.flip(a, axis=1) for a in (q, k, v, logf))
    S, o = hgrn2_chunk_scan(q, k, v, logf, s0, with_output)
    if reverse and with_output:
        o = jnp.flip(o, axis=1)
    return S, o


def hg_out(o, g, norm_w):
    b, s = o.shape[:2]
    y = rms_norm(o, norm_w.reshape(HG_HEADS, HG_HEAD_DIM)).reshape(b, s, HG_WIDTH)
    return y.astype(g.dtype) * jax.nn.silu(g)


def recurrent_mixer(u_lat, u_ctx, w_in, conv_w, conv_b, ra_w, ra_b, ix_w, ix_b, lam, lb, hg_norm_w, need_ctx):
    p_lat = jnp.split(u_lat @ w_in, REC_SPLIT_IDX, axis=-1)
    p_ctx = jnp.split(u_ctx @ w_in, REC_SPLIT_IDX, axis=-1)
    xc_lat = short_conv(p_lat[0], conv_w, conv_b)
    xc_ctx = short_conv(p_ctx[0], conv_w, conv_b)
    q_lat, v_lat = hg_heads(jax.nn.silu(p_lat[2])), hg_heads(p_lat[5])
    q_ctx, v_ctx = hg_heads(jax.nn.silu(p_ctx[2])), hg_heads(p_ctx[5])
    lru_lat, lru_ctx, hg_lat, hg_ctx = 0.0, 0.0, 0.0, 0.0
    for d, rev in enumerate((False, True)):
        a_c, b_c = rglru_coeffs(xc_ctx, ra_w[d], ra_b[d], ix_w[d], ix_b[d], lam[d])
        h_c = linear_scan(a_c, b_c, None, rev)
        a_l, b_l = rglru_coeffs(xc_lat, ra_w[d], ra_b[d], ix_w[d], ix_b[d], lam[d])
        h_l = linear_scan(a_l, b_l, h_c[:, 0] if rev else h_c[:, -1], rev)
        lru_lat = lru_lat + h_l
        k_c, g_c = hg_forget(p_ctx[3 + d], lb[d])
        s_c, o_c = hgrn2_direction(q_ctx, k_c, v_ctx, g_c, None, rev, need_ctx)
        k_l, g_l = hg_forget(p_lat[3 + d], lb[d])
        _, o_l = hgrn2_direction(q_lat, k_l, v_lat, g_l, s_c, rev, True)
        hg_lat = hg_lat + o_l
        if need_ctx:
            lru_ctx = lru_ctx + h_c
            hg_ctx = hg_ctx + o_c
    y_lat = jnp.concatenate([jax.nn.gelu(p_lat[1]) * lru_lat.astype(p_lat[1].dtype),
                             hg_out(hg_lat, p_lat[6], hg_norm_w)], axis=-1)
    y_ctx = None
    if need_ctx:
        y_ctx = jnp.concatenate([jax.nn.gelu(p_ctx[1]) * lru_ctx.astype(p_ctx[1].dtype),
                                 hg_out(hg_ctx, p_ctx[6], hg_norm_w)], axis=-1)
    return y_lat, y_ctx


def swiglu(u, w_gate, w_up, w_down):
    return (jax.nn.silu(u @ w_gate) * (u @ w_up)) @ w_down


def setup_inputs(seed: int = 0) -> dict:
    key = jax.random.key(seed)
    ks = iter(jax.random.split(key, 32))
    f32 = jnp.float32

    def nrm(shape, fan_in):
        return jax.random.normal(next(ks), shape, f32) * fan_in ** -0.5

    def gain(shape):
        return 1.0 + 0.05 * jax.random.normal(next(ks), shape, f32)

    def bias(shape):
        return 0.02 * jax.random.normal(next(ks), shape, f32)

    x = jax.random.normal(next(ks), (BATCH, SEQ, D_MODEL), f32)
    c = jax.random.normal(next(ks), (BATCH, D_MODEL), f32)
    ctx = jax.random.normal(next(ks), (BATCH, CTX_LEN, D_MODEL), f32)
    c_ctx = jax.random.normal(next(ks), (D_MODEL,), f32)
    mod_w = 0.5 * nrm((DEPTH, D_MODEL, MOD_CHUNKS * D_MODEL), D_MODEL)
    mod_b = bias((DEPTH, MOD_CHUNKS * D_MODEL))
    norm_mix_w = gain((DEPTH, D_MODEL))
    norm_ffn_w = gain((DEPTH, D_MODEL))
    mix_out_w = nrm((DEPTH, MIX_WIDTH, D_MODEL), MIX_WIDTH)
    ffn_gate_w = nrm((DEPTH, D_MODEL, FFN_HIDDEN), D_MODEL)
    ffn_up_w = nrm((DEPTH, D_MODEL, FFN_HIDDEN), D_MODEL)
    ffn_down_w = nrm((DEPTH, FFN_HIDDEN, D_MODEL), FFN_HIDDEN)
    attn_in_w = nrm((N_ATTN, D_MODEL, ATTN_IN_WIDTH), D_MODEL)
    gqa_q_norm_w = gain((N_ATTN, GQA_HEAD_DIM))
    gqa_k_norm_w = gain((N_ATTN, GQA_HEAD_DIM))
    mla_q_norm_w = gain((N_ATTN, MLA_Q_RANK))
    mla_uq_w = nrm((N_ATTN, MLA_Q_RANK, MLA_HEADS * (MLA_NOPE + MLA_ROPE)), MLA_Q_RANK)
    mla_kv_norm_w = gain((N_ATTN, MLA_KV_RANK))
    mla_ukv_w = nrm((N_ATTN, MLA_KV_RANK, MLA_HEADS * (MLA_NOPE + MLA_V)), MLA_KV_RANK)
    rec_in_w = nrm((N_REC, D_MODEL, REC_IN_WIDTH), D_MODEL)
    lru_conv_w = nrm((N_REC, LRU_CONV, LRU_WIDTH), LRU_CONV)
    lru_conv_b = bias((N_REC, LRU_WIDTH))
    lru_ra_w = nrm((N_REC, 2, LRU_BLOCKS, LRU_BLOCK_W, LRU_BLOCK_W), LRU_BLOCK_W)
    lru_ra_b = bias((N_REC, 2, LRU_BLOCKS, LRU_BLOCK_W))
    lru_ix_w = nrm((N_REC, 2, LRU_BLOCKS, LRU_BLOCK_W, LRU_BLOCK_W), LRU_BLOCK_W)
    lru_ix_b = bias((N_REC, 2, LRU_BLOCKS, LRU_BLOCK_W))
    a_pow_c = jax.random.uniform(next(ks), (N_REC, 2, LRU_WIDTH), f32, 0.9, 0.999)
    a0 = a_pow_c ** (1.0 / LRU_C)
    lru_lambda = jnp.log(a0) - jnp.log1p(-a0)
    hgrn_lb_logits = jax.random.normal(next(ks), (2, DEPTH, HG_WIDTH), f32)
    hgrn_norm_w = gain((N_REC, HG_WIDTH))
    final_norm_w = gain((D_MODEL,))
    return {"x": x, "c": c, "ctx": ctx, "c_ctx": c_ctx, "mod_w": mod_w, "mod_b": mod_b,
            "norm_mix_w": norm_mix_w, "norm_ffn_w": norm_ffn_w, "mix_out_w": mix_out_w,
            "ffn_gate_w": ffn_gate_w, "ffn_up_w": ffn_up_w, "ffn_down_w": ffn_down_w,
            "attn_in_w": attn_in_w, "gqa_q_norm_w": gqa_q_norm_w, "gqa_k_norm_w": gqa_k_norm_w,
            "mla_q_norm_w": mla_q_norm_w, "mla_uq_w": mla_uq_w, "mla_kv_norm_w": mla_kv_norm_w,
            "mla_ukv_w": mla_ukv_w, "rec_in_w": rec_in_w, "lru_conv_w": lru_conv_w,
            "lru_conv_b": lru_conv_b, "lru_ra_w": lru_ra_w, "lru_ra_b": lru_ra_b,
            "lru_ix_w": lru_ix_w, "lru_ix_b": lru_ix_b, "lru_lambda": lru_lambda,
            "hgrn_lb_logits": hgrn_lb_logits, "hgrn_norm_w": hgrn_norm_w, "final_norm_w": final_norm_w}


def reference(x, c, ctx, c_ctx, mod_w, mod_b, norm_mix_w, norm_ffn_w, mix_out_w, ffn_gate_w, ffn_up_w,
              ffn_down_w, attn_in_w, gqa_q_norm_w, gqa_k_norm_w, mla_q_norm_w, mla_uq_w, mla_kv_norm_w,
              mla_ukv_w, rec_in_w, lru_conv_w, lru_conv_b, lru_ra_w, lru_ra_b, lru_ix_w, lru_ix_b,
              lru_lambda, hgrn_lb_logits, hgrn_norm_w, final_norm_w):
    seq = x.shape[1]
    rows = seq // GRID_W
    row_id = jnp.repeat(jnp.arange(rows), GRID_W)
    col_id = jnp.arange(rows * GRID_W) % GRID_W
    rope_a = axial_rope(row_id, col_id, GQA_HEAD_DIM)
    rope_b = axial_rope(row_id, col_id, MLA_ROPE)
    lb_all = jnp.cumsum(jax.nn.softmax(hgrn_lb_logits.astype(jnp.float32), axis=1), axis=1)
    lb_all = lb_all - lb_all[:, :1]
    silu_c = jax.nn.silu(c)
    silu_cc = jax.nn.silu(c_ctx)
    h_lat, h_ctx = x, ctx
    for l in range(DEPTH):
        need_ctx = l < DEPTH - 1
        j = l // 2
        mod_lat = (silu_c @ mod_w[l] + mod_b[l])[:, None, :]
        mod_ctx = silu_cc @ mod_w[l] + mod_b[l]
        sh1, sc1, g1, sh2, sc2, g2 = jnp.split(mod_lat, MOD_CHUNKS, axis=-1)
        csh1, csc1, cg1, csh2, csc2, cg2 = jnp.split(mod_ctx, MOD_CHUNKS, axis=-1)
        u_lat = modulate(rms_norm(h_lat, norm_mix_w[l]), sh1, sc1)
        u_ctx = modulate(rms_norm(h_ctx, norm_mix_w[l]), csh1, csc1)
        if l % 2 == 0:
            w = (attn_in_w[j], gqa_q_norm_w[j], gqa_k_norm_w[j], mla_q_norm_w[j], mla_uq_w[j],
                 mla_kv_norm_w[j], mla_ukv_w[j])
            m_lat, m_ctx = attention_mixer(u_lat, u_ctx, rope_a, rope_b, w, need_ctx)
        else:
            m_lat, m_ctx = recurrent_mixer(u_lat, u_ctx, rec_in_w[j], lru_conv_w[j], lru_conv_b[j],
                                           lru_ra_w[j], lru_ra_b[j], lru_ix_w[j], lru_ix_b[j],
                                           lru_lambda[j], lb_all[:, l], hgrn_norm_w[j], need_ctx)
        h_lat = h_lat + g1 * (m_lat @ mix_out_w[l])
        h_lat = h_lat + g2 * swiglu(modulate(rms_norm(h_lat, norm_ffn_w[l]), sh2, sc2),
                                    ffn_gate_w[l], ffn_up_w[l], ffn_down_w[l])
        if need_ctx:
            h_ctx = h_ctx + cg1 * (m_ctx @ mix_out_w[l])
            h_ctx = h_ctx + cg2 * swiglu(modulate(rms_norm(h_ctx, norm_ffn_w[l]), csh2, csc2),
                                         ffn_gate_w[l], ffn_up_w[l], ffn_down_w[l])
    return rms_norm(h_lat, final_norm_w)
```

```python
import functools
import math

import numpy as np
import jax
import jax.numpy as jnp
from jax import lax
from jax.experimental import pallas as pl
from jax.experimental.pallas import tpu as pltpu

F32 = jnp.float32
BF16 = jnp.bfloat16

EPS = 1e-6
ROPE_THETA = 10000.0
GRID_W = 64
MOD_CHUNKS = 6
LANES = 128
HEADS = 8
HEAD_DIM = 128
GQA_KV_HEADS = 2
GQA_GROUP = HEADS // GQA_KV_HEADS
MLA_RANK = 512
MLA_ROPE = 64
MLA_QK = HEAD_DIM + MLA_ROPE
HALF = HEADS * HEAD_DIM
LRU_C = 8.0
HG_CHUNK = 64
HG_SUB = 16
HG_BLOCK = 256
NEG_BIG = -1e30
VMEM_LIMIT = 56 * 1024 * 1024


def _pick(n, prefs):
    for p in prefs:
        if n % p == 0:
            return p
    raise ValueError(f"no tile for {n} in {prefs}")


def _cparams(sem):
    return pltpu.CompilerParams(dimension_semantics=sem, vmem_limit_bytes=VMEM_LIMIT)


def _dot(a, b):
    return jnp.dot(a, b, preferred_element_type=F32)


def _dot_nt(a, b):
    return lax.dot_general(a, b, (((1,), (1,)), ((), ())), preferred_element_type=F32)


def _dot_tn(a, b):
    return lax.dot_general(a, b, (((0,), (0,)), ((), ())), preferred_element_type=F32)


def _sigmoid(x):
    return 1.0 / (1.0 + jnp.exp(-x))


def _mod_kernel(c_ref, w_ref, b_ref, o_ref):
    c = c_ref[...]
    o_ref[0] = _dot(c * _sigmoid(c), w_ref[0]) + b_ref[0]


def _modulation(cc, mod_w, mod_b):
    depth, d, n = mod_w.shape
    tn = _pick(n, (1024, 512, 256, 128))
    return pl.pallas_call(
        _mod_kernel,
        grid=(depth, n // tn),
        in_specs=[pl.BlockSpec((8, d), lambda l, j: (0, 0)),
                  pl.BlockSpec((1, d, tn), lambda l, j: (l, 0, j)),
                  pl.BlockSpec((1, 1, tn), lambda l, j: (l, 0, j))],
        out_specs=pl.BlockSpec((1, 8, tn), lambda l, j: (l, 0, j)),
        out_shape=jax.ShapeDtypeStruct((depth, 8, n), F32),
        compiler_params=_cparams(("parallel", "parallel")),
        name="modulation",
    )(cc, mod_w, mod_b.reshape(depth, 1, n))


def _nm_kernel(n_w, epilogue, tm, rc, h_ref, nw_ref, sh_ref, sc_ref, *rest):
    w_refs, o_ref, u_scr = rest[:n_w], rest[n_w], rest[n_w + 1]

    @pl.when(pl.program_id(1) == 0)
    def _():
        def chunk(r, carry):
            r0 = pl.multiple_of(r * rc, rc)
            x = h_ref[pl.ds(r0, rc), :]
            ms = jnp.mean(x * x, axis=-1, keepdims=True)
            y = x * lax.rsqrt(ms + EPS) * nw_ref[...]
            u_scr[pl.ds(r0, rc), :] = (y * (1.0 + sc_ref[0]) + sh_ref[0]).astype(BF16)
            return carry
        lax.fori_loop(0, tm // rc, chunk, 0)

    u = u_scr[...]
    if epilogue == "plain":
        o_ref[...] = _dot(u, w_refs[0][...]).astype(o_ref.dtype)
    else:
        g = _dot(u, w_refs[0][...])
        up = _dot(u, w_refs[1][...])
        o_ref[...] = (g * _sigmoid(g) * up).astype(o_ref.dtype)


def _norm_mod_matmul(h, rows, norm_w, mods, mod_base, ws, epilogue, out_dtype, seq, batch, tn_prefs, name):
    d = h.shape[1]
    n = ws[0].shape[1]
    tm = _pick(math.gcd(seq, rows), (1024, 512, 256))
    tn = _pick(n, tn_prefs)
    rc = min(tm, 256)

    def mrow(i, k):
        return mod_base + jnp.minimum(i * tm // seq, batch) * MOD_CHUNKS + k

    in_specs = [pl.BlockSpec((tm, d), lambda i, j: (i, 0)),
                pl.BlockSpec((1, d), lambda i, j: (0, 0)),
                pl.BlockSpec((1, 1, d), lambda i, j: (mrow(i, 0), 0, 0)),
                pl.BlockSpec((1, 1, d), lambda i, j: (mrow(i, 1), 0, 0))]
    in_specs += [pl.BlockSpec((d, tn), lambda i, j: (0, j)) for _ in ws]
    return pl.pallas_call(
        functools.partial(_nm_kernel, len(ws), epilogue, tm, rc),
        grid=(rows // tm, n // tn),
        in_specs=in_specs,
        out_specs=pl.BlockSpec((tm, tn), lambda i, j: (i, j)),
        out_shape=jax.ShapeDtypeStruct((rows, n), out_dtype),
        scratch_shapes=[pltpu.VMEM((tm, d), BF16)],
        compiler_params=_cparams(("parallel", "arbitrary")),
        name=name,
    )(h, norm_w.reshape(1, d), mods, mods, *ws)


def _mr_kernel(n_a, *refs):
    a_refs, w_refs = refs[:n_a], refs[n_a:2 * n_a]
    res_ref, g_ref, o_ref = refs[2 * n_a:]
    acc = _dot(a_refs[0][...], w_refs[0][...])
    for a, w in zip(a_refs[1:], w_refs[1:]):
        acc += _dot(a[...], w[...])
    o_ref[...] = res_ref[...] + g_ref[0] * acc


def _matmul_residual(a_list, w_list, res, rows, mods, mod_row0, seq, batch, name):
    d = res.shape[1]
    tm = _pick(math.gcd(seq, rows), (1024, 512, 256))
    tn = _pick(d, (512, 256, 128))

    def mrow(i):
        return mod_row0 + jnp.minimum(i * tm // seq, batch) * MOD_CHUNKS

    in_specs = [pl.BlockSpec((tm, a.shape[1]), lambda i, j: (i, 0)) for a in a_list]
    in_specs += [pl.BlockSpec((w.shape[0], tn), lambda i, j: (0, j)) for w in w_list]
    in_specs += [pl.BlockSpec((tm, tn), lambda i, j: (i, j)),
                 pl.BlockSpec((1, 1, tn), lambda i, j: (mrow(i), 0, j))]
    return pl.pallas_call(
        functools.partial(_mr_kernel, len(a_list)),
        grid=(rows // tm, d // tn),
        in_specs=in_specs,
        out_specs=pl.BlockSpec((tm, tn), lambda i, j: (i, j)),
        out_shape=jax.ShapeDtypeStruct((rows, d), F32),
        compiler_params=_cparams(("parallel", "arbitrary")),
        name=name,
    )(*a_list, *w_list, res, mods)


def _fn_kernel(h_ref, w_ref, o_ref):
    x = h_ref[...]
    ms = jnp.mean(x * x, axis=-1, keepdims=True)
    o_ref[...] = x * lax.rsqrt(ms + EPS) * w_ref[...]


def _final_norm(h, w):
    rows, d = h.shape
    tm = _pick(rows, (256, 128))
    return pl.pallas_call(
        _fn_kernel,
        grid=(rows // tm,),
        in_specs=[pl.BlockSpec((tm, d), lambda i: (i, 0)), pl.BlockSpec((1, d), lambda i: (0, 0))],
        out_specs=pl.BlockSpec((tm, d), lambda i: (i, 0)),
        out_shape=jax.ShapeDtypeStruct((rows, d), F32),
        compiler_params=_cparams(("parallel",)),
        name="final_norm",
    )(h, w.reshape(1, d))


def _prep_kernel(sa, sb, y_ref, qw_ref, kw_ref, mqw_ref, mkw_ref, uq_ref, ukv_ref,
                 ca_ref, sna_ref, cb_ref, snb_ref,
                 qg_ref, kg_ref, vg_ref, qm_ref, km_ref, vm_ref):
    hd = HEAD_DIM
    ca, sna, cb, snb = ca_ref[...], sna_ref[...], cb_ref[...], snb_ref[...]

    def head_norm_rope(x, w):
        r = lax.rsqrt(jnp.mean(x * x, axis=-1, keepdims=True) + EPS)
        yh = x * r * w
        return yh * ca + pltpu.roll(yh, hd // 2, axis=1) * sna

    o = 0
    for h in range(HEADS):
        qg_ref[0, h] = (head_norm_rope(y_ref[:, o + h * hd:o + (h + 1) * hd], qw_ref[...]) * sa).astype(BF16)
    o += HALF
    for h in range(GQA_KV_HEADS):
        kg_ref[0, h] = head_norm_rope(y_ref[:, o + h * hd:o + (h + 1) * hd], kw_ref[...]).astype(BF16)
    o += GQA_KV_HEADS * hd
    for h in range(GQA_KV_HEADS):
        vg_ref[0, h] = y_ref[:, o + h * hd:o + (h + 1) * hd].astype(BF16)
    o += GQA_KV_HEADS * hd

    def rms(x, w):
        return (x * lax.rsqrt(jnp.mean(x * x, axis=-1, keepdims=True) + EPS) * w).astype(BF16)

    qb = _dot(rms(y_ref[:, o:o + MLA_RANK], mqw_ref[...]), uq_ref[...])
    o += MLA_RANK
    kvb = _dot(rms(y_ref[:, o:o + MLA_RANK], mkw_ref[...]), ukv_ref[...])
    o += MLA_RANK
    kr = (y_ref[:, o:o + hd] * cb + y_ref[:, o + hd:o + 2 * hd] * snb).astype(BF16)
    for h in range(HEADS):
        lo, hi = h * hd, (h + 1) * hd
        qm_ref[0, h, :, :hd] = (qb[:, lo:hi] * sb).astype(BF16)
        qm_ref[0, h, :, hd:] = ((qb[:, HALF + lo:HALF + hi] * cb + qb[:, 2 * HALF + lo:2 * HALF + hi] * snb) * sb).astype(BF16)
        km_ref[0, h, :, :hd] = kvb[:, lo:hi].astype(BF16)
        km_ref[0, h, :, hd:] = kr
        vm_ref[0, h] = kvb[:, HALF + lo:HALF + hi].astype(BF16)


def _attn_prep(y, qw, kw, mqw, mkw, uq, ukv, tabs, batch, seq, ctx_len):
    rows, n = y.shape
    ts = ctx_len
    nl = seq // ts
    n_lat = batch * nl
    t_all = seq + ctx_len
    hd = HEAD_DIM

    def bidx(t):
        return jnp.where(t < n_lat, t // nl, t - n_lat)

    def sidx(t):
        return jnp.where(t < n_lat, t % nl + 1, 0)

    def ridx(t):
        return jnp.where(t < n_lat, t % nl, nl)

    def hm(width, heads):
        return pl.BlockSpec((1, heads, ts, width), lambda t: (bidx(t), 0, sidx(t), 0))

    full = lambda a: pl.BlockSpec(a.shape, lambda t: (0,) * a.ndim)
    tab = pl.BlockSpec((ts, hd), lambda t: (ridx(t), 0))
    outs = [((batch, HEADS, t_all, hd), hm(hd, HEADS)),
            ((batch, GQA_KV_HEADS, t_all, hd), hm(hd, GQA_KV_HEADS)),
            ((batch, GQA_KV_HEADS, t_all, hd), hm(hd, GQA_KV_HEADS)),
            ((batch, HEADS, t_all, 2 * hd), hm(2 * hd, HEADS)),
            ((batch, HEADS, t_all, 2 * hd), hm(2 * hd, HEADS)),
            ((batch, HEADS, t_all, hd), hm(hd, HEADS))]
    return pl.pallas_call(
        functools.partial(_prep_kernel, hd ** -0.5, MLA_QK ** -0.5),
        grid=(rows // ts,),
        in_specs=[pl.BlockSpec((ts, n), lambda t: (t, 0)), full(qw), full(kw), full(mqw), full(mkw),
                  full(uq), full(ukv), tab, tab, tab, tab],
        out_specs=[s for _, s in outs],
        out_shape=[jax.ShapeDtypeStruct(sh, BF16) for sh, _ in outs],
        compiler_params=_cparams(("parallel",)),
        name="attn_prep",
    )(y, qw, kw, mqw, mkw, uq, ukv, *tabs)


def _attn_kernel(group, ctx_len, q_ref, k_ref, v_ref, o_ref):
    dv = v_ref.shape[-1]

    def run(t_k):
        k = k_ref[0, 0, :t_k, :]
        v = v_ref[0, 0, :t_k, :]
        for g in range(group):
            s = _dot_nt(q_ref[0, g], k)
            p = jnp.exp(s - jnp.max(s, axis=-1, keepdims=True))
            l = jnp.sum(p, axis=-1, keepdims=True)
            o_ref[:, g * dv:(g + 1) * dv] = (_dot(p.astype(BF16), v) / l).astype(o_ref.dtype)

    is_ctx = pl.program_id(2) == 0

    @pl.when(is_ctx)
    def _():
        run(ctx_len)

    @pl.when(jnp.logical_not(is_ctx))
    def _():
        run(k_ref.shape[2])


def _attention(q, k, v, group, batch, seq, ctx_len, name):
    _, hq, t_all, dq = q.shape
    hkv = hq // group
    dv = v.shape[-1]
    tq = ctx_len
    nq = seq // tq
    rows = batch * t_all

    def orow(b, qi):
        return jnp.where(qi == 0, batch * nq + b, b * nq + qi - 1)

    return pl.pallas_call(
        functools.partial(_attn_kernel, group, ctx_len),
        grid=(batch, hkv, nq + 1),
        in_specs=[pl.BlockSpec((1, group, tq, dq), lambda b, h, qi: (b, h, qi, 0)),
                  pl.BlockSpec((1, 1, t_all, dq), lambda b, h, qi: (b, h, 0, 0)),
                  pl.BlockSpec((1, 1, t_all, dv), lambda b, h, qi: (b, h, 0, 0))],
        out_specs=pl.BlockSpec((tq, group * dv), lambda b, h, qi: (orow(b, qi), h)),
        out_shape=jax.ShapeDtypeStruct((rows, hkv * group * dv), BF16),
        compiler_params=_cparams(("parallel", "parallel", "arbitrary")),
        name=name,
    )(q, k, v)


def _gelu(x):
    return 0.5 * x * (1.0 + jnp.tanh(0.7978845608028654 * (x + 0.044715 * x * x * x)))


def _lru_kernel(seq, ctx_len, rc, xl_ref, xc_ref, gate_ref, cw_ref, cb_ref, w_ref, b_ref, lam_ref,
                o_ref, xp_scr, af_scr, bf_scr, ab_scr, bb_scr, hs_scr):
    cb_w = xl_ref.shape[1]
    nblk = cb_w // LANES
    lam = lam_ref[...]
    sp = jnp.maximum(-lam, 0.0) + jnp.log(1.0 + jnp.exp(-jnp.abs(lam)))
    cw = cw_ref[...]
    cbias = cb_ref[...]
    zeros8 = jnp.zeros((8, cb_w), F32)

    def coeffs(src_ref, n_rows, dst0):
        xp_scr[pl.ds(0, 8), :] = zeros8
        xp_scr[pl.ds(8 + n_rows, 8), :] = zeros8

        def cp(r, carry):
            r0 = pl.multiple_of(r * rc, rc)
            xp_scr[pl.ds(8 + r0, rc), :] = src_ref[pl.ds(r0, rc), :]
            return carry
        lax.fori_loop(0, n_rows // rc, cp, 0)

        def chunk(r, carry):
            r0 = pl.multiple_of(r * rc, rc)
            win = xp_scr[pl.ds(r0, rc + 16), :]
            xc = cbias + cw[2:3] * win[8:8 + rc]
            xc += cw[0:1] * pltpu.roll(win, 2, axis=0)[8:8 + rc]
            xc += cw[1:2] * pltpu.roll(win, 1, axis=0)[8:8 + rc]
            xc += cw[3:4] * pltpu.roll(win, rc + 15, axis=0)[8:8 + rc]
            for n in range(nblk):
                lo, hi = n * LANES, (n + 1) * LANES
                xcn = xc[:, lo:hi]
                z = _dot(xcn.astype(BF16), w_ref[n]) + b_ref[n]
                for d, (a_scr, b_scr) in enumerate(((af_scr, bf_scr), (ab_scr, bb_scr))):
                    r_g = _sigmoid(z[:, (2 * d) * LANES:(2 * d + 1) * LANES])
                    i_g = _sigmoid(z[:, (2 * d + 1) * LANES:(2 * d + 2) * LANES])
                    a = jnp.exp(-LRU_C * r_g * sp[d:d + 1, lo:hi])
                    a_scr[pl.ds(dst0 + r0, rc), lo:hi] = a
                    b_scr[pl.ds(dst0 + r0, rc), lo:hi] = jnp.sqrt(1.0 - a * a) * (i_g * xcn)
            return carry
        lax.fori_loop(0, n_rows // rc, chunk, 0)

    coeffs(xc_ref, ctx_len, 0)
    coeffs(xl_ref, seq, ctx_len)
    t_all = seq + ctx_len

    def step(t, carry):
        hf, hb = carry
        hf = af_scr[pl.ds(t, 1), :] * hf + bf_scr[pl.ds(t, 1), :]
        hs_scr[pl.ds(t, 1), :] = hf
        tb = jnp.where(t < ctx_len, ctx_len - 1 - t, t_all + ctx_len - 1 - t)
        hb = ab_scr[pl.ds(tb, 1), :] * hb + bb_scr[pl.ds(tb, 1), :]
        bb_scr[pl.ds(tb, 1), :] = hb
        return hf, hb
    h0 = jnp.zeros((1, cb_w), F32)
    lax.fori_loop(0, t_all, step, (h0, h0))

    def fin(r, carry):
        r0 = pl.multiple_of(r * rc, rc)
        hsum = hs_scr[pl.ds(ctx_len + r0, rc), :] + bb_scr[pl.ds(ctx_len + r0, rc), :]
        o_ref[pl.ds(r0, rc), :] = (_gelu(gate_ref[pl.ds(r0, rc), :]) * hsum).astype(o_ref.dtype)
        return carry
    lax.fori_loop(0, seq // rc, fin, 0)


def _lru(y, conv_w, conv_b, w_cat, b_cat, lam, batch, seq, ctx_len):
    cb_w = 2 * LANES
    ncb = HALF // cb_w
    nl = batch * seq
    rc = 256
    t_all = seq + ctx_len
    scr = lambda r: pltpu.VMEM((r, cb_w), F32)
    return pl.pallas_call(
        functools.partial(_lru_kernel, seq, ctx_len, rc),
        grid=(batch, ncb),
        in_specs=[pl.BlockSpec((seq, cb_w), lambda b, c: (b, c)),
                  pl.BlockSpec((ctx_len, cb_w), lambda b, c: (nl // ctx_len + b, c)),
                  pl.BlockSpec((seq, cb_w), lambda b, c: (b, ncb + c)),
                  pl.BlockSpec((4, cb_w), lambda b, c: (0, c)),
                  pl.BlockSpec((1, cb_w), lambda b, c: (0, c)),
                  pl.BlockSpec((cb_w // LANES, LANES, 4 * LANES), lambda b, c: (c, 0, 0)),
                  pl.BlockSpec((cb_w // LANES, 1, 4 * LANES), lambda b, c: (c, 0, 0)),
                  pl.BlockSpec((2, cb_w), lambda b, c: (0, c))],
        out_specs=pl.BlockSpec((seq, cb_w), lambda b, c: (b, c)),
        out_shape=jax.ShapeDtypeStruct((nl, HALF), BF16),
        scratch_shapes=[scr(seq + 16), scr(t_all), scr(t_all), scr(t_all), scr(t_all), scr(t_all)],
        compiler_params=_cparams(("parallel", "parallel")),
        name="rglru",
    )(y, y, y, conv_w, conv_b, w_cat, b_cat, lam)


def _hg_tables():
    c, sub = HG_CHUNK, HG_SUB
    masks, pairs = [], []
    for d in range(2):
        r = np.arange(c) if d == 0 else c - 1 - np.arange(c)
        rt, rs = r[:, None], r[None, :]
        m = [rs <= rt, rs > rt]
        pm = []
        w = c // 2
        while w >= sub:
            bnd = w * (2 * (rt // (2 * w)) + 1)
            odd = (rt // w) % 2 == 1
            m.append(odd & (bnd <= rs) & (rs <= rt))
            m.append(~odd & (rt < rs) & (rs <= bnd - 1))
            pm.append(odd & ((rs // w) % 2 == 0) & (rt // (2 * w) == rs // (2 * w)))
            w //= 2
        m.append((rs <= rt) & (rs // sub == rt // sub))
        m.append(np.ones((8, c), bool))
        masks.append(np.concatenate(m, axis=0))
        pairs.append(np.stack(pm))
    return np.stack(masks).astype(np.float32), np.stack(pairs).astype(np.float32)


def _hg_kernel(n_ctx_blk, n_levels, q_ref, f_ref, v_ref, lb_ref, mall_ref, pm_ref, o_ref, s_scr):
    c, sub, hd = HG_CHUNK, HG_SUB, HEAD_DIM
    nsub = c // sub
    d = pl.program_id(0)
    st = pl.program_id(2)
    n_chunks = HG_BLOCK // c

    @pl.when(st == 0)
    def _():
        s_scr[...] = jnp.zeros_like(s_scr)

    mall = mall_ref[0]
    ones_red = jnp.ones((hd, hd), BF16)
    sgn = 1 - 2 * d
    tl_s = (lax.broadcasted_iota(jnp.int32, (c, hd), 0) % sub) * sgn

    def bcast_sub(x, j):
        x3 = x.reshape(nsub, sub, hd)
        return jnp.broadcast_to(x3[:, j:j + 1, :], (nsub, sub, hd)).reshape(c, hd)

    def chunk(i, carry):
        ci = jnp.where(d == 0, i, n_chunks - 1 - i)
        r0 = pl.multiple_of(ci * c, c)
        for h in range(HEADS):
            lo, hi = h * hd, (h + 1) * hd
            lb = lb_ref[0, :, lo:hi]
            f = lb + (1.0 - lb) * _sigmoid(f_ref[pl.ds(r0, c), lo:hi])
            k = 1.0 - f
            logf = jnp.log(f)
            lf_hi = logf.astype(BF16)
            lf_lo = (logf - lf_hi.astype(F32)).astype(BF16)
            cs = _dot(mall, lf_hi) + _dot(mall, lf_lo)
            rows = lambda n: cs[n * c:(n + 1) * c]
            qx = q_ref[pl.ds(r0, c), lo:hi]
            q = qx * _sigmoid(qx)
            v = v_ref[pl.ds(r0, c), lo:hi]
            v16 = v.astype(BF16)
            s_t = s_scr[h]

            @pl.when(st >= n_ctx_blk)
            def _():
                o = _dot_nt((q * jnp.exp(rows(0))).astype(BF16), s_t.astype(BF16))
                a = None
                for lv in range(n_levels):
                    ql = (q * jnp.exp(rows(2 + 2 * lv))).astype(BF16)
                    kl = (k * jnp.exp(rows(3 + 2 * lv))).astype(BF16)
                    sc = pm_ref[0, lv] * _dot_nt(ql, kl)
                    a = sc if a is None else a + sc
                o = o + _dot(a.astype(BF16), v16)
                cl = rows(2 + 2 * n_levels)
                for j in range(sub):
                    e = jnp.exp(jnp.where(tl_s >= j * sgn, cl - bcast_sub(cl, j), NEG_BIG))
                    p = (q * bcast_sub(k, j) * e).astype(BF16)
                    o = o + _dot(p, ones_red) * bcast_sub(v, j)
                o_ref[0, pl.ds(r0, c), lo:hi] = o

            base = (3 + 2 * n_levels) * c
            last = cs[base:base + 1]
            kst = (k * jnp.exp(rows(1))).astype(BF16)
            s_scr[h] = jnp.exp(last) * s_t + _dot_tn(v16, kst)
        return carry
    lax.fori_loop(0, n_chunks, chunk, 0)


def _hgrn(y, lb, batch, seq, ctx_len):
    rb = HG_BLOCK
    n_ctx_blk = ctx_len // rb
    n_lat_blk = seq // rb
    nl = batch * seq
    mall_np, pm_np = _hg_tables()
    n_levels = pm_np.shape[1]
    mall = jnp.asarray(mall_np, BF16)
    pm = jnp.asarray(pm_np, F32)
    cols = HALF // HALF

    def lat_blk(d, st):
        lc = jnp.maximum(st - n_ctx_blk, 0)
        return jnp.where(d == 0, lc, n_lat_blk - 1 - lc)

    def row_blk(d, b, st):
        cc = jnp.where(d == 0, st, n_ctx_blk - 1 - st)
        return jnp.where(st < n_ctx_blk, nl // rb + b * n_ctx_blk + cc, b * n_lat_blk + lat_blk(d, st))

    return pl.pallas_call(
        functools.partial(_hg_kernel, n_ctx_blk, n_levels),
        grid=(2, batch, n_ctx_blk + n_lat_blk),
        in_specs=[pl.BlockSpec((rb, HALF), lambda d, b, st: (row_blk(d, b, st), 2 * cols)),
                  pl.BlockSpec((rb, HALF), lambda d, b, st: (row_blk(d, b, st), 3 * cols + d)),
                  pl.BlockSpec((rb, HALF), lambda d, b, st: (row_blk(d, b, st), 5 * cols)),
                  pl.BlockSpec((1, 1, HALF), lambda d, b, st: (d, 0, 0)),
                  pl.BlockSpec((1,) + mall_np.shape[1:], lambda d, b, st: (d, 0, 0)),
                  pl.BlockSpec((1,) + pm_np.shape[1:], lambda d, b, st: (d, 0, 0, 0))],
        out_specs=pl.BlockSpec((1, rb, HALF), lambda d, b, st: (d, b * n_lat_blk + lat_blk(d, st), 0)),
        out_shape=jax.ShapeDtypeStruct((2, nl, HALF), F32),
        scratch_shapes=[pltpu.VMEM((HEADS, HEAD_DIM, HEAD_DIM), F32)],
        compiler_params=_cparams(("parallel", "parallel", "arbitrary")),
        name="hgrn2",
    )(y, y, y, lb, mall, pm)


def _hgo_kernel(o_ref, g_ref, w_ref, y_ref):
    hd = HEAD_DIM
    for h in range(HEADS):
        lo, hi = h * hd, (h + 1) * hd
        o = o_ref[0, :, lo:hi] + o_ref[1, :, lo:hi]
        n = o * lax.rsqrt(jnp.mean(o * o, axis=-1, keepdims=True) + EPS) * w_ref[:, lo:hi]
        g = g_ref[:, lo:hi]
        y_ref[:, lo:hi] = (n * (g * _sigmoid(g))).astype(y_ref.dtype)


def _hg_out(o, y, norm_w):
    _, nl, w = o.shape
    tm = _pick(nl, (512, 256))
    return pl.pallas_call(
        _hgo_kernel,
        grid=(nl // tm,),
        in_specs=[pl.BlockSpec((2, tm, w), lambda i: (0, i, 0)),
                  pl.BlockSpec((tm, w), lambda i: (i, 6)),
                  pl.BlockSpec((1, w), lambda i: (0, 0))],
        out_specs=pl.BlockSpec((tm, w), lambda i: (i, 0)),
        out_shape=jax.ShapeDtypeStruct((nl, w), BF16),
        compiler_params=_cparams(("parallel",)),
        name="hgrn2_out",
    )(o, y, norm_w.reshape(1, w))


def _rot_cols(w):
    x1, x2 = jnp.split(w, 2, axis=-1)
    return jnp.concatenate([-x2, x1], axis=-1)


def _pad_lanes(w):
    return jnp.pad(w, [(0, 0)] * (w.ndim - 1) + [(0, LANES - w.shape[-1])])


def _attn_weights(w_in, uq, ukv):
    d = w_in.shape[0]
    kr_w = w_in[:, -MLA_ROPE:]
    w_in_p = jnp.concatenate([w_in[:, :-MLA_ROPE], _pad_lanes(kr_w), _pad_lanes(_rot_cols(kr_w))], axis=1)
    uq3 = uq.reshape(MLA_RANK, HEADS, MLA_QK)
    nope, rope = uq3[..., :HEAD_DIM], uq3[..., HEAD_DIM:]
    uq_p = jnp.concatenate([nope.reshape(MLA_RANK, HALF), _pad_lanes(rope).reshape(MLA_RANK, HALF),
                            _pad_lanes(_rot_cols(rope)).reshape(MLA_RANK, HALF)], axis=1)
    ukv3 = ukv.reshape(MLA_RANK, HEADS, 2 * HEAD_DIM)
    ukv_p = jnp.concatenate([ukv3[..., :HEAD_DIM].reshape(MLA_RANK, HALF),
                             ukv3[..., HEAD_DIM:].reshape(MLA_RANK, HALF)], axis=1)
    del d
    return w_in_p.astype(BF16), uq_p.astype(BF16), ukv_p.astype(BF16)


def _rope_tables(seq, ctx_len):
    rows = seq // GRID_W
    row_id = jnp.repeat(jnp.arange(rows), GRID_W).astype(F32)
    col_id = (jnp.arange(seq) % GRID_W).astype(F32)

    def table(dim):
        quarter = dim // 4
        inv_freq = ROPE_THETA ** (-jnp.arange(quarter, dtype=F32) / quarter)
        ang = jnp.concatenate([row_id[:, None] * inv_freq, col_id[:, None] * inv_freq], axis=-1)
        cos, sin = jnp.cos(ang), jnp.sin(ang)
        cos2 = jnp.concatenate([cos, cos], axis=-1)
        cos2 = jnp.concatenate([cos2, jnp.ones((ctx_len, dim), F32)], axis=0)
        return cos2, cos, sin

    ca, _, sin_a = table(HEAD_DIM)
    sna = jnp.concatenate([jnp.concatenate([-sin_a, sin_a], axis=-1), jnp.zeros((ctx_len, HEAD_DIM), F32)], axis=0)
    cb, _, sin_b = table(MLA_ROPE)
    snb = jnp.concatenate([jnp.concatenate([sin_b, sin_b], axis=-1), jnp.zeros((ctx_len, MLA_ROPE), F32)], axis=0)
    return ca, sna, _pad_lanes(cb), _pad_lanes(snb)


def kernel(x, c, ctx, c_ctx, mod_w, mod_b, norm_mix_w, norm_ffn_w, mix_out_w, ffn_gate_w, ffn_up_w, ffn_down_w, attn_in_w, gqa_q_norm_w, gqa_k_norm_w, mla_q_norm_w, mla_uq_w, mla_kv_norm_w, mla_ukv_w, rec_in_w, lru_conv_w, lru_conv_b, lru_ra_w, lru_ra_b, lru_ix_w, lru_ix_b, lru_lambda, hgrn_lb_logits, hgrn_norm_w, final_norm_w):
    batch, seq, d = x.shape
    ctx_len = ctx.shape[1]
    depth = mod_w.shape[0]
    assert depth == 2 and batch < 8 and seq % ctx_len == 0 and ctx_len == HG_BLOCK
    nl = batch * seq
    rows_all = nl + batch * ctx_len

    cc = jnp.zeros((8, d), F32).at[:batch].set(c).at[batch].set(c_ctx)
    mods = _modulation(cc, mod_w, mod_b).reshape(depth * 8 * MOD_CHUNKS, 1, d)
    h = jnp.concatenate([x.reshape(nl, d), ctx.reshape(batch * ctx_len, d)], axis=0)
    nm = functools.partial(_norm_mod_matmul, seq=seq, batch=batch)
    mr = functools.partial(_matmul_residual, seq=seq, batch=batch)

    def ffn(h_in, rows, l):
        base = l * 8 * MOD_CHUNKS
        hid = nm(h_in, rows, norm_ffn_w[l], mods, base + 3,
                 [ffn_gate_w[l].astype(BF16), ffn_up_w[l].astype(BF16)], "swiglu", BF16,
                 tn_prefs=(512, 256, 128), name=f"ffn_up{l}")
        return mr([hid], [ffn_down_w[l].astype(BF16)], h_in, rows, mods, base + 5, name=f"ffn_down{l}")

    w_in_p, uq_p, ukv_p = _attn_weights(attn_in_w[0], mla_uq_w[0], mla_ukv_w[0])
    y = nm(h, rows_all, norm_mix_w[0], mods, 0, [w_in_p], "plain", F32,
           tn_prefs=(1408, 704, 256, 128), name="attn_in")
    qg, kg, vg, qm, km, vm = _attn_prep(
        y, gqa_q_norm_w[0].reshape(1, -1), gqa_k_norm_w[0].reshape(1, -1), mla_q_norm_w[0].reshape(1, -1),
        mla_kv_norm_w[0].reshape(1, -1), uq_p, ukv_p, _rope_tables(seq, ctx_len), batch, seq, ctx_len)
    og = _attention(qg, kg, vg, GQA_GROUP, batch, seq, ctx_len, "gqa")
    om = _attention(qm, km, vm, 1, batch, seq, ctx_len, "mla")
    wo = mix_out_w[0].astype(BF16)
    h = mr([og, om], [wo[:HALF], wo[HALF:]], h, rows_all, mods, 2, name="mix_out0")
    h = ffn(h, rows_all, 0)

    base = 8 * MOD_CHUNKS
    y = nm(h, rows_all, norm_mix_w[1], mods, base, [rec_in_w[0].astype(BF16)], "plain", F32,
           tn_prefs=(1024, 512, 256, 128), name="rec_in")
    w_cat = jnp.concatenate([lru_ra_w[0, 0], lru_ix_w[0, 0], lru_ra_w[0, 1], lru_ix_w[0, 1]], axis=-1).astype(BF16)
    b_cat = jnp.concatenate([lru_ra_b[0, 0], lru_ix_b[0, 0], lru_ra_b[0, 1], lru_ix_b[0, 1]], axis=-1)[:, None, :]
    y_lru = _lru(y, lru_conv_w[0], lru_conv_b[0].reshape(1, -1), w_cat, b_cat, lru_lambda[0], batch, seq, ctx_len)
    lb_all = jnp.cumsum(jax.nn.softmax(hgrn_lb_logits.astype(F32), axis=1), axis=1)
    lb = (lb_all - lb_all[:, :1])[:, 1].reshape(2, 1, HALF)
    o_hg = _hgrn(y, lb, batch, seq, ctx_len)
    y_hg = _hg_out(o_hg, y, hgrn_norm_w[0])
    wo = mix_out_w[1].astype(BF16)
    h = mr([y_lru, y_hg], [wo[:HALF], wo[HALF:]], h, nl, mods, base + 2, name="mix_out1")
    h = ffn(h, nl, 1)
    return _final_norm(h, final_norm_w).reshape(batch, seq, d)
```

```python
import functools
import math

import numpy as np
import jax
import jax.numpy as jnp
from jax import lax
from jax.experimental import pallas as pl
from jax.experimental.pallas import tpu as pltpu

F32 = jnp.float32
BF16 = jnp.bfloat16

EPS = 1e-6
ROPE_THETA = 10000.0
GRID_W = 64
MOD_CHUNKS = 6
LANES = 128
HEADS = 8
HEAD_DIM = 128
GQA_KV_HEADS = 2
GQA_GROUP = HEADS // GQA_KV_HEADS
MLA_RANK = 512
MLA_ROPE = 64
MLA_QK = HEAD_DIM + MLA_ROPE
HALF = HEADS * HEAD_DIM
LRU_C = 8.0
HG_CHUNK = 64
HG_BLOCK = 256
VMEM_LIMIT = 56 * 1024 * 1024


def _pick(n, prefs):
    for p in prefs:
        if n % p == 0:
            return p
    raise ValueError(f"no tile for {n} in {prefs}")


def _cparams(sem):
    return pltpu.CompilerParams(dimension_semantics=sem, vmem_limit_bytes=VMEM_LIMIT)


def _dot(a, b):
    return jnp.dot(a, b, preferred_element_type=F32)


def _dot_nt(a, b):
    return lax.dot_general(a, b, (((1,), (1,)), ((), ())), preferred_element_type=F32)


def _dot_tn(a, b):
    return lax.dot_general(a, b, (((0,), (0,)), ((), ())), preferred_element_type=F32)


def _sigmoid(x):
    return 1.0 / (1.0 + jnp.exp(-x))


def _mod_kernel(c_ref, w_ref, b_ref, o_ref):
    c = c_ref[...]
    o_ref[0] = _dot(c * _sigmoid(c), w_ref[0]) + b_ref[0]


def _modulation(cc, mod_w, mod_b):
    depth, d, n = mod_w.shape
    tn = _pick(n, (1024, 512, 256, 128))
    return pl.pallas_call(
        _mod_kernel,
        grid=(depth, n // tn),
        in_specs=[pl.BlockSpec((8, d), lambda l, j: (0, 0)),
                  pl.BlockSpec((1, d, tn), lambda l, j: (l, 0, j)),
                  pl.BlockSpec((1, 1, tn), lambda l, j: (l, 0, j))],
        out_specs=pl.BlockSpec((1, 8, tn), lambda l, j: (l, 0, j)),
        out_shape=jax.ShapeDtypeStruct((depth, 8, n), F32),
        compiler_params=_cparams(("parallel", "parallel")),
        name="modulation",
    )(cc, mod_w, mod_b.reshape(depth, 1, n))


def _nm_kernel(n_w, epilogue, tm, rc, h_ref, nw_ref, sh_ref, sc_ref, *rest):
    w_refs, o_ref, u_scr = rest[:n_w], rest[n_w], rest[n_w + 1]

    @pl.when(pl.program_id(1) == 0)
    def _():
        def chunk(r, carry):
            r0 = pl.multiple_of(r * rc, rc)
            x = h_ref[pl.ds(r0, rc), :]
            ms = jnp.mean(x * x, axis=-1, keepdims=True)
            y = x * lax.rsqrt(ms + EPS) * nw_ref[...]
            u_scr[pl.ds(r0, rc), :] = (y * (1.0 + sc_ref[0]) + sh_ref[0]).astype(BF16)
            return carry
        lax.fori_loop(0, tm // rc, chunk, 0)

    u = u_scr[...]
    if epilogue == "plain":
        o_ref[...] = _dot(u, w_refs[0][...]).astype(o_ref.dtype)
    else:
        g = _dot(u, w_refs[0][...])
        up = _dot(u, w_refs[1][...])
        o_ref[...] = (g * _sigmoid(g) * up).astype(o_ref.dtype)


def _norm_mod_matmul(h, rows, norm_w, mods, mod_base, ws, epilogue, out_dtype, seq, batch, tn_prefs, name):
    d = h.shape[1]
    n = ws[0].shape[1]
    tm = _pick(math.gcd(seq, rows), (1024, 512, 256))
    tn = _pick(n, tn_prefs)
    rc = min(tm, 256)

    def mrow(i, k):
        return mod_base + jnp.minimum(i * tm // seq, batch) * MOD_CHUNKS + k

    in_specs = [pl.BlockSpec((tm, d), lambda i, j: (i, 0)),
                pl.BlockSpec((1, d), lambda i, j: (0, 0)),
                pl.BlockSpec((1, 1, d), lambda i, j: (mrow(i, 0), 0, 0)),
                pl.BlockSpec((1, 1, d), lambda i, j: (mrow(i, 1), 0, 0))]
    in_specs += [pl.BlockSpec((d, tn), lambda i, j: (0, j)) for _ in ws]
    return pl.pallas_call(
        functools.partial(_nm_kernel, len(ws), epilogue, tm, rc),
        grid=(rows // tm, n // tn),
        in_specs=in_specs,
        out_specs=pl.BlockSpec((tm, tn), lambda i, j: (i, j)),
        out_shape=jax.ShapeDtypeStruct((rows, n), out_dtype),
        scratch_shapes=[pltpu.VMEM((tm, d), BF16)],
        compiler_params=_cparams(("parallel", "arbitrary")),
        name=name,
    )(h, norm_w.reshape(1, d), mods, mods, *ws)


def _mr_kernel(n_a, *refs):
    a_refs, w_refs = refs[:n_a], refs[n_a:2 * n_a]
    res_ref, g_ref, o_ref = refs[2 * n_a:]
    acc = _dot(a_refs[0][...], w_refs[0][...])
    for a, w in zip(a_refs[1:], w_refs[1:]):
        acc += _dot(a[...], w[...])
    o_ref[...] = res_ref[...] + g_ref[0] * acc


def _matmul_residual(a_list, w_list, res, rows, mods, mod_row0, seq, batch, name):
    d = res.shape[1]
    tm = _pick(math.gcd(seq, rows), (1024, 512, 256))
    tn = _pick(d, (512, 256, 128))

    def mrow(i):
        return mod_row0 + jnp.minimum(i * tm // seq, batch) * MOD_CHUNKS

    in_specs = [pl.BlockSpec((tm, a.shape[1]), lambda i, j: (i, 0)) for a in a_list]
    in_specs += [pl.BlockSpec((w.shape[0], tn), lambda i, j: (0, j)) for w in w_list]
    in_specs += [pl.BlockSpec((tm, tn), lambda i, j: (i, j)),
                 pl.BlockSpec((1, 1, tn), lambda i, j: (mrow(i), 0, j))]
    return pl.pallas_call(
        functools.partial(_mr_kernel, len(a_list)),
        grid=(rows // tm, d // tn),
        in_specs=in_specs,
        out_specs=pl.BlockSpec((tm, tn), lambda i, j: (i, j)),
        out_shape=jax.ShapeDtypeStruct((rows, d), F32),
        compiler_params=_cparams(("parallel", "arbitrary")),
        name=name,
    )(*a_list, *w_list, res, mods)


def _fn_kernel(h_ref, w_ref, o_ref):
    x = h_ref[...]
    ms = jnp.mean(x * x, axis=-1, keepdims=True)
    o_ref[...] = x * lax.rsqrt(ms + EPS) * w_ref[...]


def _final_norm(h, w):
    rows, d = h.shape
    tm = _pick(rows, (256, 128))
    return pl.pallas_call(
        _fn_kernel,
        grid=(rows // tm,),
        in_specs=[pl.BlockSpec((tm, d), lambda i: (i, 0)), pl.BlockSpec((1, d), lambda i: (0, 0))],
        out_specs=pl.BlockSpec((tm, d), lambda i: (i, 0)),
        out_shape=jax.ShapeDtypeStruct((rows, d), F32),
        compiler_params=_cparams(("parallel",)),
        name="final_norm",
    )(h, w.reshape(1, d))


def _prep_kernel(sa, sb, y_ref, qw_ref, kw_ref, mqw_ref, mkw_ref, uq_ref, ukv_ref,
                 ca_ref, sna_ref, cb_ref, snb_ref,
                 qg_ref, kg_ref, vg_ref, qm_ref, km_ref, vm_ref):
    hd = HEAD_DIM
    ca, sna, cb, snb = ca_ref[...], sna_ref[...], cb_ref[...], snb_ref[...]

    def head_norm_rope(x, w):
        r = lax.rsqrt(jnp.mean(x * x, axis=-1, keepdims=True) + EPS)
        yh = x * r * w
        return yh * ca + pltpu.roll(yh, hd // 2, axis=1) * sna

    o = 0
    for h in range(HEADS):
        qg_ref[0, h] = (head_norm_rope(y_ref[:, o + h * hd:o + (h + 1) * hd], qw_ref[...]) * sa).astype(BF16)
    o += HALF
    for h in range(GQA_KV_HEADS):
        kg_ref[0, h] = head_norm_rope(y_ref[:, o + h * hd:o + (h + 1) * hd], kw_ref[...]).astype(BF16)
    o += GQA_KV_HEADS * hd
    for h in range(GQA_KV_HEADS):
        vg_ref[0, h] = y_ref[:, o + h * hd:o + (h + 1) * hd].astype(BF16)
    o += GQA_KV_HEADS * hd

    def rms(x, w):
        return (x * lax.rsqrt(jnp.mean(x * x, axis=-1, keepdims=True) + EPS) * w).astype(BF16)

    qb = _dot(rms(y_ref[:, o:o + MLA_RANK], mqw_ref[...]), uq_ref[...])
    o += MLA_RANK
    kvb = _dot(rms(y_ref[:, o:o + MLA_RANK], mkw_ref[...]), ukv_ref[...])
    o += MLA_RANK
    kr = (y_ref[:, o:o + hd] * cb + y_ref[:, o + hd:o + 2 * hd] * snb).astype(BF16)
    for h in range(HEADS):
        lo, hi = h * hd, (h + 1) * hd
        qm_ref[0, h, :, :hd] = (qb[:, lo:hi] * sb).astype(BF16)
        qm_ref[0, h, :, hd:] = ((qb[:, HALF + lo:HALF + hi] * cb + qb[:, 2 * HALF + lo:2 * HALF + hi] * snb) * sb).astype(BF16)
        km_ref[0, h, :, :hd] = kvb[:, lo:hi].astype(BF16)
        km_ref[0, h, :, hd:] = kr
        vm_ref[0, h] = kvb[:, HALF + lo:HALF + hi].astype(BF16)


def _attn_prep(y, qw, kw, mqw, mkw, uq, ukv, tabs, batch, seq, ctx_len):
    rows, n = y.shape
    ts = ctx_len
    nl = seq // ts
    n_lat = batch * nl
    t_all = seq + ctx_len
    hd = HEAD_DIM

    def bidx(t):
        return jnp.where(t < n_lat, t // nl, t - n_lat)

    def sidx(t):
        return jnp.where(t < n_lat, t % nl + 1, 0)

    def ridx(t):
        return jnp.where(t < n_lat, t % nl, nl)

    def hm(width, heads):
        return pl.BlockSpec((1, heads, ts, width), lambda t: (bidx(t), 0, sidx(t), 0))

    full = lambda a: pl.BlockSpec(a.shape, lambda t: (0,) * a.ndim)
    tab = pl.BlockSpec((ts, hd), lambda t: (ridx(t), 0))
    outs = [((batch, HEADS, t_all, hd), hm(hd, HEADS)),
            ((batch, GQA_KV_HEADS, t_all, hd), hm(hd, GQA_KV_HEADS)),
            ((batch, GQA_KV_HEADS, t_all, hd), hm(hd, GQA_KV_HEADS)),
            ((batch, HEADS, t_all, 2 * hd), hm(2 * hd, HEADS)),
            ((batch, HEADS, t_all, 2 * hd), hm(2 * hd, HEADS)),
            ((batch, HEADS, t_all, hd), hm(hd, HEADS))]
    return pl.pallas_call(
        functools.partial(_prep_kernel, hd ** -0.5, MLA_QK ** -0.5),
        grid=(rows // ts,),
        in_specs=[pl.BlockSpec((ts, n), lambda t: (t, 0)), full(qw), full(kw), full(mqw), full(mkw),
                  full(uq), full(ukv), tab, tab, tab, tab],
        out_specs=[s for _, s in outs],
        out_shape=[jax.ShapeDtypeStruct(sh, BF16) for sh, _ in outs],
        compiler_params=_cparams(("parallel",)),
        name="attn_prep",
    )(y, qw, kw, mqw, mkw, uq, ukv, *tabs)


def _attn_kernel(group, ctx_len, q_ref, k_ref, v_ref, o_ref):
    dv = v_ref.shape[-1]

    def run(t_k):
        k = k_ref[0, 0, :t_k, :]
        v = v_ref[0, 0, :t_k, :]
        for g in range(group):
            s = _dot_nt(q_ref[0, g], k)
            p = jnp.exp(s - jnp.max(s, axis=-1, keepdims=True))
            l = jnp.sum(p, axis=-1, keepdims=True)
            o_ref[:, g * dv:(g + 1) * dv] = (_dot(p.astype(BF16), v) / l).astype(o_ref.dtype)

    is_ctx = pl.program_id(2) == 0

    @pl.when(is_ctx)
    def _():
        run(ctx_len)

    @pl.when(jnp.logical_not(is_ctx))
    def _():
        run(k_ref.shape[2])


def _attention(q, k, v, group, batch, seq, ctx_len, name):
    _, hq, t_all, dq = q.shape
    hkv = hq // group
    dv = v.shape[-1]
    tq = ctx_len
    nq = seq // tq
    rows = batch * t_all

    def orow(b, qi):
        return jnp.where(qi == 0, batch * nq + b, b * nq + qi - 1)

    return pl.pallas_call(
        functools.partial(_attn_kernel, group, ctx_len),
        grid=(batch, hkv, nq + 1),
        in_specs=[pl.BlockSpec((1, group, tq, dq), lambda b, h, qi: (b, h, qi, 0)),
                  pl.BlockSpec((1, 1, t_all, dq), lambda b, h, qi: (b, h, 0, 0)),
                  pl.BlockSpec((1, 1, t_all, dv), lambda b, h, qi: (b, h, 0, 0))],
        out_specs=pl.BlockSpec((tq, group * dv), lambda b, h, qi: (orow(b, qi), h)),
        out_shape=jax.ShapeDtypeStruct((rows, hkv * group * dv), BF16),
        compiler_params=_cparams(("parallel", "parallel", "arbitrary")),
        name=name,
    )(q, k, v)


def _gelu(x):
    return 0.5 * x * (1.0 + jnp.tanh(0.7978845608028654 * (x + 0.044715 * x * x * x)))


def _lru_kernel(seq, ctx_len, rc, xl_ref, xc_ref, gate_ref, cw_ref, cb_ref, w_ref, b_ref, lam_ref,
                o_ref, xp_scr, af_scr, bf_scr, ab_scr, bb_scr, hs_scr):
    cb_w = xl_ref.shape[1]
    nblk = cb_w // LANES
    lam = lam_ref[...]
    sp = jnp.maximum(-lam, 0.0) + jnp.log(1.0 + jnp.exp(-jnp.abs(lam)))
    cw = cw_ref[...]
    cbias = cb_ref[...]
    zeros8 = jnp.zeros((8, cb_w), F32)

    def coeffs(src_ref, n_rows, dst0):
        xp_scr[pl.ds(0, 8), :] = zeros8
        xp_scr[pl.ds(8 + n_rows, 8), :] = zeros8

        def cp(r, carry):
            r0 = pl.multiple_of(r * rc, rc)
            xp_scr[pl.ds(8 + r0, rc), :] = src_ref[pl.ds(r0, rc), :]
            return carry
        lax.fori_loop(0, n_rows // rc, cp, 0)

        def chunk(r, carry):
            r0 = pl.multiple_of(r * rc, rc)
            win = xp_scr[pl.ds(r0, rc + 16), :]
            xc = cbias + cw[2:3] * win[8:8 + rc]
            xc += cw[0:1] * pltpu.roll(win, 2, axis=0)[8:8 + rc]
            xc += cw[1:2] * pltpu.roll(win, 1, axis=0)[8:8 + rc]
            xc += cw[3:4] * pltpu.roll(win, rc + 15, axis=0)[8:8 + rc]
            for n in range(nblk):
                lo, hi = n * LANES, (n + 1) * LANES
                xcn = xc[:, lo:hi]
                z = _dot(xcn.astype(BF16), w_ref[n]) + b_ref[n]
                for d, (a_scr, b_scr) in enumerate(((af_scr, bf_scr), (ab_scr, bb_scr))):
                    r_g = _sigmoid(z[:, (2 * d) * LANES:(2 * d + 1) * LANES])
                    i_g = _sigmoid(z[:, (2 * d + 1) * LANES:(2 * d + 2) * LANES])
                    a = jnp.exp(-LRU_C * r_g * sp[d:d + 1, lo:hi])
                    a_scr[pl.ds(dst0 + r0, rc), lo:hi] = a
                    b_scr[pl.ds(dst0 + r0, rc), lo:hi] = jnp.sqrt(1.0 - a * a) * (i_g * xcn)
            return carry
        lax.fori_loop(0, n_rows // rc, chunk, 0)

    coeffs(xc_ref, ctx_len, 0)
    coeffs(xl_ref, seq, ctx_len)
    t_all = seq + ctx_len

    row = lax.broadcasted_iota(jnp.int32, (8, cb_w), 0)

    def group_scan(a, b, h_in, reverse):
        for sh in (1, 2, 4):
            keep = (row < 8 - sh) if reverse else (row >= sh)
            rot = (8 - sh) if reverse else sh
            a_s = jnp.where(keep, pltpu.roll(a, rot, axis=0), 1.0)
            b_s = jnp.where(keep, pltpu.roll(b, rot, axis=0), 0.0)
            b = a * b_s + b
            a = a * a_s
        h = a * h_in + b
        edge = 0 if reverse else 7
        return h, jnp.broadcast_to(h[edge:edge + 1], (8, cb_w))

    n_groups = t_all // 8
    n_ctx_groups = ctx_len // 8

    def step(g, carry):
        hf_in, hb_in = carry
        r0 = pl.multiple_of(g * 8, 8)
        hf, hf_in = group_scan(af_scr[pl.ds(r0, 8), :], bf_scr[pl.ds(r0, 8), :], hf_in, False)
        hs_scr[pl.ds(r0, 8), :] = hf
        gb = jnp.where(g < n_ctx_groups, n_ctx_groups - 1 - g, n_groups + n_ctx_groups - 1 - g)
        rb0 = pl.multiple_of(gb * 8, 8)
        hb, hb_in = group_scan(ab_scr[pl.ds(rb0, 8), :], bb_scr[pl.ds(rb0, 8), :], hb_in, True)
        bb_scr[pl.ds(rb0, 8), :] = hb
        return hf_in, hb_in
    h0 = jnp.zeros((8, cb_w), F32)
    lax.fori_loop(0, n_groups, step, (h0, h0), unroll=2)

    def fin(r, carry):
        r0 = pl.multiple_of(r * rc, rc)
        hsum = hs_scr[pl.ds(ctx_len + r0, rc), :] + bb_scr[pl.ds(ctx_len + r0, rc), :]
        o_ref[pl.ds(r0, rc), :] = (_gelu(gate_ref[pl.ds(r0, rc), :]) * hsum).astype(o_ref.dtype)
        return carry
    lax.fori_loop(0, seq // rc, fin, 0)


def _lru(y, conv_w, conv_b, w_cat, b_cat, lam, batch, seq, ctx_len):
    cb_w = 2 * LANES
    ncb = HALF // cb_w
    nl = batch * seq
    rc = 256
    t_all = seq + ctx_len
    scr = lambda r: pltpu.VMEM((r, cb_w), F32)
    return pl.pallas_call(
        functools.partial(_lru_kernel, seq, ctx_len, rc),
        grid=(batch, ncb),
        in_specs=[pl.BlockSpec((seq, cb_w), lambda b, c: (b, c)),
                  pl.BlockSpec((ctx_len, cb_w), lambda b, c: (nl // ctx_len + b, c)),
                  pl.BlockSpec((seq, cb_w), lambda b, c: (b, ncb + c)),
                  pl.BlockSpec((4, cb_w), lambda b, c: (0, c)),
                  pl.BlockSpec((1, cb_w), lambda b, c: (0, c)),
                  pl.BlockSpec((cb_w // LANES, LANES, 4 * LANES), lambda b, c: (c, 0, 0)),
                  pl.BlockSpec((cb_w // LANES, 1, 4 * LANES), lambda b, c: (c, 0, 0)),
                  pl.BlockSpec((2, cb_w), lambda b, c: (0, c))],
        out_specs=pl.BlockSpec((seq, cb_w), lambda b, c: (b, c)),
        out_shape=jax.ShapeDtypeStruct((nl, HALF), BF16),
        scratch_shapes=[scr(seq + 16), scr(t_all), scr(t_all), scr(t_all), scr(t_all), scr(t_all)],
        compiler_params=_cparams(("parallel", "parallel")),
        name="rglru",
    )(y, y, y, conv_w, conv_b, w_cat, b_cat, lam)


def _hg_tables():
    c = HG_CHUNK
    masks, pairs = [], []
    for d in range(2):
        r = np.arange(c) if d == 0 else c - 1 - np.arange(c)
        rt, rs = r[:, None], r[None, :]
        m = [rs <= rt, rs > rt]
        pm = []
        w = c // 2
        while w >= 1:
            bnd = w * (2 * (rt // (2 * w)) + 1)
            odd = (rt // w) % 2 == 1
            m.append(np.where(odd, (bnd <= rs) & (rs <= rt), (rt < rs) & (rs <= bnd - 1)))
            pm.append(odd & ((rs // w) % 2 == 0) & (rt // (2 * w) == rs // (2 * w)))
            w //= 2
        pm.append(rt == rs)
        m.append(np.ones((8, c), bool))
        masks.append(np.concatenate(m, axis=0))
        pairs.append(np.stack(pm))
    return np.stack(masks).astype(np.float32), np.stack(pairs).astype(np.float32)


def _hg_kernel(n_ctx_blk, n_levels, q_ref, f_ref, v_ref, lb_ref, mall_ref, pm_ref, o_ref, s_scr):
    c, hd = HG_CHUNK, HEAD_DIM
    d = pl.program_id(0)
    st = pl.program_id(2)
    n_chunks = HG_BLOCK // c

    @pl.when(st == 0)
    def _():
        s_scr[...] = jnp.zeros_like(s_scr)

    def make_chunk(with_out):
        def chunk(i, carry):
            ci = jnp.where(d == 0, i, n_chunks - 1 - i)
            r0 = pl.multiple_of(ci * c, c)
            mall = mall_ref[0]
            heads = range(HEADS)
            sl = [slice(h * hd, (h + 1) * hd) for h in heads]
            ks, cs2s = [], []
            for h in heads:
                lb = lb_ref[0, :, sl[h]]
                f = lb + (1.0 - lb) * _sigmoid(f_ref[pl.ds(r0, c), sl[h]])
                logf = jnp.log(f)
                lf_hi = logf.astype(BF16)
                lf_lo = (logf - lf_hi.astype(F32)).astype(BF16)
                ks.append(1.0 - f)
                cs2s.append(_dot(mall, jnp.concatenate([lf_hi, lf_lo], axis=1)))
            es = [jnp.exp(cs2[:, :hd] + cs2[:, hd:]) for cs2 in cs2s]
            rows = lambda h, n: es[h][n * c:(n + 1) * c]
            v16s = [v_ref[pl.ds(r0, c), sl[h]].astype(BF16) for h in heads]
            s_ts = [s_scr[h] for h in heads]
            if with_out:
                qs = []
                for h in heads:
                    qx = q_ref[pl.ds(r0, c), sl[h]]
                    qs.append(qx * _sigmoid(qx))
                a_s = []
                for h in heads:
                    a = pm_ref[0, n_levels] * _dot_nt(qs[h].astype(BF16), ks[h].astype(BF16))
                    for lv in range(n_levels):
                        el = rows(h, 2 + lv)
                        a += pm_ref[0, lv] * _dot_nt((qs[h] * el).astype(BF16), (ks[h] * el).astype(BF16))
                    a_s.append(a.astype(BF16))
                for h in heads:
                    o_ref[0, pl.ds(r0, c), sl[h]] = (
                        _dot(a_s[h], v16s[h]) + _dot_nt((qs[h] * rows(h, 0)).astype(BF16), s_ts[h].astype(BF16)))
            base = (2 + n_levels) * c
            for h in heads:
                s_scr[h] = es[h][base:base + 1] * s_ts[h] + _dot_tn(v16s[h], (ks[h] * rows(h, 1)).astype(BF16))
            return carry
        return chunk

    @pl.when(st >= n_ctx_blk)
    def _():
        lax.fori_loop(0, n_chunks, make_chunk(True), 0)

    @pl.when(st < n_ctx_blk)
    def _():
        lax.fori_loop(0, n_chunks, make_chunk(False), 0)


def _hgrn(y, lb, batch, seq, ctx_len):
    rb = HG_BLOCK
    n_ctx_blk = ctx_len // rb
    n_lat_blk = seq // rb
    nl = batch * seq
    mall_np, pm_np = _hg_tables()
    n_levels = pm_np.shape[1] - 1
    mall = jnp.asarray(mall_np, BF16)
    pm = jnp.asarray(pm_np, F32)
    cols = 1

    def lat_blk(d, st):
        lc = jnp.maximum(st - n_ctx_blk, 0)
        return jnp.where(d == 0, lc, n_lat_blk - 1 - lc)

    def row_blk(d, b, st):
        cc = jnp.where(d == 0, st, n_ctx_blk - 1 - st)
        return jnp.where(st < n_ctx_blk, nl // rb + b * n_ctx_blk + cc, b * n_lat_blk + lat_blk(d, st))

    return pl.pallas_call(
        functools.partial(_hg_kernel, n_ctx_blk, n_levels),
        grid=(2, batch, n_ctx_blk + n_lat_blk),
        in_specs=[pl.BlockSpec((rb, HALF), lambda d, b, st: (row_blk(d, b, st), 2 * cols)),
                  pl.BlockSpec((rb, HALF), lambda d, b, st: (row_blk(d, b, st), 3 * cols + d)),
                  pl.BlockSpec((rb, HALF), lambda d, b, st: (row_blk(d, b, st), 5 * cols)),
                  pl.BlockSpec((1, 1, HALF), lambda d, b, st: (d, 0, 0)),
                  pl.BlockSpec((1,) + mall_np.shape[1:], lambda d, b, st: (d, 0, 0)),
                  pl.BlockSpec((1,) + pm_np.shape[1:], lambda d, b, st: (d, 0, 0, 0))],
        out_specs=pl.BlockSpec((1, rb, HALF), lambda d, b, st: (d, b * n_lat_blk + lat_blk(d, st), 0)),
        out_shape=jax.ShapeDtypeStruct((2, nl, HALF), F32),
        scratch_shapes=[pltpu.VMEM((HEADS, HEAD_DIM, HEAD_DIM), F32)],
        compiler_params=_cparams(("parallel", "parallel", "arbitrary")),
        name="hgrn2",
    )(y, y, y, lb, mall, pm)


def _hgo_kernel(o_ref, g_ref, w_ref, y_ref):
    hd = HEAD_DIM
    for h in range(HEADS):
        lo, hi = h * hd, (h + 1) * hd
        o = o_ref[0, :, lo:hi] + o_ref[1, :, lo:hi]
        n = o * lax.rsqrt(jnp.mean(o * o, axis=-1, keepdims=True) + EPS) * w_ref[:, lo:hi]
        g = g_ref[:, lo:hi]
        y_ref[:, lo:hi] = (n * (g * _sigmoid(g))).astype(y_ref.dtype)


def _hg_out(o, y, norm_w):
    _, nl, w = o.shape
    tm = _pick(nl, (512, 256))
    return pl.pallas_call(
        _hgo_kernel,
        grid=(nl // tm,),
        in_specs=[pl.BlockSpec((2, tm, w), lambda i: (0, i, 0)),
                  pl.BlockSpec((tm, w), lambda i: (i, 6)),
                  pl.BlockSpec((1, w), lambda i: (0, 0))],
        out_specs=pl.BlockSpec((tm, w), lambda i: (i, 0)),
        out_shape=jax.ShapeDtypeStruct((nl, w), BF16),
        compiler_params=_cparams(("parallel",)),
        name="hgrn2_out",
    )(o, y, norm_w.reshape(1, w))


def _rot_cols(w):
    x1, x2 = jnp.split(w, 2, axis=-1)
    return jnp.concatenate([-x2, x1], axis=-1)


def _pad_lanes(w):
    return jnp.pad(w, [(0, 0)] * (w.ndim - 1) + [(0, LANES - w.shape[-1])])


def _attn_weights(w_in, uq, ukv):
    d = w_in.shape[0]
    kr_w = w_in[:, -MLA_ROPE:]
    w_in_p = jnp.concatenate([w_in[:, :-MLA_ROPE], _pad_lanes(kr_w), _pad_lanes(_rot_cols(kr_w))], axis=1)
    uq3 = uq.reshape(MLA_RANK, HEADS, MLA_QK)
    nope, rope = uq3[..., :HEAD_DIM], uq3[..., HEAD_DIM:]
    uq_p = jnp.concatenate([nope.reshape(MLA_RANK, HALF), _pad_lanes(rope).reshape(MLA_RANK, HALF),
                            _pad_lanes(_rot_cols(rope)).reshape(MLA_RANK, HALF)], axis=1)
    ukv3 = ukv.reshape(MLA_RANK, HEADS, 2 * HEAD_DIM)
    ukv_p = jnp.concatenate([ukv3[..., :HEAD_DIM].reshape(MLA_RANK, HALF),
                             ukv3[..., HEAD_DIM:].reshape(MLA_RANK, HALF)], axis=1)
    del d
    return w_in_p.astype(BF16), uq_p.astype(BF16), ukv_p.astype(BF16)


def _rope_tables(seq, ctx_len):
    rows = seq // GRID_W
    row_id = jnp.repeat(jnp.arange(rows), GRID_W).astype(F32)
    col_id = (jnp.arange(seq) % GRID_W).astype(F32)

    def table(dim):
        quarter = dim // 4
        inv_freq = ROPE_THETA ** (-jnp.arange(quarter, dtype=F32) / quarter)
        ang = jnp.concatenate([row_id[:, None] * inv_freq, col_id[:, None] * inv_freq], axis=-1)
        cos, sin = jnp.cos(ang), jnp.sin(ang)
        cos2 = jnp.concatenate([cos, cos], axis=-1)
        cos2 = jnp.concatenate([cos2, jnp.ones((ctx_len, dim), F32)], axis=0)
        return cos2, cos, sin

    ca, _, sin_a = table(HEAD_DIM)
    sna = jnp.concatenate([jnp.concatenate([-sin_a, sin_a], axis=-1), jnp.zeros((ctx_len, HEAD_DIM), F32)], axis=0)
    cb, _, sin_b = table(MLA_ROPE)
    snb = jnp.concatenate([jnp.concatenate([sin_b, sin_b], axis=-1), jnp.zeros((ctx_len, MLA_ROPE), F32)], axis=0)
    return ca, sna, _pad_lanes(cb), _pad_lanes(snb)


def kernel(x, c, ctx, c_ctx, mod_w, mod_b, norm_mix_w, norm_ffn_w, mix_out_w, ffn_gate_w, ffn_up_w, ffn_down_w, attn_in_w, gqa_q_norm_w, gqa_k_norm_w, mla_q_norm_w, mla_uq_w, mla_kv_norm_w, mla_ukv_w, rec_in_w, lru_conv_w, lru_conv_b, lru_ra_w, lru_ra_b, lru_ix_w, lru_ix_b, lru_lambda, hgrn_lb_logits, hgrn_norm_w, final_norm_w):
    batch, seq, d = x.shape
    ctx_len = ctx.shape[1]
    depth = mod_w.shape[0]
    assert depth == 2 and batch < 8 and seq % ctx_len == 0 and ctx_len == HG_BLOCK
    nl = batch * seq
    rows_all = nl + batch * ctx_len

    cc = jnp.zeros((8, d), F32).at[:batch].set(c).at[batch].set(c_ctx)
    mods = _modulation(cc, mod_w, mod_b).reshape(depth * 8 * MOD_CHUNKS, 1, d)
    h = jnp.concatenate([x.reshape(nl, d), ctx.reshape(batch * ctx_len, d)], axis=0)
    nm = functools.partial(_norm_mod_matmul, seq=seq, batch=batch)
    mr = functools.partial(_matmul_residual, seq=seq, batch=batch)

    def ffn(h_in, rows, l):
        base = l * 8 * MOD_CHUNKS
        hid = nm(h_in, rows, norm_ffn_w[l], mods, base + 3,
                 [ffn_gate_w[l].astype(BF16), ffn_up_w[l].astype(BF16)], "swiglu", BF16,
                 tn_prefs=(512, 256, 128), name=f"ffn_up{l}")
        return mr([hid], [ffn_down_w[l].astype(BF16)], h_in, rows, mods, base + 5, name=f"ffn_down{l}")

    w_in_p, uq_p, ukv_p = _attn_weights(attn_in_w[0], mla_uq_w[0], mla_ukv_w[0])
    y = nm(h, rows_all, norm_mix_w[0], mods, 0, [w_in_p], "plain", F32,
           tn_prefs=(1408, 704, 256, 128), name="attn_in")
    qg, kg, vg, qm, km, vm = _attn_prep(
        y, gqa_q_norm_w[0].reshape(1, -1), gqa_k_norm_w[0].reshape(1, -1), mla_q_norm_w[0].reshape(1, -1),
        mla_kv_norm_w[0].reshape(1, -1), uq_p, ukv_p, _rope_tables(seq, ctx_len), batch, seq, ctx_len)
    og = _attention(qg, kg, vg, GQA_GROUP, batch, seq, ctx_len, "gqa")
    om = _attention(qm, km, vm, 1, batch, seq, ctx_len, "mla")
    wo = mix_out_w[0].astype(BF16)
    h = mr([og, om], [wo[:HALF], wo[HALF:]], h, rows_all, mods, 2, name="mix_out0")
    h = ffn(h, rows_all, 0)

    base = 8 * MOD_CHUNKS
    y = nm(h, rows_all, norm_mix_w[1], mods, base, [rec_in_w[0].astype(BF16)], "plain", F32,
           tn_prefs=(1024, 512, 256, 128), name="rec_in")
    w_cat = jnp.concatenate([lru_ra_w[0, 0], lru_ix_w[0, 0], lru_ra_w[0, 1], lru_ix_w[0, 1]], axis=-1).astype(BF16)
    b_cat = jnp.concatenate([lru_ra_b[0, 0], lru_ix_b[0, 0], lru_ra_b[0, 1], lru_ix_b[0, 1]], axis=-1)[:, None, :]
    y_lru = _lru(y, lru_conv_w[0], lru_conv_b[0].reshape(1, -1), w_cat, b_cat, lru_lambda[0], batch, seq, ctx_len)
    lb_all = jnp.cumsum(jax.nn.softmax(hgrn_lb_logits.astype(F32), axis=1), axis=1)
    lb = (lb_all - lb_all[:, :1])[:, 1].reshape(2, 1, HALF)
    o_hg = _hgrn(y, lb, batch, seq, ctx_len)
    y_hg = _hg_out(o_hg, y, hgrn_norm_w[0])
    wo = mix_out_w[1].astype(BF16)
    h = mr([y_lru, y_hg], [wo[:HALF], wo[HALF:]], h, nl, mods, base + 2, name="mix_out1")
    h = ffn(h, nl, 1)
    return _final_norm(h, final_norm_w).reshape(batch, seq, d)
```

```python
import functools
import math

import numpy as np
import jax
import jax.numpy as jnp
from jax import lax
from jax.experimental import pallas as pl
from jax.experimental.pallas import tpu as pltpu

F32 = jnp.float32
BF16 = jnp.bfloat16

EPS = 1e-6
ROPE_THETA = 10000.0
GRID_W = 64
MOD_CHUNKS = 6
LANES = 128
HEADS = 8
HEAD_DIM = 128
GQA_KV_HEADS = 2
GQA_GROUP = HEADS // GQA_KV_HEADS
MLA_RANK = 512
MLA_ROPE = 64
MLA_QK = HEAD_DIM + MLA_ROPE
HALF = HEADS * HEAD_DIM
LRU_C = 8.0
HG_CHUNK = 64
HG_BLOCK = 256
VMEM_LIMIT = 56 * 1024 * 1024


def _pick(n, prefs):
    for p in prefs:
        if n % p == 0:
            return p
    raise ValueError(f"no tile for {n} in {prefs}")


def _cparams(sem):
    return pltpu.CompilerParams(dimension_semantics=sem, vmem_limit_bytes=VMEM_LIMIT)


def _dot(a, b):
    return jnp.dot(a, b, preferred_element_type=F32)


def _dot_nt(a, b):
    return lax.dot_general(a, b, (((1,), (1,)), ((), ())), preferred_element_type=F32)


def _dot_tn(a, b):
    return lax.dot_general(a, b, (((0,), (0,)), ((), ())), preferred_element_type=F32)


def _sigmoid(x):
    return 1.0 / (1.0 + jnp.exp(-x))


def _mod_kernel(c_ref, w_ref, b_ref, o_ref):
    c = c_ref[...]
    o_ref[0] = _dot(c * _sigmoid(c), w_ref[0]) + b_ref[0]


def _modulation(cc, mod_w, mod_b):
    depth, d, n = mod_w.shape
    tn = _pick(n, (1024, 512, 256, 128))
    return pl.pallas_call(
        _mod_kernel,
        grid=(depth, n // tn),
        in_specs=[pl.BlockSpec((8, d), lambda l, j: (0, 0)),
                  pl.BlockSpec((1, d, tn), lambda l, j: (l, 0, j)),
                  pl.BlockSpec((1, 1, tn), lambda l, j: (l, 0, j))],
        out_specs=pl.BlockSpec((1, 8, tn), lambda l, j: (l, 0, j)),
        out_shape=jax.ShapeDtypeStruct((depth, 8, n), F32),
        compiler_params=_cparams(("parallel", "parallel")),
        name="modulation",
    )(cc, mod_w, mod_b.reshape(depth, 1, n))


def _row_sources(h_parts, tm):
    n1 = h_parts[0].shape[0] // tm
    if len(h_parts) == 1:
        return n1, [lambda i: i]
    return n1, [lambda i: jnp.minimum(i, n1 - 1), lambda i: jnp.maximum(i - n1, 0)]


def _nm_kernel(n_h, n_w, n1, epilogue, tm, rc, *refs):
    h_refs = refs[:n_h]
    nw_ref, sh_ref, sc_ref = refs[n_h:n_h + 3]
    w_refs = refs[n_h + 3:n_h + 3 + n_w]
    o_ref, u_scr = refs[n_h + 3 + n_w:]
    i = pl.program_id(0)

    def prologue(h_ref):
        def chunk(r, carry):
            r0 = pl.multiple_of(r * rc, rc)
            x = h_ref[pl.ds(r0, rc), :]
            ms = jnp.mean(x * x, axis=-1, keepdims=True)
            y = x * lax.rsqrt(ms + EPS) * nw_ref[...]
            u_scr[pl.ds(r0, rc), :] = (y * (1.0 + sc_ref[0]) + sh_ref[0]).astype(BF16)
            return carry
        lax.fori_loop(0, tm // rc, chunk, 0)

    first = pl.program_id(1) == 0
    if n_h == 1:
        pl.when(first)(lambda: prologue(h_refs[0]))
    else:
        pl.when(first & (i < n1))(lambda: prologue(h_refs[0]))
        pl.when(first & (i >= n1))(lambda: prologue(h_refs[1]))

    u = u_scr[...]
    if epilogue == "plain":
        o_ref[...] = _dot(u, w_refs[0][0].astype(BF16)).astype(o_ref.dtype)
    else:
        g = _dot(u, w_refs[0][0].astype(BF16))
        up = _dot(u, w_refs[1][0].astype(BF16))
        o_ref[...] = (g * _sigmoid(g) * up).astype(o_ref.dtype)


def _norm_mod_matmul(h_parts, rows, norm_w, mods, mod_base, ws, layer, epilogue, out_dtype, seq, batch, tn_prefs, name):
    d = h_parts[0].shape[1]
    n = ws[0].shape[2]
    tm = _pick(math.gcd(seq, rows), (1024, 512, 256) if len(h_parts) == 1 else (512, 256))
    tn = _pick(n, tn_prefs)
    rc = min(tm, 256)
    n1, rmaps = _row_sources(h_parts, tm)

    def mrow(i, k):
        return mod_base + jnp.minimum(i * tm // seq, batch) * MOD_CHUNKS + k

    in_specs = [pl.BlockSpec((tm, d), lambda i, j, m=m: (m(i), 0)) for m in rmaps]
    in_specs += [pl.BlockSpec((1, d), lambda i, j: (0, 0)),
                 pl.BlockSpec((1, 1, d), lambda i, j: (mrow(i, 0), 0, 0)),
                 pl.BlockSpec((1, 1, d), lambda i, j: (mrow(i, 1), 0, 0))]
    in_specs += [pl.BlockSpec((1, d, tn), lambda i, j: (layer, 0, j)) for _ in ws]
    return pl.pallas_call(
        functools.partial(_nm_kernel, len(h_parts), len(ws), n1, epilogue, tm, rc),
        grid=(rows // tm, n // tn),
        in_specs=in_specs,
        out_specs=pl.BlockSpec((tm, tn), lambda i, j: (i, j)),
        out_shape=jax.ShapeDtypeStruct((rows, n), out_dtype),
        scratch_shapes=[pltpu.VMEM((tm, d), BF16)],
        compiler_params=_cparams(("parallel", "arbitrary")),
        name=name,
    )(*h_parts, norm_w.reshape(1, d), mods, mods, *ws)


def _mr_kernel(n_a, n_r, n1, *refs):
    a_refs, w_refs = refs[:n_a], refs[n_a:2 * n_a]
    res_refs = refs[2 * n_a:2 * n_a + n_r]
    g_ref, o_ref = refs[2 * n_a + n_r:]
    acc = _dot(a_refs[0][...], w_refs[0][0].astype(BF16))
    for a, w in zip(a_refs[1:], w_refs[1:]):
        acc += _dot(a[...], w[0].astype(BF16))
    upd = g_ref[0] * acc
    if n_r == 1:
        o_ref[...] = res_refs[0][...] + upd
    else:
        i = pl.program_id(0)

        @pl.when(i < n1)
        def _():
            o_ref[...] = res_refs[0][...] + upd

        @pl.when(i >= n1)
        def _():
            o_ref[...] = res_refs[1][...] + upd


def _matmul_residual(a_list, w_specs, res_parts, rows, mods, mod_row0, seq, batch, tn_prefs, name):
    d = res_parts[0].shape[1]
    tm = _pick(math.gcd(seq, rows), (1024, 512, 256))
    tn = _pick(d, tn_prefs)
    n1, rmaps = _row_sources(res_parts, tm)

    def mrow(i):
        return mod_row0 + jnp.minimum(i * tm // seq, batch) * MOD_CHUNKS

    in_specs = [pl.BlockSpec((tm, a.shape[1]), lambda i, j: (i, 0)) for a in a_list]
    in_specs += [pl.BlockSpec((1, a.shape[1], tn), lambda i, j, l=l, kb=kb: (l, kb, j))
                 for a, (_, l, kb) in zip(a_list, w_specs)]
    if len(res_parts) == 1:
        in_specs += [pl.BlockSpec((tm, tn), lambda i, j: (i, j))]
    else:
        in_specs += [pl.BlockSpec((tm, tn), lambda i, j: (rmaps[0](i), jnp.where(i < n1, j, 0))),
                     pl.BlockSpec((tm, tn), lambda i, j: (rmaps[1](i), jnp.where(i < n1, 0, j)))]
    in_specs += [pl.BlockSpec((1, 1, tn), lambda i, j: (mrow(i), 0, j))]
    return pl.pallas_call(
        functools.partial(_mr_kernel, len(a_list), len(res_parts), n1),
        grid=(rows // tm, d // tn),
        in_specs=in_specs,
        out_specs=pl.BlockSpec((tm, tn), lambda i, j: (i, j)),
        out_shape=jax.ShapeDtypeStruct((rows, d), F32),
        compiler_params=_cparams(("parallel", "arbitrary")),
        name=name,
    )(*a_list, *[w for w, _, _ in w_specs], *res_parts, mods)


def _fn_kernel(h_ref, w_ref, o_ref):
    x = h_ref[...]
    ms = jnp.mean(x * x, axis=-1, keepdims=True)
    o_ref[...] = x * lax.rsqrt(ms + EPS) * w_ref[...]


def _final_norm(h, w):
    rows, d = h.shape
    tm = _pick(rows, (256, 128))
    return pl.pallas_call(
        _fn_kernel,
        grid=(rows // tm,),
        in_specs=[pl.BlockSpec((tm, d), lambda i: (i, 0)), pl.BlockSpec((1, d), lambda i: (0, 0))],
        out_specs=pl.BlockSpec((tm, d), lambda i: (i, 0)),
        out_shape=jax.ShapeDtypeStruct((rows, d), F32),
        compiler_params=_cparams(("parallel",)),
        name="final_norm",
    )(h, w.reshape(1, d))


def _prep_kernel(sa, sb, y_ref, qw_ref, kw_ref, mqw_ref, mkw_ref, uq_ref, ukv_ref,
                 ca_ref, sna_ref, cb_ref, snb_ref,
                 qg_ref, kg_ref, vg_ref, qm_ref, km_ref, vm_ref):
    hd = HEAD_DIM
    ca, sna, cb, snb = ca_ref[...], sna_ref[...], cb_ref[...], snb_ref[...]

    def head_norm_rope(x, w):
        r = lax.rsqrt(jnp.mean(x * x, axis=-1, keepdims=True) + EPS)
        yh = x * r * w
        return yh * ca + pltpu.roll(yh, hd // 2, axis=1) * sna

    o = 0
    for h in range(HEADS):
        qg_ref[0, h] = (head_norm_rope(y_ref[:, o + h * hd:o + (h + 1) * hd], qw_ref[...]) * sa).astype(BF16)
    o += HALF
    for h in range(GQA_KV_HEADS):
        kg_ref[0, h] = head_norm_rope(y_ref[:, o + h * hd:o + (h + 1) * hd], kw_ref[...]).astype(BF16)
    o += GQA_KV_HEADS * hd
    ones_col = (lax.broadcasted_iota(jnp.int32, (y_ref.shape[0], hd), 1) == 0).astype(BF16)
    for h in range(GQA_KV_HEADS):
        vg_ref[0, h, :, :hd] = y_ref[:, o + h * hd:o + (h + 1) * hd].astype(BF16)
        vg_ref[0, h, :, hd:] = ones_col
    o += GQA_KV_HEADS * hd

    def rms(x, w):
        return (x * lax.rsqrt(jnp.mean(x * x, axis=-1, keepdims=True) + EPS) * w).astype(BF16)

    qb = _dot(rms(y_ref[:, o:o + MLA_RANK], mqw_ref[...]), uq_ref[...])
    o += MLA_RANK
    kvb = _dot(rms(y_ref[:, o:o + MLA_RANK], mkw_ref[...]), ukv_ref[...])
    o += MLA_RANK
    kr = (y_ref[:, o:o + hd] * cb + y_ref[:, o + hd:o + 2 * hd] * snb).astype(BF16)
    for h in range(HEADS):
        lo, hi = h * hd, (h + 1) * hd
        qm_ref[0, h, :, :hd] = (qb[:, lo:hi] * sb).astype(BF16)
        qm_ref[0, h, :, hd:] = ((qb[:, HALF + lo:HALF + hi] * cb + qb[:, 2 * HALF + lo:2 * HALF + hi] * snb) * sb).astype(BF16)
        km_ref[0, h, :, :hd] = kvb[:, lo:hi].astype(BF16)
        km_ref[0, h, :, hd:] = kr
        vm_ref[0, h, :, :hd] = kvb[:, HALF + lo:HALF + hi].astype(BF16)
        vm_ref[0, h, :, hd:] = ones_col


def _attn_prep(y, qw, kw, mqw, mkw, uq, ukv, tabs, batch, seq, ctx_len):
    rows, n = y.shape
    ts = ctx_len
    nl = seq // ts
    n_lat = batch * nl
    t_all = seq + ctx_len
    hd = HEAD_DIM

    def bidx(t):
        return jnp.where(t < n_lat, t // nl, t - n_lat)

    def sidx(t):
        return jnp.where(t < n_lat, t % nl + 1, 0)

    def ridx(t):
        return jnp.where(t < n_lat, t % nl, nl)

    def hm(width, heads):
        return pl.BlockSpec((1, heads, ts, width), lambda t: (bidx(t), 0, sidx(t), 0))

    full = lambda a: pl.BlockSpec(a.shape, lambda t: (0,) * a.ndim)
    tab = pl.BlockSpec((ts, hd), lambda t: (ridx(t), 0))
    outs = [((batch, HEADS, t_all, hd), hm(hd, HEADS)),
            ((batch, GQA_KV_HEADS, t_all, hd), hm(hd, GQA_KV_HEADS)),
            ((batch, GQA_KV_HEADS, t_all, 2 * hd), hm(2 * hd, GQA_KV_HEADS)),
            ((batch, HEADS, t_all, 2 * hd), hm(2 * hd, HEADS)),
            ((batch, HEADS, t_all, 2 * hd), hm(2 * hd, HEADS)),
            ((batch, HEADS, t_all, 2 * hd), hm(2 * hd, HEADS))]
    return pl.pallas_call(
        functools.partial(_prep_kernel, hd ** -0.5, MLA_QK ** -0.5),
        grid=(rows // ts,),
        in_specs=[pl.BlockSpec((ts, n), lambda t: (t, 0)), full(qw), full(kw), full(mqw), full(mkw),
                  full(uq), full(ukv), tab, tab, tab, tab],
        out_specs=[s for _, s in outs],
        out_shape=[jax.ShapeDtypeStruct(sh, BF16) for sh, _ in outs],
        compiler_params=_cparams(("parallel",)),
        name="attn_prep",
    )(y, qw, kw, mqw, mkw, uq, ukv, *tabs)


def _attn_kernel(group, kv_group, ctx_len, q_ref, k_ref, v_ref, o_ref):
    dv = HEAD_DIM

    def run(t_k):
        kv = lambda g: g if kv_group > 1 else 0
        ss = [_dot_nt(q_ref[0, g], k_ref[0, kv(g), :t_k, :]) for g in range(group)]
        for g, s in enumerate(ss):
            p = jnp.exp(s - jnp.max(s, axis=-1, keepdims=True))
            ov = _dot(p.astype(BF16), v_ref[0, kv(g), :t_k, :])
            o_ref[:, g * dv:(g + 1) * dv] = (ov[:, :dv] / ov[:, dv:dv + 1]).astype(o_ref.dtype)

    is_ctx = pl.program_id(2) == 0

    @pl.when(is_ctx)
    def _():
        run(ctx_len)

    @pl.when(jnp.logical_not(is_ctx))
    def _():
        run(k_ref.shape[2])


def _attention(q, k, v, group, kv_group, batch, seq, ctx_len, name):
    _, hq, t_all, dq = q.shape
    dvp = v.shape[-1]
    tq = ctx_len
    nq = seq // tq
    rows = batch * t_all

    def orow(b, qi):
        return jnp.where(qi == 0, batch * nq + b, b * nq + qi - 1)

    return pl.pallas_call(
        functools.partial(_attn_kernel, group, kv_group, ctx_len),
        grid=(batch, hq // group, nq + 1),
        in_specs=[pl.BlockSpec((1, group, tq, dq), lambda b, h, qi: (b, h, qi, 0)),
                  pl.BlockSpec((1, kv_group, t_all, dq), lambda b, h, qi: (b, h, 0, 0)),
                  pl.BlockSpec((1, kv_group, t_all, dvp), lambda b, h, qi: (b, h, 0, 0))],
        out_specs=pl.BlockSpec((tq, group * HEAD_DIM), lambda b, h, qi: (orow(b, qi), h)),
        out_shape=jax.ShapeDtypeStruct((rows, hq * HEAD_DIM), BF16),
        compiler_params=_cparams(("parallel", "parallel", "arbitrary")),
        name=name,
    )(q, k, v)


def _gelu(x):
    return 0.5 * x * (1.0 + jnp.tanh(0.7978845608028654 * (x + 0.044715 * x * x * x)))


def _lru_kernel(seq, ctx_len, rc, xl_ref, xc_ref, gate_ref, cw_ref, cb_ref, w_ref, b_ref, lam_ref,
                o_ref, xp_scr, af_scr, bf_scr, ab_scr, bb_scr, hs_scr):
    cb_w = xl_ref.shape[1]
    nblk = cb_w // LANES
    lam = lam_ref[...]
    sp = jnp.maximum(-lam, 0.0) + jnp.log(1.0 + jnp.exp(-jnp.abs(lam)))
    cw = cw_ref[...]
    cbias = cb_ref[...]
    zeros8 = jnp.zeros((8, cb_w), F32)

    def coeffs(src_ref, n_rows, dst0):
        xp_scr[pl.ds(0, 8), :] = zeros8
        xp_scr[pl.ds(8 + n_rows, 8), :] = zeros8

        def cp(r, carry):
            r0 = pl.multiple_of(r * rc, rc)
            xp_scr[pl.ds(8 + r0, rc), :] = src_ref[pl.ds(r0, rc), :]
            return carry
        lax.fori_loop(0, n_rows // rc, cp, 0)

        def chunk(r, carry):
            r0 = pl.multiple_of(r * rc, rc)
            win = xp_scr[pl.ds(r0, rc + 16), :]
            xc = cbias + cw[2:3] * win[8:8 + rc]
            xc += cw[0:1] * pltpu.roll(win, 2, axis=0)[8:8 + rc]
            xc += cw[1:2] * pltpu.roll(win, 1, axis=0)[8:8 + rc]
            xc += cw[3:4] * pltpu.roll(win, rc + 15, axis=0)[8:8 + rc]
            for n in range(nblk):
                lo, hi = n * LANES, (n + 1) * LANES
                xcn = xc[:, lo:hi]
                z = _dot(xcn.astype(BF16), w_ref[n]) + b_ref[n]
                for d, (a_scr, b_scr) in enumerate(((af_scr, bf_scr), (ab_scr, bb_scr))):
                    r_g = _sigmoid(z[:, (2 * d) * LANES:(2 * d + 1) * LANES])
                    i_g = _sigmoid(z[:, (2 * d + 1) * LANES:(2 * d + 2) * LANES])
                    a = jnp.exp(-LRU_C * r_g * sp[d:d + 1, lo:hi])
                    a_scr[pl.ds(dst0 + r0, rc), lo:hi] = a
                    b_scr[pl.ds(dst0 + r0, rc), lo:hi] = jnp.sqrt(1.0 - a * a) * (i_g * xcn)
            return carry
        lax.fori_loop(0, n_rows // rc, chunk, 0)

    coeffs(xc_ref, ctx_len, 0)
    coeffs(xl_ref, seq, ctx_len)
    t_all = seq + ctx_len

    row = lax.broadcasted_iota(jnp.int32, (8, cb_w), 0)

    def group_scan(a, b, h_in, reverse):
        for sh in (1, 2, 4):
            keep = (row < 8 - sh) if reverse else (row >= sh)
            rot = (8 - sh) if reverse else sh
            a_s = jnp.where(keep, pltpu.roll(a, rot, axis=0), 1.0)
            b_s = jnp.where(keep, pltpu.roll(b, rot, axis=0), 0.0)
            b = a * b_s + b
            a = a * a_s
        h = a * h_in + b
        edge = 0 if reverse else 7
        return h, jnp.broadcast_to(h[edge:edge + 1], (8, cb_w))

    n_groups = t_all // 8
    n_ctx_groups = ctx_len // 8

    def step(g, carry):
        hf_in, hb_in = carry
        r0 = pl.multiple_of(g * 8, 8)
        hf, hf_in = group_scan(af_scr[pl.ds(r0, 8), :], bf_scr[pl.ds(r0, 8), :], hf_in, False)
        hs_scr[pl.ds(r0, 8), :] = hf
        gb = jnp.where(g < n_ctx_groups, n_ctx_groups - 1 - g, n_groups + n_ctx_groups - 1 - g)
        rb0 = pl.multiple_of(gb * 8, 8)
        hb, hb_in = group_scan(ab_scr[pl.ds(rb0, 8), :], bb_scr[pl.ds(rb0, 8), :], hb_in, True)
        bb_scr[pl.ds(rb0, 8), :] = hb
        return hf_in, hb_in
    h0 = jnp.zeros((8, cb_w), F32)
    lax.fori_loop(0, n_groups, step, (h0, h0), unroll=2)

    def fin(r, carry):
        r0 = pl.multiple_of(r * rc, rc)
        hsum = hs_scr[pl.ds(ctx_len + r0, rc), :] + bb_scr[pl.ds(ctx_len + r0, rc), :]
        o_ref[pl.ds(r0, rc), :] = (_gelu(gate_ref[pl.ds(r0, rc), :]) * hsum).astype(o_ref.dtype)
        return carry
    lax.fori_loop(0, seq // rc, fin, 0)


def _lru(y, conv_w, conv_b, w_cat, b_cat, lam, batch, seq, ctx_len):
    cb_w = 2 * LANES
    ncb = HALF // cb_w
    nl = batch * seq
    rc = 256
    t_all = seq + ctx_len
    scr = lambda r: pltpu.VMEM((r, cb_w), F32)
    return pl.pallas_call(
        functools.partial(_lru_kernel, seq, ctx_len, rc),
        grid=(batch, ncb),
        in_specs=[pl.BlockSpec((seq, cb_w), lambda b, c: (b, c)),
                  pl.BlockSpec((ctx_len, cb_w), lambda b, c: (nl // ctx_len + b, c)),
                  pl.BlockSpec((seq, cb_w), lambda b, c: (b, ncb + c)),
                  pl.BlockSpec((4, cb_w), lambda b, c: (0, c)),
                  pl.BlockSpec((1, cb_w), lambda b, c: (0, c)),
                  pl.BlockSpec((cb_w // LANES, LANES, 4 * LANES), lambda b, c: (c, 0, 0)),
                  pl.BlockSpec((cb_w // LANES, 1, 4 * LANES), lambda b, c: (c, 0, 0)),
                  pl.BlockSpec((2, cb_w), lambda b, c: (0, c))],
        out_specs=pl.BlockSpec((seq, cb_w), lambda b, c: (b, c)),
        out_shape=jax.ShapeDtypeStruct((nl, HALF), BF16),
        scratch_shapes=[scr(seq + 16), scr(t_all), scr(t_all), scr(t_all), scr(t_all), scr(t_all)],
        compiler_params=_cparams(("parallel", "parallel")),
        name="rglru",
    )(y, y, y, conv_w, conv_b, w_cat, b_cat, lam)


def _hg_tables():
    c = HG_CHUNK
    masks, pairs = [], []
    for d in range(2):
        r = np.arange(c) if d == 0 else c - 1 - np.arange(c)
        rt, rs = r[:, None], r[None, :]
        m = [rs <= rt, rs > rt]
        pm = []
        w = c // 2
        while w >= 1:
            bnd = w * (2 * (rt // (2 * w)) + 1)
            odd = (rt // w) % 2 == 1
            m.append(np.where(odd, (bnd <= rs) & (rs <= rt), (rt < rs) & (rs <= bnd - 1)))
            pm.append(odd & ((rs // w) % 2 == 0) & (rt // (2 * w) == rs // (2 * w)))
            w //= 2
        pm.append(rt == rs)
        m.append(np.ones((8, c), bool))
        masks.append(np.concatenate(m, axis=0))
        pairs.append(np.stack(pm))
    return np.stack(masks).astype(np.float32), np.stack(pairs).astype(np.float32)


def _hg_kernel(n_ctx_blk, n_levels, q_ref, f_ref, v_ref, lb_ref, mall_ref, pm_ref, o_ref, s_scr):
    c, hd = HG_CHUNK, HEAD_DIM
    d = pl.program_id(0)
    st = pl.program_id(2)
    n_chunks = HG_BLOCK // c

    @pl.when(st == 0)
    def _():
        s_scr[...] = jnp.zeros_like(s_scr)

    def make_chunk(with_out):
        def chunk(i, carry):
            ci = jnp.where(d == 0, i, n_chunks - 1 - i)
            r0 = pl.multiple_of(ci * c, c)
            mall = mall_ref[0]
            heads = range(HEADS)
            sl = [slice(h * hd, (h + 1) * hd) for h in heads]
            ks, cs2s = [], []
            for h in heads:
                lb = lb_ref[0, :, sl[h]]
                f = lb + (1.0 - lb) * _sigmoid(f_ref[pl.ds(r0, c), sl[h]])
                logf = jnp.log(f)
                lf_hi = logf.astype(BF16)
                lf_lo = (logf - lf_hi.astype(F32)).astype(BF16)
                ks.append(1.0 - f)
                cs2s.append(_dot(mall, jnp.concatenate([lf_hi, lf_lo], axis=1)))
            es = [jnp.exp(cs2[:, :hd] + cs2[:, hd:]) for cs2 in cs2s]
            rows = lambda h, n: es[h][n * c:(n + 1) * c]
            v16s = [v_ref[pl.ds(r0, c), sl[h]].astype(BF16) for h in heads]
            s_ts = [s_scr[h] for h in heads]
            if with_out:
                qs = []
                for h in heads:
                    qx = q_ref[pl.ds(r0, c), sl[h]]
                    qs.append(qx * _sigmoid(qx))
                a_s = []
                for h in heads:
                    a = pm_ref[0, n_levels] * _dot_nt(qs[h].astype(BF16), ks[h].astype(BF16))
                    for lv in range(n_levels):
                        el = rows(h, 2 + lv)
                        a += pm_ref[0, lv] * _dot_nt((qs[h] * el).astype(BF16), (ks[h] * el).astype(BF16))
                    a_s.append(a.astype(BF16))
                for h in heads:
                    o_ref[0, pl.ds(r0, c), sl[h]] = (
                        _dot(a_s[h], v16s[h]) + _dot_nt((qs[h] * rows(h, 0)).astype(BF16), s_ts[h].astype(BF16)))
            base = (2 + n_levels) * c
            for h in heads:
                s_scr[h] = es[h][base:base + 1] * s_ts[h] + _dot_tn(v16s[h], (ks[h] * rows(h, 1)).astype(BF16))
            return carry
        return chunk

    @pl.when(st >= n_ctx_blk)
    def _():
        lax.fori_loop(0, n_chunks, make_chunk(True), 0)

    @pl.when(st < n_ctx_blk)
    def _():
        lax.fori_loop(0, n_chunks, make_chunk(False), 0)


def _hgrn(y, lb, batch, seq, ctx_len):
    rb = HG_BLOCK
    n_ctx_blk = ctx_len // rb
    n_lat_blk = seq // rb
    nl = batch * seq
    mall_np, pm_np = _hg_tables()
    n_levels = pm_np.shape[1] - 1
    mall = jnp.asarray(mall_np, BF16)
    pm = jnp.asarray(pm_np, F32)
    cols = 1

    def lat_blk(d, st):
        lc = jnp.maximum(st - n_ctx_blk, 0)
        return jnp.where(d == 0, lc, n_lat_blk - 1 - lc)

    def row_blk(d, b, st):
        cc = jnp.where(d == 0, st, n_ctx_blk - 1 - st)
        return jnp.where(st < n_ctx_blk, nl // rb + b * n_ctx_blk + cc, b * n_lat_blk + lat_blk(d, st))

    return pl.pallas_call(
        functools.partial(_hg_kernel, n_ctx_blk, n_levels),
        grid=(2, batch, n_ctx_blk + n_lat_blk),
        in_specs=[pl.BlockSpec((rb, HALF), lambda d, b, st: (row_blk(d, b, st), 2 * cols)),
                  pl.BlockSpec((rb, HALF), lambda d, b, st: (row_blk(d, b, st), 3 * cols + d)),
                  pl.BlockSpec((rb, HALF), lambda d, b, st: (row_blk(d, b, st), 5 * cols)),
                  pl.BlockSpec((1, 1, HALF), lambda d, b, st: (d, 0, 0)),
                  pl.BlockSpec((1,) + mall_np.shape[1:], lambda d, b, st: (d, 0, 0)),
                  pl.BlockSpec((1,) + pm_np.shape[1:], lambda d, b, st: (d, 0, 0, 0))],
        out_specs=pl.BlockSpec((1, rb, HALF), lambda d, b, st: (d, b * n_lat_blk + lat_blk(d, st), 0)),
        out_shape=jax.ShapeDtypeStruct((2, nl, HALF), F32),
        scratch_shapes=[pltpu.VMEM((HEADS, HEAD_DIM, HEAD_DIM), F32)],
        compiler_params=_cparams(("parallel", "parallel", "arbitrary")),
        name="hgrn2",
    )(y, y, y, lb, mall, pm)


def _hgo_kernel(o_ref, g_ref, w_ref, y_ref):
    hd = HEAD_DIM
    for h in range(HEADS):
        lo, hi = h * hd, (h + 1) * hd
        o = o_ref[0, :, lo:hi] + o_ref[1, :, lo:hi]
        n = o * lax.rsqrt(jnp.mean(o * o, axis=-1, keepdims=True) + EPS) * w_ref[:, lo:hi]
        g = g_ref[:, lo:hi]
        y_ref[:, lo:hi] = (n * (g * _sigmoid(g))).astype(y_ref.dtype)


def _hg_out(o, y, norm_w):
    _, nl, w = o.shape
    tm = _pick(nl, (512, 256))
    return pl.pallas_call(
        _hgo_kernel,
        grid=(nl // tm,),
        in_specs=[pl.BlockSpec((2, tm, w), lambda i: (0, i, 0)),
                  pl.BlockSpec((tm, w), lambda i: (i, 6)),
                  pl.BlockSpec((1, w), lambda i: (0, 0))],
        out_specs=pl.BlockSpec((tm, w), lambda i: (i, 0)),
        out_shape=jax.ShapeDtypeStruct((nl, w), BF16),
        compiler_params=_cparams(("parallel",)),
        name="hgrn2_out",
    )(o, y, norm_w.reshape(1, w))


def _rot_cols(w):
    x1, x2 = jnp.split(w, 2, axis=-1)
    return jnp.concatenate([-x2, x1], axis=-1)


def _pad_lanes(w):
    return jnp.pad(w, [(0, 0)] * (w.ndim - 1) + [(0, LANES - w.shape[-1])])


def _attn_weights(w_in, uq, ukv):
    d = w_in.shape[0]
    kr_w = w_in[:, -MLA_ROPE:]
    w_in_p = jnp.concatenate([w_in[:, :-MLA_ROPE], _pad_lanes(kr_w), _pad_lanes(_rot_cols(kr_w))], axis=1)
    uq3 = uq.reshape(MLA_RANK, HEADS, MLA_QK)
    nope, rope = uq3[..., :HEAD_DIM], uq3[..., HEAD_DIM:]
    uq_p = jnp.concatenate([nope.reshape(MLA_RANK, HALF), _pad_lanes(rope).reshape(MLA_RANK, HALF),
                            _pad_lanes(_rot_cols(rope)).reshape(MLA_RANK, HALF)], axis=1)
    ukv3 = ukv.reshape(MLA_RANK, HEADS, 2 * HEAD_DIM)
    ukv_p = jnp.concatenate([ukv3[..., :HEAD_DIM].reshape(MLA_RANK, HALF),
                             ukv3[..., HEAD_DIM:].reshape(MLA_RANK, HALF)], axis=1)
    del d
    return w_in_p.astype(BF16), uq_p.astype(BF16), ukv_p.astype(BF16)


def _rope_tables(seq, ctx_len):
    rows = seq // GRID_W
    row_id = jnp.repeat(jnp.arange(rows), GRID_W).astype(F32)
    col_id = (jnp.arange(seq) % GRID_W).astype(F32)

    def table(dim):
        quarter = dim // 4
        inv_freq = ROPE_THETA ** (-jnp.arange(quarter, dtype=F32) / quarter)
        ang = jnp.concatenate([row_id[:, None] * inv_freq, col_id[:, None] * inv_freq], axis=-1)
        cos, sin = jnp.cos(ang), jnp.sin(ang)
        cos2 = jnp.concatenate([cos, cos], axis=-1)
        cos2 = jnp.concatenate([cos2, jnp.ones((ctx_len, dim), F32)], axis=0)
        return cos2, cos, sin

    ca, _, sin_a = table(HEAD_DIM)
    sna = jnp.concatenate([jnp.concatenate([-sin_a, sin_a], axis=-1), jnp.zeros((ctx_len, HEAD_DIM), F32)], axis=0)
    cb, _, sin_b = table(MLA_ROPE)
    snb = jnp.concatenate([jnp.concatenate([sin_b, sin_b], axis=-1), jnp.zeros((ctx_len, MLA_ROPE), F32)], axis=0)
    return ca, sna, _pad_lanes(cb), _pad_lanes(snb)


def kernel(x, c, ctx, c_ctx, mod_w, mod_b, norm_mix_w, norm_ffn_w, mix_out_w, ffn_gate_w, ffn_up_w, ffn_down_w, attn_in_w, gqa_q_norm_w, gqa_k_norm_w, mla_q_norm_w, mla_uq_w, mla_kv_norm_w, mla_ukv_w, rec_in_w, lru_conv_w, lru_conv_b, lru_ra_w, lru_ra_b, lru_ix_w, lru_ix_b, lru_lambda, hgrn_lb_logits, hgrn_norm_w, final_norm_w):
    batch, seq, d = x.shape
    ctx_len = ctx.shape[1]
    depth = mod_w.shape[0]
    assert depth == 2 and batch < 8 and seq % ctx_len == 0 and ctx_len == HG_BLOCK
    nl = batch * seq
    rows_all = nl + batch * ctx_len

    cc = jnp.zeros((8, d), F32).at[:batch].set(c).at[batch].set(c_ctx)
    mods = _modulation(cc, mod_w, mod_b).reshape(depth * 8 * MOD_CHUNKS, 1, d)
    x_rows, ctx_rows = x.reshape(nl, d), ctx.reshape(batch * ctx_len, d)
    nm = functools.partial(_norm_mod_matmul, seq=seq, batch=batch)
    mr = functools.partial(_matmul_residual, seq=seq, batch=batch)

    def ffn(h_in, rows, l):
        base = l * 8 * MOD_CHUNKS
        hid = nm([h_in], rows, norm_ffn_w[l], mods, base + 3, [ffn_gate_w, ffn_up_w], l, "swiglu", BF16,
                 tn_prefs=(512, 256, 128), name=f"ffn_up{l}")
        return mr([hid], [(ffn_down_w, l, 0)], [h_in], rows, mods, base + 5, tn_prefs=(256, 128),
                  name=f"ffn_down{l}")

    w_in_p, uq_p, ukv_p = _attn_weights(attn_in_w[0], mla_uq_w[0], mla_ukv_w[0])
    y = nm([x_rows, ctx_rows], rows_all, norm_mix_w[0], mods, 0, [w_in_p[None]], 0, "plain", F32,
           tn_prefs=(1408, 704, 256, 128), name="attn_in")
    qg, kg, vg, qm, km, vm = _attn_prep(
        y, gqa_q_norm_w[0].reshape(1, -1), gqa_k_norm_w[0].reshape(1, -1), mla_q_norm_w[0].reshape(1, -1),
        mla_kv_norm_w[0].reshape(1, -1), uq_p, ukv_p, _rope_tables(seq, ctx_len), batch, seq, ctx_len)
    og = _attention(qg, kg, vg, GQA_GROUP, 1, batch, seq, ctx_len, "gqa")
    om = _attention(qm, km, vm, 2, 2, batch, seq, ctx_len, "mla")
    h = mr([og, om], [(mix_out_w, 0, 0), (mix_out_w, 0, 1)], [x_rows, ctx_rows], rows_all, mods, 2,
           tn_prefs=(512, 256, 128), name="mix_out0")
    h = ffn(h, rows_all, 0)

    base = 8 * MOD_CHUNKS
    y = nm([h], rows_all, norm_mix_w[1], mods, base, [rec_in_w], 0, "plain", F32,
           tn_prefs=(1024, 512, 256, 128), name="rec_in")
    w_cat = jnp.concatenate([lru_ra_w[0, 0], lru_ix_w[0, 0], lru_ra_w[0, 1], lru_ix_w[0, 1]], axis=-1).astype(BF16)
    b_cat = jnp.concatenate([lru_ra_b[0, 0], lru_ix_b[0, 0], lru_ra_b[0, 1], lru_ix_b[0, 1]], axis=-1)[:, None, :]
    y_lru = _lru(y, lru_conv_w[0], lru_conv_b[0].reshape(1, -1), w_cat, b_cat, lru_lambda[0], batch, seq, ctx_len)
    lb_all = jnp.cumsum(jax.nn.softmax(hgrn_lb_logits.astype(F32), axis=1), axis=1)
    lb = (lb_all - lb_all[:, :1])[:, 1].reshape(2, 1, HALF)
    o_hg = _hgrn(y, lb, batch, seq, ctx_len)
    y_hg = _hg_out(o_hg, y, hgrn_norm_w[0])
    h = mr([y_lru, y_hg], [(mix_out_w, 1, 0), (mix_out_w, 1, 1)], [h], nl, mods, base + 2,
           tn_prefs=(512, 256, 128), name="mix_out1")
    h = ffn(h, nl, 1)
    return _final_norm(h, final_norm_w).reshape(batch, seq, d)
```

```python
import functools
import math

import numpy as np
import jax
import jax.numpy as jnp
from jax import lax
from jax.experimental import pallas as pl
from jax.experimental.pallas import tpu as pltpu

F32 = jnp.float32
BF16 = jnp.bfloat16

EPS = 1e-6
ROPE_THETA = 10000.0
GRID_W = 64
MOD_CHUNKS = 6
LANES = 128
HEADS = 8
HEAD_DIM = 128
GQA_KV_HEADS = 2
GQA_GROUP = HEADS // GQA_KV_HEADS
MLA_RANK = 512
MLA_ROPE = 64
MLA_QK = HEAD_DIM + MLA_ROPE
HALF = HEADS * HEAD_DIM
LRU_C = 8.0
HG_CHUNK = 64
HG_BLOCK = 256
VMEM_LIMIT = 56 * 1024 * 1024


def _pick(n, prefs):
    for p in prefs:
        if n % p == 0:
            return p
    raise ValueError(f"no tile for {n} in {prefs}")


def _cparams(sem):
    return pltpu.CompilerParams(dimension_semantics=sem, vmem_limit_bytes=VMEM_LIMIT)


def _dot(a, b):
    return jnp.dot(a, b, preferred_element_type=F32)


def _dot_nt(a, b):
    return lax.dot_general(a, b, (((1,), (1,)), ((), ())), preferred_element_type=F32)


def _dot_tn(a, b):
    return lax.dot_general(a, b, (((0,), (0,)), ((), ())), preferred_element_type=F32)


def _sigmoid(x):
    return 1.0 / (1.0 + jnp.exp(-x))


def _mod_kernel(c_ref, w_ref, b_ref, o_ref):
    c = c_ref[...]
    o_ref[0] = _dot(c * _sigmoid(c), w_ref[0]) + b_ref[0]


def _modulation(cc, mod_w, mod_b):
    depth, d, n = mod_w.shape
    tn = _pick(n, (1024, 512, 256, 128))
    return pl.pallas_call(
        _mod_kernel,
        grid=(depth, n // tn),
        in_specs=[pl.BlockSpec((8, d), lambda l, j: (0, 0)),
                  pl.BlockSpec((1, d, tn), lambda l, j: (l, 0, j)),
                  pl.BlockSpec((1, 1, tn), lambda l, j: (l, 0, j))],
        out_specs=pl.BlockSpec((1, 8, tn), lambda l, j: (l, 0, j)),
        out_shape=jax.ShapeDtypeStruct((depth, 8, n), F32),
        compiler_params=_cparams(("parallel", "parallel")),
        name="modulation",
    )(cc, mod_w, mod_b.reshape(depth, 1, n))


def _row_sources(h_parts, tm):
    n1 = h_parts[0].shape[0] // tm
    if len(h_parts) == 1:
        return n1, [lambda i: i]
    return n1, [lambda i: jnp.minimum(i, n1 - 1), lambda i: jnp.maximum(i - n1, 0)]


def _nm_kernel(n_h, n_w, n1, epilogue, tm, rc, *refs):
    h_refs = refs[:n_h]
    nw_ref, sh_ref, sc_ref = refs[n_h:n_h + 3]
    w_refs = refs[n_h + 3:n_h + 3 + n_w]
    o_ref, u_scr = refs[n_h + 3 + n_w:]
    i = pl.program_id(0)

    def prologue(h_ref):
        def chunk(r, carry):
            r0 = pl.multiple_of(r * rc, rc)
            x = h_ref[pl.ds(r0, rc), :]
            ms = jnp.mean(x * x, axis=-1, keepdims=True)
            y = x * lax.rsqrt(ms + EPS) * nw_ref[...]
            u_scr[pl.ds(r0, rc), :] = (y * (1.0 + sc_ref[0]) + sh_ref[0]).astype(BF16)
            return carry
        lax.fori_loop(0, tm // rc, chunk, 0)

    first = pl.program_id(1) == 0
    if n_h == 1:
        pl.when(first)(lambda: prologue(h_refs[0]))
    else:
        pl.when(first & (i < n1))(lambda: prologue(h_refs[0]))
        pl.when(first & (i >= n1))(lambda: prologue(h_refs[1]))

    u = u_scr[...]
    if epilogue == "plain":
        o_ref[...] = _dot(u, w_refs[0][0].astype(BF16)).astype(o_ref.dtype)
    else:
        g = _dot(u, w_refs[0][0].astype(BF16))
        up = _dot(u, w_refs[1][0].astype(BF16))
        o_ref[...] = (g * _sigmoid(g) * up).astype(o_ref.dtype)


def _norm_mod_matmul(h_parts, rows, norm_w, mods, mod_base, ws, layer, epilogue, out_dtype, seq, batch, tn_prefs, name):
    d = h_parts[0].shape[1]
    n = ws[0].shape[2]
    tm = _pick(math.gcd(seq, rows), (1024, 512, 256) if len(h_parts) == 1 else (512, 256))
    tn = _pick(n, tn_prefs)
    rc = 32
    n1, rmaps = _row_sources(h_parts, tm)

    def mrow(i, k):
        return mod_base + jnp.minimum(i * tm // seq, batch) * MOD_CHUNKS + k

    in_specs = [pl.BlockSpec((tm, d), lambda i, j, m=m: (m(i), 0)) for m in rmaps]
    in_specs += [pl.BlockSpec((1, d), lambda i, j: (0, 0)),
                 pl.BlockSpec((1, 1, d), lambda i, j: (mrow(i, 0), 0, 0)),
                 pl.BlockSpec((1, 1, d), lambda i, j: (mrow(i, 1), 0, 0))]
    in_specs += [pl.BlockSpec((1, d, tn), lambda i, j: (layer, 0, j)) for _ in ws]
    return pl.pallas_call(
        functools.partial(_nm_kernel, len(h_parts), len(ws), n1, epilogue, tm, rc),
        grid=(rows // tm, n // tn),
        in_specs=in_specs,
        out_specs=pl.BlockSpec((tm, tn), lambda i, j: (i, j)),
        out_shape=jax.ShapeDtypeStruct((rows, n), out_dtype),
        scratch_shapes=[pltpu.VMEM((tm, d), BF16)],
        compiler_params=_cparams(("parallel", "arbitrary")),
        name=name,
    )(*h_parts, norm_w.reshape(1, d), mods, mods, *ws)


def _um_kernel(n_w, epilogue, u_ref, *refs):
    w_refs, o_ref = refs[:n_w], refs[n_w]
    u = u_ref[...]
    if epilogue == "plain":
        o_ref[...] = _dot(u, w_refs[0][0].astype(BF16)).astype(o_ref.dtype)
    else:
        g = _dot(u, w_refs[0][0].astype(BF16))
        up = _dot(u, w_refs[1][0].astype(BF16))
        o_ref[...] = (g * _sigmoid(g) * up).astype(o_ref.dtype)


def _u_matmul(u, rows, ws, layer, epilogue, out_dtype, seq, tn_prefs, name):
    d = u.shape[1]
    n = ws[0].shape[2]
    tm = _pick(math.gcd(seq, rows), (1024, 512, 256))
    tn = _pick(n, tn_prefs)
    return pl.pallas_call(
        functools.partial(_um_kernel, len(ws), epilogue),
        grid=(rows // tm, n // tn),
        in_specs=[pl.BlockSpec((tm, d), lambda i, j: (i, 0))]
        + [pl.BlockSpec((1, d, tn), lambda i, j: (layer, 0, j)) for _ in ws],
        out_specs=pl.BlockSpec((tm, tn), lambda i, j: (i, j)),
        out_shape=jax.ShapeDtypeStruct((rows, n), out_dtype),
        compiler_params=_cparams(("parallel", "arbitrary")),
        name=name,
    )(u, *ws)


def _mrf_kernel(n_a, n_r, n1, mode, *refs):
    a_refs, w_refs = refs[:n_a], refs[n_a:2 * n_a]
    res_refs = refs[2 * n_a:2 * n_a + n_r]
    rest = refs[2 * n_a + n_r:]
    if mode == "next":
        g_ref, nw_ref, sh_ref, sc_ref, h_ref, u_ref = rest
    else:
        g_ref, nw_ref, o_ref = rest
    acc = _dot(a_refs[0][...], w_refs[0][0])
    for a, w in zip(a_refs[1:], w_refs[1:]):
        acc += _dot(a[...], w[0])
    upd = g_ref[0] * acc

    def finish(res_ref):
        h = res_ref[...] + upd
        y = h * lax.rsqrt(jnp.mean(h * h, axis=-1, keepdims=True) + EPS) * nw_ref[...]
        if mode == "next":
            h_ref[...] = h
            u_ref[...] = (y * (1.0 + sc_ref[0]) + sh_ref[0]).astype(BF16)
        else:
            o_ref[...] = y

    if n_r == 1:
        finish(res_refs[0])
    else:
        i = pl.program_id(0)
        pl.when(i < n1)(lambda: finish(res_refs[0]))
        pl.when(i >= n1)(lambda: finish(res_refs[1]))


def _matmul_residual_norm(a_list, w_specs, res_parts, rows, mods, gate_row0, norm_w, next_row0, seq, batch, tm_prefs, name):
    d = res_parts[0].shape[1]
    tm = _pick(math.gcd(seq, rows), tm_prefs)
    n1, rmaps = _row_sources(res_parts, tm)
    mode = "final" if next_row0 is None else "next"

    def mrow(i, base):
        return base + jnp.minimum(i * tm // seq, batch) * MOD_CHUNKS

    row_tile = lambda w: pl.BlockSpec((tm, w), lambda i: (i, 0))
    vec = lambda base: pl.BlockSpec((1, 1, d), lambda i: (mrow(i, base), 0, 0))
    in_specs = [row_tile(a.shape[1]) for a in a_list]
    in_specs += [pl.BlockSpec((1, a.shape[1], d), lambda i, l=l, kb=kb: (l, kb, 0), pipeline_mode=pl.Buffered(1))
                 for a, (_, l, kb) in zip(a_list, w_specs)]
    in_specs += [pl.BlockSpec((tm, d), lambda i, m=m: (m(i), 0)) for m in rmaps]
    in_specs += [vec(gate_row0), pl.BlockSpec((1, d), lambda i: (0, 0))]
    args = [*a_list, *[w for w, _, _ in w_specs], *res_parts, mods, norm_w.reshape(1, d)]
    if mode == "next":
        in_specs += [vec(next_row0), vec(next_row0 + 1)]
        args += [mods, mods]
        out_specs = [row_tile(d), row_tile(d)]
        out_shape = [jax.ShapeDtypeStruct((rows, d), F32), jax.ShapeDtypeStruct((rows, d), BF16)]
    else:
        out_specs = row_tile(d)
        out_shape = jax.ShapeDtypeStruct((rows, d), F32)
    return pl.pallas_call(
        functools.partial(_mrf_kernel, len(a_list), len(res_parts), n1, mode),
        grid=(rows // tm,),
        in_specs=in_specs,
        out_specs=out_specs,
        out_shape=out_shape,
        compiler_params=_cparams(("parallel",)),
        name=name,
    )(*args)


def _prep_kernel(sa, sb, y_ref, qw_ref, kw_ref, mqw_ref, mkw_ref, uq_ref, ukv_ref,
                 ca_ref, sna_ref, cb_ref, snb_ref,
                 qg_ref, kg_ref, vg_ref, qm_ref, km_ref, vm_ref):
    hd = HEAD_DIM
    ca, sna, cb, snb = ca_ref[...], sna_ref[...], cb_ref[...], snb_ref[...]

    def head_norm_rope(x, w):
        r = lax.rsqrt(jnp.mean(x * x, axis=-1, keepdims=True) + EPS)
        yh = x * r * w
        return yh * ca + pltpu.roll(yh, hd // 2, axis=1) * sna

    o = 0
    for h in range(HEADS):
        qg_ref[0, h] = (head_norm_rope(y_ref[:, o + h * hd:o + (h + 1) * hd], qw_ref[...]) * sa).astype(BF16)
    o += HALF
    for h in range(GQA_KV_HEADS):
        kg_ref[0, h] = head_norm_rope(y_ref[:, o + h * hd:o + (h + 1) * hd], kw_ref[...]).astype(BF16)
    o += GQA_KV_HEADS * hd
    ones_col = (lax.broadcasted_iota(jnp.int32, (y_ref.shape[0], hd), 1) == 0).astype(BF16)
    for h in range(GQA_KV_HEADS):
        vg_ref[0, h, :, :hd] = y_ref[:, o + h * hd:o + (h + 1) * hd].astype(BF16)
        vg_ref[0, h, :, hd:] = ones_col
    o += GQA_KV_HEADS * hd

    def rms(x, w):
        return (x * lax.rsqrt(jnp.mean(x * x, axis=-1, keepdims=True) + EPS) * w).astype(BF16)

    qb = _dot(rms(y_ref[:, o:o + MLA_RANK], mqw_ref[...]), uq_ref[...])
    o += MLA_RANK
    kvb = _dot(rms(y_ref[:, o:o + MLA_RANK], mkw_ref[...]), ukv_ref[...])
    o += MLA_RANK
    kr = (y_ref[:, o:o + hd] * cb + y_ref[:, o + hd:o + 2 * hd] * snb).astype(BF16)
    for h in range(HEADS):
        lo, hi = h * hd, (h + 1) * hd
        qm_ref[0, h, :, :hd] = (qb[:, lo:hi] * sb).astype(BF16)
        qm_ref[0, h, :, hd:] = ((qb[:, HALF + lo:HALF + hi] * cb + qb[:, 2 * HALF + lo:2 * HALF + hi] * snb) * sb).astype(BF16)
        km_ref[0, h, :, :hd] = kvb[:, lo:hi].astype(BF16)
        km_ref[0, h, :, hd:] = kr
        vm_ref[0, h, :, :hd] = kvb[:, HALF + lo:HALF + hi].astype(BF16)
        vm_ref[0, h, :, hd:] = ones_col


def _attn_prep(y, qw, kw, mqw, mkw, uq, ukv, tabs, batch, seq, ctx_len):
    rows, n = y.shape
    ts = ctx_len
    nl = seq // ts
    n_lat = batch * nl
    t_all = seq + ctx_len
    hd = HEAD_DIM

    def bidx(t):
        return jnp.where(t < n_lat, t // nl, t - n_lat)

    def sidx(t):
        return jnp.where(t < n_lat, t % nl + 1, 0)

    def ridx(t):
        return jnp.where(t < n_lat, t % nl, nl)

    def hm(width, heads):
        return pl.BlockSpec((1, heads, ts, width), lambda t: (bidx(t), 0, sidx(t), 0))

    full = lambda a: pl.BlockSpec(a.shape, lambda t: (0,) * a.ndim)
    tab = pl.BlockSpec((ts, hd), lambda t: (ridx(t), 0))
    outs = [((batch, HEADS, t_all, hd), hm(hd, HEADS)),
            ((batch, GQA_KV_HEADS, t_all, hd), hm(hd, GQA_KV_HEADS)),
            ((batch, GQA_KV_HEADS, t_all, 2 * hd), hm(2 * hd, GQA_KV_HEADS)),
            ((batch, HEADS, t_all, 2 * hd), hm(2 * hd, HEADS)),
            ((batch, HEADS, t_all, 2 * hd), hm(2 * hd, HEADS)),
            ((batch, HEADS, t_all, 2 * hd), hm(2 * hd, HEADS))]
    return pl.pallas_call(
        functools.partial(_prep_kernel, hd ** -0.5, MLA_QK ** -0.5),
        grid=(rows // ts,),
        in_specs=[pl.BlockSpec((ts, n), lambda t: (t, 0)), full(qw), full(kw), full(mqw), full(mkw),
                  full(uq), full(ukv), tab, tab, tab, tab],
        out_specs=[s for _, s in outs],
        out_shape=[jax.ShapeDtypeStruct(sh, BF16) for sh, _ in outs],
        compiler_params=_cparams(("parallel",)),
        name="attn_prep",
    )(y, qw, kw, mqw, mkw, uq, ukv, *tabs)


def _attn_kernel(group, kv_group, ctx_len, q_ref, k_ref, v_ref, o_ref):
    dv = HEAD_DIM

    def run(t_k):
        kv = lambda g: g if kv_group > 1 else 0
        ss = [_dot_nt(q_ref[0, g], k_ref[0, kv(g), :t_k, :]) for g in range(group)]
        for g, s in enumerate(ss):
            p = jnp.exp(s - jnp.max(s, axis=-1, keepdims=True))
            ov = _dot(p.astype(BF16), v_ref[0, kv(g), :t_k, :])
            o_ref[:, g * dv:(g + 1) * dv] = (ov[:, :dv] / ov[:, dv:dv + 1]).astype(o_ref.dtype)

    is_ctx = pl.program_id(2) == 0

    @pl.when(is_ctx)
    def _():
        run(ctx_len)

    @pl.when(jnp.logical_not(is_ctx))
    def _():
        run(k_ref.shape[2])


def _attention(q, k, v, group, kv_group, batch, seq, ctx_len, name):
    _, hq, t_all, dq = q.shape
    dvp = v.shape[-1]
    tq = ctx_len
    nq = seq // tq
    rows = batch * t_all

    def orow(b, qi):
        return jnp.where(qi == 0, batch * nq + b, b * nq + qi - 1)

    return pl.pallas_call(
        functools.partial(_attn_kernel, group, kv_group, ctx_len),
        grid=(batch, hq // group, nq + 1),
        in_specs=[pl.BlockSpec((1, group, tq, dq), lambda b, h, qi: (b, h, qi, 0)),
                  pl.BlockSpec((1, kv_group, t_all, dq), lambda b, h, qi: (b, h, 0, 0)),
                  pl.BlockSpec((1, kv_group, t_all, dvp), lambda b, h, qi: (b, h, 0, 0))],
        out_specs=pl.BlockSpec((tq, group * HEAD_DIM), lambda b, h, qi: (orow(b, qi), h)),
        out_shape=jax.ShapeDtypeStruct((rows, hq * HEAD_DIM), BF16),
        compiler_params=_cparams(("parallel", "parallel", "arbitrary")),
        name=name,
    )(q, k, v)


def _gelu(x):
    return 0.5 * x * (1.0 + jnp.tanh(0.7978845608028654 * (x + 0.044715 * x * x * x)))


def _lru_kernel(seq, ctx_len, rc, xl_ref, xc_ref, gate_ref, cw_ref, cb_ref, w_ref, b_ref, lam_ref,
                o_ref, xp_scr, af_scr, bf_scr, ab_scr, bb_scr, hs_scr):
    cb_w = xl_ref.shape[1]
    nblk = cb_w // LANES
    lam = lam_ref[...]
    sp = jnp.maximum(-lam, 0.0) + jnp.log(1.0 + jnp.exp(-jnp.abs(lam)))
    cw = cw_ref[...]
    cbias = cb_ref[...]
    zeros8 = jnp.zeros((8, cb_w), F32)

    def coeffs(src_ref, n_rows, dst0):
        xp_scr[pl.ds(0, 8), :] = zeros8
        xp_scr[pl.ds(8 + n_rows, 8), :] = zeros8

        def cp(r, carry):
            r0 = pl.multiple_of(r * rc, rc)
            xp_scr[pl.ds(8 + r0, rc), :] = src_ref[pl.ds(r0, rc), :]
            return carry
        lax.fori_loop(0, n_rows // rc, cp, 0)

        def chunk(r, carry):
            r0 = pl.multiple_of(r * rc, rc)
            win = xp_scr[pl.ds(r0, rc + 16), :]
            xc = cbias + cw[2:3] * win[8:8 + rc]
            xc += cw[0:1] * pltpu.roll(win, 2, axis=0)[8:8 + rc]
            xc += cw[1:2] * pltpu.roll(win, 1, axis=0)[8:8 + rc]
            xc += cw[3:4] * pltpu.roll(win, rc + 15, axis=0)[8:8 + rc]
            for n in range(nblk):
                lo, hi = n * LANES, (n + 1) * LANES
                xcn = xc[:, lo:hi]
                z = _dot(xcn.astype(BF16), w_ref[n]) + b_ref[n]
                for d, (a_scr, b_scr) in enumerate(((af_scr, bf_scr), (ab_scr, bb_scr))):
                    r_g = _sigmoid(z[:, (2 * d) * LANES:(2 * d + 1) * LANES])
                    i_g = _sigmoid(z[:, (2 * d + 1) * LANES:(2 * d + 2) * LANES])
                    a = jnp.exp(-LRU_C * r_g * sp[d:d + 1, lo:hi])
                    a_scr[pl.ds(dst0 + r0, rc), lo:hi] = a
                    b_scr[pl.ds(dst0 + r0, rc), lo:hi] = jnp.sqrt(1.0 - a * a) * (i_g * xcn)
            return carry
        lax.fori_loop(0, n_rows // rc, chunk, 0)

    coeffs(xc_ref, ctx_len, 0)
    coeffs(xl_ref, seq, ctx_len)
    t_all = seq + ctx_len

    row = lax.broadcasted_iota(jnp.int32, (8, cb_w), 0)

    def group_scan(a, b, h_in, reverse):
        for sh in (1, 2, 4):
            keep = (row < 8 - sh) if reverse else (row >= sh)
            rot = (8 - sh) if reverse else sh
            a_s = jnp.where(keep, pltpu.roll(a, rot, axis=0), 1.0)
            b_s = jnp.where(keep, pltpu.roll(b, rot, axis=0), 0.0)
            b = a * b_s + b
            a = a * a_s
        h = a * h_in + b
        edge = 0 if reverse else 7
        return h, jnp.broadcast_to(h[edge:edge + 1], (8, cb_w))

    n_groups = t_all // 8
    n_ctx_groups = ctx_len // 8

    def step(g, carry):
        hf_in, hb_in = carry
        r0 = pl.multiple_of(g * 8, 8)
        hf, hf_in = group_scan(af_scr[pl.ds(r0, 8), :], bf_scr[pl.ds(r0, 8), :], hf_in, False)
        hs_scr[pl.ds(r0, 8), :] = hf
        gb = jnp.where(g < n_ctx_groups, n_ctx_groups - 1 - g, n_groups + n_ctx_groups - 1 - g)
        rb0 = pl.multiple_of(gb * 8, 8)
        hb, hb_in = group_scan(ab_scr[pl.ds(rb0, 8), :], bb_scr[pl.ds(rb0, 8), :], hb_in, True)
        bb_scr[pl.ds(rb0, 8), :] = hb
        return hf_in, hb_in
    h0 = jnp.zeros((8, cb_w), F32)
    lax.fori_loop(0, n_groups, step, (h0, h0), unroll=2)

    def fin(r, carry):
        r0 = pl.multiple_of(r * rc, rc)
        hsum = hs_scr[pl.ds(ctx_len + r0, rc), :] + bb_scr[pl.ds(ctx_len + r0, rc), :]
        o_ref[pl.ds(r0, rc), :] = (_gelu(gate_ref[pl.ds(r0, rc), :]) * hsum).astype(o_ref.dtype)
        return carry
    lax.fori_loop(0, seq // rc, fin, 0)


def _lru(y, conv_w, conv_b, w_cat, b_cat, lam, batch, seq, ctx_len):
    cb_w = 2 * LANES
    ncb = HALF // cb_w
    nl = batch * seq
    rc = 256
    t_all = seq + ctx_len
    scr = lambda r: pltpu.VMEM((r, cb_w), F32)
    return pl.pallas_call(
        functools.partial(_lru_kernel, seq, ctx_len, rc),
        grid=(batch, ncb),
        in_specs=[pl.BlockSpec((seq, cb_w), lambda b, c: (b, c)),
                  pl.BlockSpec((ctx_len, cb_w), lambda b, c: (nl // ctx_len + b, c)),
                  pl.BlockSpec((seq, cb_w), lambda b, c: (b, ncb + c)),
                  pl.BlockSpec((4, cb_w), lambda b, c: (0, c)),
                  pl.BlockSpec((1, cb_w), lambda b, c: (0, c)),
                  pl.BlockSpec((cb_w // LANES, LANES, 4 * LANES), lambda b, c: (c, 0, 0)),
                  pl.BlockSpec((cb_w // LANES, 1, 4 * LANES), lambda b, c: (c, 0, 0)),
                  pl.BlockSpec((2, cb_w), lambda b, c: (0, c))],
        out_specs=pl.BlockSpec((seq, cb_w), lambda b, c: (b, c)),
        out_shape=jax.ShapeDtypeStruct((nl, HALF), BF16),
        scratch_shapes=[scr(seq + 16), scr(t_all), scr(t_all), scr(t_all), scr(t_all), scr(t_all)],
        compiler_params=_cparams(("parallel", "parallel")),
        name="rglru",
    )(y, y, y, conv_w, conv_b, w_cat, b_cat, lam)


def _hg_tables():
    c = HG_CHUNK
    masks, pairs = [], []
    for d in range(2):
        r = np.arange(c) if d == 0 else c - 1 - np.arange(c)
        rt, rs = r[:, None], r[None, :]
        m = [rs <= rt, rs > rt]
        pm = []
        w = c // 2
        while w >= 1:
            bnd = w * (2 * (rt // (2 * w)) + 1)
            odd = (rt // w) % 2 == 1
            m.append(np.where(odd, (bnd <= rs) & (rs <= rt), (rt < rs) & (rs <= bnd - 1)))
            pm.append(odd & ((rs // w) % 2 == 0) & (rt // (2 * w) == rs // (2 * w)))
            w //= 2
        pm.append(rt == rs)
        m.append(np.ones((8, c), bool))
        masks.append(np.concatenate(m, axis=0))
        pairs.append(np.stack(pm))
    return np.stack(masks).astype(np.float32), np.stack(pairs).astype(np.float32)


def _hg_kernel(n_ctx_blk, n_levels, q_ref, f_ref, v_ref, lb_ref, mall_ref, pm_ref, o_ref, s_scr):
    c, hd = HG_CHUNK, HEAD_DIM
    d = pl.program_id(0)
    st = pl.program_id(2)
    n_chunks = HG_BLOCK // c

    @pl.when(st == 0)
    def _():
        s_scr[...] = jnp.zeros_like(s_scr)

    def make_chunk(with_out):
        def chunk(i, carry):
            ci = jnp.where(d == 0, i, n_chunks - 1 - i)
            r0 = pl.multiple_of(ci * c, c)
            mall = mall_ref[0]
            heads = range(HEADS)
            sl = [slice(h * hd, (h + 1) * hd) for h in heads]
            ks, cs2s = [], []
            for h in heads:
                lb = lb_ref[0, :, sl[h]]
                f = lb + (1.0 - lb) * _sigmoid(f_ref[pl.ds(r0, c), sl[h]])
                logf = jnp.log(f)
                lf_hi = logf.astype(BF16)
                lf_lo = (logf - lf_hi.astype(F32)).astype(BF16)
                ks.append(1.0 - f)
                cs2s.append(_dot(mall, jnp.concatenate([lf_hi, lf_lo], axis=1)))
            es = [jnp.exp(cs2[:, :hd] + cs2[:, hd:]) for cs2 in cs2s]
            rows = lambda h, n: es[h][n * c:(n + 1) * c]
            v16s = [v_ref[pl.ds(r0, c), sl[h]].astype(BF16) for h in heads]
            s_ts = [s_scr[h] for h in heads]
            if with_out:
                qs = []
                for h in heads:
                    qx = q_ref[pl.ds(r0, c), sl[h]]
                    qs.append(qx * _sigmoid(qx))
                a_s = []
                for h in heads:
                    a = pm_ref[0, n_levels] * _dot_nt(qs[h].astype(BF16), ks[h].astype(BF16))
                    for lv in range(n_levels):
                        el = rows(h, 2 + lv)
                        a += pm_ref[0, lv] * _dot_nt((qs[h] * el).astype(BF16), (ks[h] * el).astype(BF16))
                    a_s.append(a.astype(BF16))
                for h in heads:
                    o_ref[0, pl.ds(r0, c), sl[h]] = (
                        _dot(a_s[h], v16s[h]) + _dot_nt((qs[h] * rows(h, 0)).astype(BF16), s_ts[h].astype(BF16)))
            base = (2 + n_levels) * c
            for h in heads:
                s_scr[h] = es[h][base:base + 1] * s_ts[h] + _dot_tn(v16s[h], (ks[h] * rows(h, 1)).astype(BF16))
            return carry
        return chunk

    @pl.when(st >= n_ctx_blk)
    def _():
        lax.fori_loop(0, n_chunks, make_chunk(True), 0, unroll=True)

    @pl.when(st < n_ctx_blk)
    def _():
        lax.fori_loop(0, n_chunks, make_chunk(False), 0)


def _hgrn(y, lb, batch, seq, ctx_len):
    rb = HG_BLOCK
    n_ctx_blk = ctx_len // rb
    n_lat_blk = seq // rb
    nl = batch * seq
    mall_np, pm_np = _hg_tables()
    n_levels = pm_np.shape[1] - 1
    mall = jnp.asarray(mall_np, BF16)
    pm = jnp.asarray(pm_np, F32)
    cols = 1

    def lat_blk(d, st):
        lc = jnp.maximum(st - n_ctx_blk, 0)
        return jnp.where(d == 0, lc, n_lat_blk - 1 - lc)

    def row_blk(d, b, st):
        cc = jnp.where(d == 0, st, n_ctx_blk - 1 - st)
        return jnp.where(st < n_ctx_blk, nl // rb + b * n_ctx_blk + cc, b * n_lat_blk + lat_blk(d, st))

    return pl.pallas_call(
        functools.partial(_hg_kernel, n_ctx_blk, n_levels),
        grid=(2, batch, n_ctx_blk + n_lat_blk),
        in_specs=[pl.BlockSpec((rb, HALF), lambda d, b, st: (row_blk(d, b, st), 2 * cols)),
                  pl.BlockSpec((rb, HALF), lambda d, b, st: (row_blk(d, b, st), 3 * cols + d)),
                  pl.BlockSpec((rb, HALF), lambda d, b, st: (row_blk(d, b, st), 5 * cols)),
                  pl.BlockSpec((1, 1, HALF), lambda d, b, st: (d, 0, 0)),
                  pl.BlockSpec((1,) + mall_np.shape[1:], lambda d, b, st: (d, 0, 0)),
                  pl.BlockSpec((1,) + pm_np.shape[1:], lambda d, b, st: (d, 0, 0, 0))],
        out_specs=pl.BlockSpec((1, rb, HALF), lambda d, b, st: (d, b * n_lat_blk + lat_blk(d, st), 0)),
        out_shape=jax.ShapeDtypeStruct((2, nl, HALF), F32),
        scratch_shapes=[pltpu.VMEM((HEADS, HEAD_DIM, HEAD_DIM), F32)],
        compiler_params=_cparams(("parallel", "parallel", "arbitrary")),
        name="hgrn2",
    )(y, y, y, lb, mall, pm)


def _hgo_kernel(o_ref, g_ref, w_ref, y_ref):
    hd = HEAD_DIM
    for h in range(HEADS):
        lo, hi = h * hd, (h + 1) * hd
        o = o_ref[0, :, lo:hi] + o_ref[1, :, lo:hi]
        n = o * lax.rsqrt(jnp.mean(o * o, axis=-1, keepdims=True) + EPS) * w_ref[:, lo:hi]
        g = g_ref[:, lo:hi]
        y_ref[:, lo:hi] = (n * (g * _sigmoid(g))).astype(y_ref.dtype)


def _hg_out(o, y, norm_w):
    _, nl, w = o.shape
    tm = _pick(nl, (512, 256))
    return pl.pallas_call(
        _hgo_kernel,
        grid=(nl // tm,),
        in_specs=[pl.BlockSpec((2, tm, w), lambda i: (0, i, 0)),
                  pl.BlockSpec((tm, w), lambda i: (i, 6)),
                  pl.BlockSpec((1, w), lambda i: (0, 0))],
        out_specs=pl.BlockSpec((tm, w), lambda i: (i, 0)),
        out_shape=jax.ShapeDtypeStruct((nl, w), BF16),
        compiler_params=_cparams(("parallel",)),
        name="hgrn2_out",
    )(o, y, norm_w.reshape(1, w))


def _rot_cols(w):
    x1, x2 = jnp.split(w, 2, axis=-1)
    return jnp.concatenate([-x2, x1], axis=-1)


def _pad_lanes(w):
    return jnp.pad(w, [(0, 0)] * (w.ndim - 1) + [(0, LANES - w.shape[-1])])


def _attn_weights(w_in, uq, ukv):
    d = w_in.shape[0]
    kr_w = w_in[:, -MLA_ROPE:]
    w_in_p = jnp.concatenate([w_in[:, :-MLA_ROPE], _pad_lanes(kr_w), _pad_lanes(_rot_cols(kr_w))], axis=1)
    uq3 = uq.reshape(MLA_RANK, HEADS, MLA_QK)
    nope, rope = uq3[..., :HEAD_DIM], uq3[..., HEAD_DIM:]
    uq_p = jnp.concatenate([nope.reshape(MLA_RANK, HALF), _pad_lanes(rope).reshape(MLA_RANK, HALF),
                            _pad_lanes(_rot_cols(rope)).reshape(MLA_RANK, HALF)], axis=1)
    ukv3 = ukv.reshape(MLA_RANK, HEADS, 2 * HEAD_DIM)
    ukv_p = jnp.concatenate([ukv3[..., :HEAD_DIM].reshape(MLA_RANK, HALF),
                             ukv3[..., HEAD_DIM:].reshape(MLA_RANK, HALF)], axis=1)
    del d
    return w_in_p.astype(BF16), uq_p.astype(BF16), ukv_p.astype(BF16)


def _rope_tables(seq, ctx_len):
    rows = seq // GRID_W
    row_id = jnp.repeat(jnp.arange(rows), GRID_W).astype(F32)
    col_id = (jnp.arange(seq) % GRID_W).astype(F32)

    def table(dim):
        quarter = dim // 4
        inv_freq = ROPE_THETA ** (-jnp.arange(quarter, dtype=F32) / quarter)
        ang = jnp.concatenate([row_id[:, None] * inv_freq, col_id[:, None] * inv_freq], axis=-1)
        cos, sin = jnp.cos(ang), jnp.sin(ang)
        cos2 = jnp.concatenate([cos, cos], axis=-1)
        cos2 = jnp.concatenate([cos2, jnp.ones((ctx_len, dim), F32)], axis=0)
        return cos2, cos, sin

    ca, _, sin_a = table(HEAD_DIM)
    sna = jnp.concatenate([jnp.concatenate([-sin_a, sin_a], axis=-1), jnp.zeros((ctx_len, HEAD_DIM), F32)], axis=0)
    cb, _, sin_b = table(MLA_ROPE)
    snb = jnp.concatenate([jnp.concatenate([sin_b, sin_b], axis=-1), jnp.zeros((ctx_len, MLA_ROPE), F32)], axis=0)
    return ca, sna, _pad_lanes(cb), _pad_lanes(snb)


def kernel(x, c, ctx, c_ctx, mod_w, mod_b, norm_mix_w, norm_ffn_w, mix_out_w, ffn_gate_w, ffn_up_w, ffn_down_w, attn_in_w, gqa_q_norm_w, gqa_k_norm_w, mla_q_norm_w, mla_uq_w, mla_kv_norm_w, mla_ukv_w, rec_in_w, lru_conv_w, lru_conv_b, lru_ra_w, lru_ra_b, lru_ix_w, lru_ix_b, lru_lambda, hgrn_lb_logits, hgrn_norm_w, final_norm_w):
    batch, seq, d = x.shape
    ctx_len = ctx.shape[1]
    depth = mod_w.shape[0]
    assert depth == 2 and batch < 8 and seq % ctx_len == 0 and ctx_len == HG_BLOCK
    nl = batch * seq
    rows_all = nl + batch * ctx_len

    cc = jnp.zeros((8, d), F32).at[:batch].set(c).at[batch].set(c_ctx)
    mods = _modulation(cc, mod_w, mod_b).reshape(depth * 8 * MOD_CHUNKS, 1, d)
    x_rows, ctx_rows = x.reshape(nl, d), ctx.reshape(batch * ctx_len, d)
    nm = functools.partial(_norm_mod_matmul, seq=seq, batch=batch)
    um = functools.partial(_u_matmul, seq=seq)
    mrn = functools.partial(_matmul_residual_norm, seq=seq, batch=batch)
    mix_w, down_w = mix_out_w.astype(BF16), ffn_down_w.astype(BF16)
    base1 = 8 * MOD_CHUNKS

    def ffn_hidden(u, rows, l):
        return um(u, rows, [ffn_gate_w, ffn_up_w], l, "swiglu", BF16, tn_prefs=(512, 256, 128), name=f"ffn_up{l}")

    w_in_p, uq_p, ukv_p = _attn_weights(attn_in_w[0], mla_uq_w[0], mla_ukv_w[0])
    y = nm([x_rows, ctx_rows], rows_all, norm_mix_w[0], mods, 0, [w_in_p[None]], 0, "plain", F32,
           tn_prefs=(1408, 704, 256, 128), name="attn_in")
    qg, kg, vg, qm, km, vm = _attn_prep(
        y, gqa_q_norm_w[0].reshape(1, -1), gqa_k_norm_w[0].reshape(1, -1), mla_q_norm_w[0].reshape(1, -1),
        mla_kv_norm_w[0].reshape(1, -1), uq_p, ukv_p, _rope_tables(seq, ctx_len), batch, seq, ctx_len)
    og = _attention(qg, kg, vg, GQA_GROUP, 1, batch, seq, ctx_len, "gqa")
    om = _attention(qm, km, vm, 2, 2, batch, seq, ctx_len, "mla")
    h, u = mrn([og, om], [(mix_w, 0, 0), (mix_w, 0, 1)], [x_rows, ctx_rows], rows_all, mods, 2,
               norm_ffn_w[0], 3, tm_prefs=(512, 256), name="mix_out0")
    h, u = mrn([ffn_hidden(u, rows_all, 0)], [(down_w, 0, 0)], [h], rows_all, mods, 5,
               norm_mix_w[1], base1, tm_prefs=(256,), name="ffn_down0")

    y = um(u, rows_all, [rec_in_w], 0, "plain", F32, tn_prefs=(1024, 512, 256, 128), name="rec_in")
    w_cat = jnp.concatenate([lru_ra_w[0, 0], lru_ix_w[0, 0], lru_ra_w[0, 1], lru_ix_w[0, 1]], axis=-1).astype(BF16)
    b_cat = jnp.concatenate([lru_ra_b[0, 0], lru_ix_b[0, 0], lru_ra_b[0, 1], lru_ix_b[0, 1]], axis=-1)[:, None, :]
    y_lru = _lru(y, lru_conv_w[0], lru_conv_b[0].reshape(1, -1), w_cat, b_cat, lru_lambda[0], batch, seq, ctx_len)
    lb_all = jnp.cumsum(jax.nn.softmax(hgrn_lb_logits.astype(F32), axis=1), axis=1)
    lb = (lb_all - lb_all[:, :1])[:, 1].reshape(2, 1, HALF)
    o_hg = _hgrn(y, lb, batch, seq, ctx_len)
    y_hg = _hg_out(o_hg, y, hgrn_norm_w[0])
    h, u = mrn([y_lru, y_hg], [(mix_w, 1, 0), (mix_w, 1, 1)], [h], nl, mods, base1 + 2,
               norm_ffn_w[1], base1 + 3, tm_prefs=(512, 256), name="mix_out1")
    out = mrn([ffn_hidden(u, nl, 1)], [(down_w, 1, 0)], [h], nl, mods, base1 + 5,
              final_norm_w, None, tm_prefs=(256,), name="ffn_down1")
    return out.reshape(batch, seq, d)
```

```python
import functools
import math

import numpy as np
import jax
import jax.numpy as jnp
from jax import lax
from jax.experimental import pallas as pl
from jax.experimental.pallas import tpu as pltpu

F32 = jnp.float32
BF16 = jnp.bfloat16

EPS = 1e-6
ROPE_THETA = 10000.0
GRID_W = 64
MOD_CHUNKS = 6
LANES = 128
HEADS = 8
HEAD_DIM = 128
GQA_KV_HEADS = 2
GQA_GROUP = HEADS // GQA_KV_HEADS
MLA_RANK = 512
MLA_ROPE = 64
MLA_QK = HEAD_DIM + MLA_ROPE
HALF = HEADS * HEAD_DIM
LRU_C = 8.0
HG_CHUNK = 64
HG_BLOCK = 256
VMEM_LIMIT = 56 * 1024 * 1024


def _pick(n, prefs):
    for p in prefs:
        if n % p == 0:
            return p
    raise ValueError(f"no tile for {n} in {prefs}")


def _cparams(sem):
    return pltpu.CompilerParams(dimension_semantics=sem, vmem_limit_bytes=VMEM_LIMIT)


def _dot(a, b):
    return jnp.dot(a, b, preferred_element_type=F32)


def _dot_nt(a, b):
    return lax.dot_general(a, b, (((1,), (1,)), ((), ())), preferred_element_type=F32)


def _dot_tn(a, b):
    return lax.dot_general(a, b, (((0,), (0,)), ((), ())), preferred_element_type=F32)


def _sigmoid(x):
    return 1.0 / (1.0 + jnp.exp(-x))


def _mod_kernel(c_ref, w_ref, b_ref, o_ref):
    c = c_ref[...]
    o_ref[0] = _dot(c * _sigmoid(c), w_ref[0]) + b_ref[0]


def _modulation(cc, mod_w, mod_b):
    depth, d, n = mod_w.shape
    tn = _pick(n, (1024, 512, 256, 128))
    return pl.pallas_call(
        _mod_kernel,
        grid=(depth, n // tn),
        in_specs=[pl.BlockSpec((8, d), lambda l, j: (0, 0)),
                  pl.BlockSpec((1, d, tn), lambda l, j: (l, 0, j)),
                  pl.BlockSpec((1, 1, tn), lambda l, j: (l, 0, j))],
        out_specs=pl.BlockSpec((1, 8, tn), lambda l, j: (l, 0, j)),
        out_shape=jax.ShapeDtypeStruct((depth, 8, n), F32),
        compiler_params=_cparams(("parallel", "parallel")),
        name="modulation",
    )(cc, mod_w, mod_b.reshape(depth, 1, n))


def _row_sources(h_parts, tm):
    n1 = h_parts[0].shape[0] // tm
    if len(h_parts) == 1:
        return n1, [lambda i: i]
    return n1, [lambda i: jnp.minimum(i, n1 - 1), lambda i: jnp.maximum(i - n1, 0)]


def _nm_kernel(n_h, n_w, n1, epilogue, tm, rc, *refs):
    h_refs = refs[:n_h]
    nw_ref, sh_ref, sc_ref = refs[n_h:n_h + 3]
    w_refs = refs[n_h + 3:n_h + 3 + n_w]
    o_ref, u_scr = refs[n_h + 3 + n_w:]
    i = pl.program_id(0)

    def prologue(h_ref):
        def chunk(r, carry):
            r0 = pl.multiple_of(r * rc, rc)
            x = h_ref[pl.ds(r0, rc), :]
            ms = jnp.mean(x * x, axis=-1, keepdims=True)
            y = x * lax.rsqrt(ms + EPS) * nw_ref[...]
            u_scr[pl.ds(r0, rc), :] = (y * (1.0 + sc_ref[0]) + sh_ref[0]).astype(BF16)
            return carry
        lax.fori_loop(0, tm // rc, chunk, 0)

    first = pl.program_id(1) == 0
    if n_h == 1:
        pl.when(first)(lambda: prologue(h_refs[0]))
    else:
        pl.when(first & (i < n1))(lambda: prologue(h_refs[0]))
        pl.when(first & (i >= n1))(lambda: prologue(h_refs[1]))

    u = u_scr[...]
    if epilogue == "plain":
        o_ref[...] = _dot(u, w_refs[0][0].astype(BF16)).astype(o_ref.dtype)
    else:
        g = _dot(u, w_refs[0][0].astype(BF16))
        up = _dot(u, w_refs[1][0].astype(BF16))
        o_ref[...] = (g * _sigmoid(g) * up).astype(o_ref.dtype)


def _norm_mod_matmul(h_parts, rows, norm_w, mods, mod_base, ws, layer, epilogue, out_dtype, seq, batch, tn_prefs, name):
    d = h_parts[0].shape[1]
    n = ws[0].shape[2]
    tm = _pick(math.gcd(seq, rows), (1024, 512, 256))
    tn = _pick(n, tn_prefs)
    rc = 32
    n1, rmaps = _row_sources(h_parts, tm)

    def mrow(i, k):
        return mod_base + jnp.minimum(i * tm // seq, batch) * MOD_CHUNKS + k

    in_specs = [pl.BlockSpec((tm, d), lambda i, j, m=m: (m(i), 0), pipeline_mode=(pl.Buffered(1) if k else None))
                for k, m in enumerate(rmaps)]
    in_specs += [pl.BlockSpec((1, d), lambda i, j: (0, 0)),
                 pl.BlockSpec((1, 1, d), lambda i, j: (mrow(i, 0), 0, 0)),
                 pl.BlockSpec((1, 1, d), lambda i, j: (mrow(i, 1), 0, 0))]
    in_specs += [pl.BlockSpec((1, d, tn), lambda i, j: (layer, 0, j)) for _ in ws]
    return pl.pallas_call(
        functools.partial(_nm_kernel, len(h_parts), len(ws), n1, epilogue, tm, rc),
        grid=(rows // tm, n // tn),
        in_specs=in_specs,
        out_specs=pl.BlockSpec((tm, tn), lambda i, j: (i, j)),
        out_shape=jax.ShapeDtypeStruct((rows, n), out_dtype),
        scratch_shapes=[pltpu.VMEM((tm, d), BF16)],
        compiler_params=_cparams(("parallel", "arbitrary")),
        name=name,
    )(*h_parts, norm_w.reshape(1, d), mods, mods, *ws)


def _um_kernel(n_w, epilogue, u_ref, *refs):
    w_refs, o_ref = refs[:n_w], refs[n_w]
    u = u_ref[...]
    if epilogue == "plain":
        o_ref[...] = _dot(u, w_refs[0][0].astype(BF16)).astype(o_ref.dtype)
    else:
        g = _dot(u, w_refs[0][0].astype(BF16))
        up = _dot(u, w_refs[1][0].astype(BF16))
        o_ref[...] = (g * _sigmoid(g) * up).astype(o_ref.dtype)


def _u_matmul(u, rows, ws, layer, epilogue, out_dtype, seq, tn_prefs, name):
    d = u.shape[1]
    n = ws[0].shape[2]
    tm = _pick(math.gcd(seq, rows), (1024, 512, 256))
    tn = _pick(n, tn_prefs)
    return pl.pallas_call(
        functools.partial(_um_kernel, len(ws), epilogue),
        grid=(rows // tm, n // tn),
        in_specs=[pl.BlockSpec((tm, d), lambda i, j: (i, 0))]
        + [pl.BlockSpec((1, d, tn), lambda i, j: (layer, 0, j)) for _ in ws],
        out_specs=pl.BlockSpec((tm, tn), lambda i, j: (i, j)),
        out_shape=jax.ShapeDtypeStruct((rows, n), out_dtype),
        compiler_params=_cparams(("parallel", "arbitrary")),
        name=name,
    )(u, *ws)


def _mrf_kernel(n_a, n_r, n1, mode, *refs):
    a_refs, w_refs = refs[:n_a], refs[n_a:2 * n_a]
    res_refs = refs[2 * n_a:2 * n_a + n_r]
    rest = refs[2 * n_a + n_r:]
    if mode == "next":
        g_ref, nw_ref, sh_ref, sc_ref, h_ref, u_ref = rest
    else:
        g_ref, nw_ref, o_ref = rest
    acc = _dot(a_refs[0][...], w_refs[0][0])
    for a, w in zip(a_refs[1:], w_refs[1:]):
        acc += _dot(a[...], w[0])
    upd = g_ref[0] * acc

    def finish(res_ref):
        h = res_ref[...] + upd
        y = h * lax.rsqrt(jnp.mean(h * h, axis=-1, keepdims=True) + EPS) * nw_ref[...]
        if mode == "next":
            h_ref[...] = h
            u_ref[...] = (y * (1.0 + sc_ref[0]) + sh_ref[0]).astype(BF16)
        else:
            o_ref[...] = y

    if n_r == 1:
        finish(res_refs[0])
    else:
        i = pl.program_id(0)
        pl.when(i < n1)(lambda: finish(res_refs[0]))
        pl.when(i >= n1)(lambda: finish(res_refs[1]))


def _matmul_residual_norm(a_list, w_specs, res_parts, rows, mods, gate_row0, norm_w, next_row0, seq, batch, tm_prefs, name):
    d = res_parts[0].shape[1]
    tm = _pick(math.gcd(seq, rows), tm_prefs)
    n1, rmaps = _row_sources(res_parts, tm)
    mode = "final" if next_row0 is None else "next"

    def mrow(i, base):
        return base + jnp.minimum(i * tm // seq, batch) * MOD_CHUNKS

    row_tile = lambda w: pl.BlockSpec((tm, w), lambda i: (i, 0))
    vec = lambda base: pl.BlockSpec((1, 1, d), lambda i: (mrow(i, base), 0, 0))
    in_specs = [row_tile(a.shape[1]) for a in a_list]
    in_specs += [pl.BlockSpec((1, a.shape[1], d), lambda i, l=l, kb=kb: (l, kb, 0), pipeline_mode=pl.Buffered(1))
                 for a, (_, l, kb) in zip(a_list, w_specs)]
    in_specs += [pl.BlockSpec((tm, d), lambda i, m=m: (m(i), 0)) for m in rmaps]
    in_specs += [vec(gate_row0), pl.BlockSpec((1, d), lambda i: (0, 0))]
    args = [*a_list, *[w for w, _, _ in w_specs], *res_parts, mods, norm_w.reshape(1, d)]
    if mode == "next":
        in_specs += [vec(next_row0), vec(next_row0 + 1)]
        args += [mods, mods]
        out_specs = [row_tile(d), row_tile(d)]
        out_shape = [jax.ShapeDtypeStruct((rows, d), F32), jax.ShapeDtypeStruct((rows, d), BF16)]
    else:
        out_specs = row_tile(d)
        out_shape = jax.ShapeDtypeStruct((rows, d), F32)
    return pl.pallas_call(
        functools.partial(_mrf_kernel, len(a_list), len(res_parts), n1, mode),
        grid=(rows // tm,),
        in_specs=in_specs,
        out_specs=out_specs,
        out_shape=out_shape,
        compiler_params=_cparams(("parallel",)),
        name=name,
    )(*args)


def _prep_kernel(sa, sb, y_ref, qw_ref, kw_ref, mqw_ref, mkw_ref, uq_ref, ukv_ref,
                 ca_ref, sna_ref, cb_ref, snb_ref,
                 qg_ref, kg_ref, vg_ref, qm_ref, km_ref, vm_ref):
    hd = HEAD_DIM
    ca, sna, cb, snb = ca_ref[...], sna_ref[...], cb_ref[...], snb_ref[...]
    ycols = lambda start, width: y_ref[:, start:start + width].astype(F32)

    def head_norm_rope(x, w):
        r = lax.rsqrt(jnp.mean(x * x, axis=-1, keepdims=True) + EPS)
        yh = x * r * w
        return yh * ca + pltpu.roll(yh, hd // 2, axis=1) * sna

    o = 0
    for h in range(HEADS):
        qg_ref[0, h] = (head_norm_rope(ycols(o + h * hd, hd), qw_ref[...]) * sa).astype(BF16)
    o += HALF
    for h in range(GQA_KV_HEADS):
        kg_ref[0, h] = head_norm_rope(ycols(o + h * hd, hd), kw_ref[...]).astype(BF16)
    o += GQA_KV_HEADS * hd
    ones_col = (lax.broadcasted_iota(jnp.int32, (y_ref.shape[0], hd), 1) == 0).astype(BF16)
    for h in range(GQA_KV_HEADS):
        vg_ref[0, h, :, :hd] = y_ref[:, o + h * hd:o + (h + 1) * hd].astype(BF16)
        vg_ref[0, h, :, hd:] = ones_col
    o += GQA_KV_HEADS * hd

    def rms(x, w):
        return (x * lax.rsqrt(jnp.mean(x * x, axis=-1, keepdims=True) + EPS) * w).astype(BF16)

    qb = _dot(rms(ycols(o, MLA_RANK), mqw_ref[...]), uq_ref[...])
    o += MLA_RANK
    kvb = _dot(rms(ycols(o, MLA_RANK), mkw_ref[...]), ukv_ref[...])
    o += MLA_RANK
    kr = (ycols(o, hd) * cb + ycols(o + hd, hd) * snb).astype(BF16)
    for h in range(HEADS):
        lo, hi = h * hd, (h + 1) * hd
        qm_ref[0, h, :, :hd] = (qb[:, lo:hi] * sb).astype(BF16)
        qm_ref[0, h, :, hd:] = ((qb[:, HALF + lo:HALF + hi] * cb + qb[:, 2 * HALF + lo:2 * HALF + hi] * snb) * sb).astype(BF16)
        km_ref[0, h, :, :hd] = kvb[:, lo:hi].astype(BF16)
        km_ref[0, h, :, hd:] = kr
        vm_ref[0, h, :, :hd] = kvb[:, HALF + lo:HALF + hi].astype(BF16)
        vm_ref[0, h, :, hd:] = ones_col


def _attn_prep(y, qw, kw, mqw, mkw, uq, ukv, tabs, batch, seq, ctx_len):
    rows, n = y.shape
    ts = ctx_len
    nl = seq // ts
    n_lat = batch * nl
    t_all = seq + ctx_len
    hd = HEAD_DIM

    def bidx(t):
        return jnp.where(t < n_lat, t // nl, t - n_lat)

    def sidx(t):
        return jnp.where(t < n_lat, t % nl + 1, 0)

    def ridx(t):
        return jnp.where(t < n_lat, t % nl, nl)

    def hm(width, heads):
        return pl.BlockSpec((1, heads, ts, width), lambda t: (bidx(t), 0, sidx(t), 0))

    full = lambda a: pl.BlockSpec(a.shape, lambda t: (0,) * a.ndim)
    tab = pl.BlockSpec((ts, hd), lambda t: (ridx(t), 0))
    outs = [((batch, HEADS, t_all, hd), hm(hd, HEADS)),
            ((batch, GQA_KV_HEADS, t_all, hd), hm(hd, GQA_KV_HEADS)),
            ((batch, GQA_KV_HEADS, t_all, 2 * hd), hm(2 * hd, GQA_KV_HEADS)),
            ((batch, HEADS, t_all, 2 * hd), hm(2 * hd, HEADS)),
            ((batch, HEADS, t_all, 2 * hd), hm(2 * hd, HEADS)),
            ((batch, HEADS, t_all, 2 * hd), hm(2 * hd, HEADS))]
    return pl.pallas_call(
        functools.partial(_prep_kernel, hd ** -0.5, MLA_QK ** -0.5),
        grid=(rows // ts,),
        in_specs=[pl.BlockSpec((ts, n), lambda t: (t, 0)), full(qw), full(kw), full(mqw), full(mkw),
                  full(uq), full(ukv), tab, tab, tab, tab],
        out_specs=[s for _, s in outs],
        out_shape=[jax.ShapeDtypeStruct(sh, BF16) for sh, _ in outs],
        compiler_params=_cparams(("parallel",)),
        name="attn_prep",
    )(y, qw, kw, mqw, mkw, uq, ukv, *tabs)


def _attn_kernel(group, kv_group, ctx_len, q_ref, k_ref, v_ref, o_ref):
    dv = HEAD_DIM

    def run(t_k):
        kv = lambda g: g if kv_group > 1 else 0
        ss = [_dot_nt(q_ref[0, g], k_ref[0, kv(g), :t_k, :]) for g in range(group)]
        for g, s in enumerate(ss):
            p = jnp.exp(s - jnp.max(s, axis=-1, keepdims=True))
            ov = _dot(p.astype(BF16), v_ref[0, kv(g), :t_k, :])
            o_ref[:, g * dv:(g + 1) * dv] = (ov[:, :dv] / ov[:, dv:dv + 1]).astype(o_ref.dtype)

    is_ctx = pl.program_id(2) == 0

    @pl.when(is_ctx)
    def _():
        run(ctx_len)

    @pl.when(jnp.logical_not(is_ctx))
    def _():
        run(k_ref.shape[2])


def _attention(q, k, v, group, kv_group, batch, seq, ctx_len, name):
    _, hq, t_all, dq = q.shape
    dvp = v.shape[-1]
    tq = ctx_len
    nq = seq // tq
    rows = batch * t_all

    def orow(b, qi):
        return jnp.where(qi == 0, batch * nq + b, b * nq + qi - 1)

    return pl.pallas_call(
        functools.partial(_attn_kernel, group, kv_group, ctx_len),
        grid=(batch, hq // group, nq + 1),
        in_specs=[pl.BlockSpec((1, group, tq, dq), lambda b, h, qi: (b, h, qi, 0)),
                  pl.BlockSpec((1, kv_group, t_all, dq), lambda b, h, qi: (b, h, 0, 0)),
                  pl.BlockSpec((1, kv_group, t_all, dvp), lambda b, h, qi: (b, h, 0, 0))],
        out_specs=pl.BlockSpec((tq, group * HEAD_DIM), lambda b, h, qi: (orow(b, qi), h)),
        out_shape=jax.ShapeDtypeStruct((rows, hq * HEAD_DIM), BF16),
        compiler_params=_cparams(("parallel", "parallel", "arbitrary")),
        name=name,
    )(q, k, v)


def _gelu(x):
    return 0.5 * x * (1.0 + jnp.tanh(0.7978845608028654 * (x + 0.044715 * x * x * x)))


def _lru_kernel(seq, ctx_len, rc, xl_ref, xc_ref, gate_ref, cw_ref, cb_ref, w_ref, b_ref, lam_ref,
                o_ref, xp_scr, af_scr, bf_scr, ab_scr, bb_scr, hs_scr):
    cb_w = xl_ref.shape[1]
    nblk = cb_w // LANES
    lam = lam_ref[...]
    log_a_unit = -LRU_C * (jnp.maximum(-lam, 0.0) + jnp.log(1.0 + jnp.exp(-jnp.abs(lam))))
    cw = cw_ref[...]
    cbias = cb_ref[...]
    zeros8 = jnp.zeros((8, cb_w), F32)

    def coeffs(src_ref, n_rows, dst0):
        xp_scr[pl.ds(0, 8), :] = zeros8
        xp_scr[pl.ds(8 + n_rows, 8), :] = zeros8

        def cp(r, carry):
            r0 = pl.multiple_of(r * rc, rc)
            xp_scr[pl.ds(8 + r0, rc), :] = src_ref[pl.ds(r0, rc), :]
            return carry
        lax.fori_loop(0, n_rows // rc, cp, 0)

        def chunk(r, carry):
            r0 = pl.multiple_of(r * rc, rc)
            win = xp_scr[pl.ds(r0, rc + 16), :]
            xc = cbias + cw[2:3] * win[8:8 + rc]
            xc += cw[0:1] * pltpu.roll(win, 2, axis=0)[8:8 + rc]
            xc += cw[1:2] * pltpu.roll(win, 1, axis=0)[8:8 + rc]
            xc += cw[3:4] * pltpu.roll(win, rc + 15, axis=0)[8:8 + rc]
            for n in range(nblk):
                lo, hi = n * LANES, (n + 1) * LANES
                xcn = xc[:, lo:hi]
                z = _dot(xcn.astype(BF16), w_ref[n]) + b_ref[n]
                for d, (a_scr, b_scr) in enumerate(((af_scr, bf_scr), (ab_scr, bb_scr))):
                    r_g = _sigmoid(z[:, (2 * d) * LANES:(2 * d + 1) * LANES])
                    i_g = _sigmoid(z[:, (2 * d + 1) * LANES:(2 * d + 2) * LANES])
                    a = jnp.exp(r_g * log_a_unit[d:d + 1, lo:hi])
                    a_scr[pl.ds(dst0 + r0, rc), lo:hi] = a
                    b_scr[pl.ds(dst0 + r0, rc), lo:hi] = jnp.sqrt(1.0 - a * a) * (i_g * xcn)
            return carry
        lax.fori_loop(0, n_rows // rc, chunk, 0)

    coeffs(xc_ref, ctx_len, 0)
    coeffs(xl_ref, seq, ctx_len)
    t_all = seq + ctx_len

    row = lax.broadcasted_iota(jnp.int32, (8, cb_w), 0)

    def group_scan(a, b, h_in, reverse):
        for sh in (1, 2, 4):
            keep = (row < 8 - sh) if reverse else (row >= sh)
            rot = (8 - sh) if reverse else sh
            a_s = jnp.where(keep, pltpu.roll(a, rot, axis=0), 1.0)
            b_s = jnp.where(keep, pltpu.roll(b, rot, axis=0), 0.0)
            b = a * b_s + b
            a = a * a_s
        h = a * h_in + b
        edge = 0 if reverse else 7
        return h, jnp.broadcast_to(h[edge:edge + 1], (8, cb_w))

    n_groups = t_all // 8
    n_ctx_groups = ctx_len // 8

    def step(g, carry):
        hf_in, hb_in = carry
        r0 = pl.multiple_of(g * 8, 8)
        hf, hf_in = group_scan(af_scr[pl.ds(r0, 8), :], bf_scr[pl.ds(r0, 8), :], hf_in, False)
        hs_scr[pl.ds(r0, 8), :] = hf
        gb = jnp.where(g < n_ctx_groups, n_ctx_groups - 1 - g, n_groups + n_ctx_groups - 1 - g)
        rb0 = pl.multiple_of(gb * 8, 8)
        hb, hb_in = group_scan(ab_scr[pl.ds(rb0, 8), :], bb_scr[pl.ds(rb0, 8), :], hb_in, True)
        bb_scr[pl.ds(rb0, 8), :] = hb
        return hf_in, hb_in
    h0 = jnp.zeros((8, cb_w), F32)
    lax.fori_loop(0, n_groups, step, (h0, h0), unroll=2)

    def fin(r, carry):
        r0 = pl.multiple_of(r * rc, rc)
        hsum = hs_scr[pl.ds(ctx_len + r0, rc), :] + bb_scr[pl.ds(ctx_len + r0, rc), :]
        o_ref[pl.ds(r0, rc), :] = (_gelu(gate_ref[pl.ds(r0, rc), :]) * hsum).astype(o_ref.dtype)
        return carry
    lax.fori_loop(0, seq // rc, fin, 0)


def _lru(y, conv_w, conv_b, w_cat, b_cat, lam, batch, seq, ctx_len):
    cb_w = 2 * LANES
    ncb = HALF // cb_w
    nl = batch * seq
    rc = 256
    t_all = seq + ctx_len
    scr = lambda r: pltpu.VMEM((r, cb_w), F32)
    return pl.pallas_call(
        functools.partial(_lru_kernel, seq, ctx_len, rc),
        grid=(batch, ncb),
        in_specs=[pl.BlockSpec((seq, cb_w), lambda b, c: (b, c)),
                  pl.BlockSpec((ctx_len, cb_w), lambda b, c: (nl // ctx_len + b, c)),
                  pl.BlockSpec((seq, cb_w), lambda b, c: (b, ncb + c)),
                  pl.BlockSpec((4, cb_w), lambda b, c: (0, c)),
                  pl.BlockSpec((1, cb_w), lambda b, c: (0, c)),
                  pl.BlockSpec((cb_w // LANES, LANES, 4 * LANES), lambda b, c: (c, 0, 0)),
                  pl.BlockSpec((cb_w // LANES, 1, 4 * LANES), lambda b, c: (c, 0, 0)),
                  pl.BlockSpec((2, cb_w), lambda b, c: (0, c))],
        out_specs=pl.BlockSpec((seq, cb_w), lambda b, c: (b, c)),
        out_shape=jax.ShapeDtypeStruct((nl, HALF), BF16),
        scratch_shapes=[scr(seq + 16), scr(t_all), scr(t_all), scr(t_all), scr(t_all), scr(t_all)],
        compiler_params=_cparams(("parallel", "parallel")),
        name="rglru",
    )(y, y, y, conv_w, conv_b, w_cat, b_cat, lam)


def _hg_tables():
    c = HG_CHUNK
    masks, pairs = [], []
    for d in range(2):
        r = np.arange(c) if d == 0 else c - 1 - np.arange(c)
        rt, rs = r[:, None], r[None, :]
        m = [rs <= rt]
        pm = []
        w = c // 2
        while w >= 1:
            bnd = w * (2 * (rt // (2 * w)) + 1)
            odd = (rt // w) % 2 == 1
            m.append(np.where(odd, (bnd <= rs) & (rs <= rt), (rt < rs) & (rs <= bnd - 1)))
            pm.append(odd & ((rs // w) % 2 == 0) & (rt // (2 * w) == rs // (2 * w)))
            w //= 2
        pm.append(rt == rs)
        m.append(np.ones((8, c), bool))
        masks.append(np.concatenate(m, axis=0))
        pairs.append(np.stack(pm))
    return np.stack(masks).astype(np.float32), np.stack(pairs).astype(np.float32)


def _hg_kernel(n_ctx_blk, n_levels, q_ref, f_ref, v_ref, lb_ref, mall_ref, pm_ref, o_ref, s_scr):
    c, hd = HG_CHUNK, HEAD_DIM
    d = pl.program_id(0)
    st = pl.program_id(2)
    n_chunks = HG_BLOCK // c

    @pl.when(st == 0)
    def _():
        s_scr[...] = jnp.zeros_like(s_scr)

    def make_chunk(with_out):
        def chunk(i, carry):
            ci = jnp.where(d == 0, i, n_chunks - 1 - i)
            r0 = pl.multiple_of(ci * c, c)
            mall = mall_ref[0]
            heads = range(HEADS)
            sl = [slice(h * hd, (h + 1) * hd) for h in heads]
            ks, cs2s = [], []
            for h in heads:
                lb = lb_ref[0, :, sl[h]]
                f = lb + (1.0 - lb) * _sigmoid(f_ref[pl.ds(r0, c), sl[h]])
                logf = jnp.log(f)
                lf_hi = logf.astype(BF16)
                lf_lo = (logf - lf_hi.astype(F32)).astype(BF16)
                ks.append(1.0 - f)
                cs2s.append(_dot(mall, jnp.concatenate([lf_hi, lf_lo], axis=1)))
            css = [cs2[:, :hd] + cs2[:, hd:] for cs2 in cs2s]
            es = [jnp.exp(cs) for cs in css]
            rows = lambda h, n: es[h][n * c:(n + 1) * c]
            v16s = [v_ref[pl.ds(r0, c), sl[h]].astype(BF16) for h in heads]
            s_ts = [s_scr[h] for h in heads]
            if with_out:
                qs = []
                for h in heads:
                    qx = q_ref[pl.ds(r0, c), sl[h]]
                    qs.append(qx * _sigmoid(qx))
                a_s = []
                for h in heads:
                    a = pm_ref[0, n_levels] * _dot_nt(qs[h].astype(BF16), ks[h].astype(BF16))
                    for lv in range(n_levels):
                        el = rows(h, 1 + lv)
                        a += pm_ref[0, lv] * _dot_nt((qs[h] * el).astype(BF16), (ks[h] * el).astype(BF16))
                    a_s.append(a.astype(BF16))
                for h in heads:
                    o_ref[0, pl.ds(r0, c), sl[h]] = (
                        _dot(a_s[h], v16s[h]) + _dot_nt((qs[h] * rows(h, 0)).astype(BF16), s_ts[h].astype(BF16)))
            base = (1 + n_levels) * c
            for h in heads:
                after = jnp.exp(css[h][base:base + 1] - css[h][:c])
                s_scr[h] = es[h][base:base + 1] * s_ts[h] + _dot_tn(v16s[h], (ks[h] * after).astype(BF16))
            return carry
        return chunk

    @pl.when(st >= n_ctx_blk)
    def _():
        lax.fori_loop(0, n_chunks, make_chunk(True), 0, unroll=True)

    @pl.when(st < n_ctx_blk)
    def _():
        lax.fori_loop(0, n_chunks, make_chunk(False), 0)


def _hgrn(y, lb, batch, seq, ctx_len):
    rb = HG_BLOCK
    n_ctx_blk = ctx_len // rb
    n_lat_blk = seq // rb
    nl = batch * seq
    mall_np, pm_np = _hg_tables()
    n_levels = pm_np.shape[1] - 1
    mall = jnp.asarray(mall_np, BF16)
    pm = jnp.asarray(pm_np, F32)
    cols = 1

    def lat_blk(d, st):
        lc = jnp.maximum(st - n_ctx_blk, 0)
        return jnp.where(d == 0, lc, n_lat_blk - 1 - lc)

    def row_blk(d, b, st):
        cc = jnp.where(d == 0, st, n_ctx_blk - 1 - st)
        return jnp.where(st < n_ctx_blk, nl // rb + b * n_ctx_blk + cc, b * n_lat_blk + lat_blk(d, st))

    return pl.pallas_call(
        functools.partial(_hg_kernel, n_ctx_blk, n_levels),
        grid=(2, batch, n_ctx_blk + n_lat_blk),
        in_specs=[pl.BlockSpec((rb, HALF), lambda d, b, st: (row_blk(d, b, st), 2 * cols)),
                  pl.BlockSpec((rb, HALF), lambda d, b, st: (row_blk(d, b, st), 3 * cols + d)),
                  pl.BlockSpec((rb, HALF), lambda d, b, st: (row_blk(d, b, st), 5 * cols)),
                  pl.BlockSpec((1, 1, HALF), lambda d, b, st: (d, 0, 0)),
                  pl.BlockSpec((1,) + mall_np.shape[1:], lambda d, b, st: (d, 0, 0)),
                  pl.BlockSpec((1,) + pm_np.shape[1:], lambda d, b, st: (d, 0, 0, 0))],
        out_specs=pl.BlockSpec((1, rb, HALF), lambda d, b, st: (d, b * n_lat_blk + lat_blk(d, st), 0)),
        out_shape=jax.ShapeDtypeStruct((2, nl, HALF), F32),
        scratch_shapes=[pltpu.VMEM((HEADS, HEAD_DIM, HEAD_DIM), F32)],
        compiler_params=_cparams(("parallel", "parallel", "arbitrary")),
        name="hgrn2",
    )(y, y, y, lb, mall, pm)


def _hgo_kernel(o_ref, g_ref, w_ref, y_ref):
    hd = HEAD_DIM
    for h in range(HEADS):
        lo, hi = h * hd, (h + 1) * hd
        o = o_ref[0, :, lo:hi] + o_ref[1, :, lo:hi]
        n = o * lax.rsqrt(jnp.mean(o * o, axis=-1, keepdims=True) + EPS) * w_ref[:, lo:hi]
        g = g_ref[:, lo:hi]
        y_ref[:, lo:hi] = (n * (g * _sigmoid(g))).astype(y_ref.dtype)


def _hg_out(o, y, norm_w):
    _, nl, w = o.shape
    tm = _pick(nl, (512, 256))
    return pl.pallas_call(
        _hgo_kernel,
        grid=(nl // tm,),
        in_specs=[pl.BlockSpec((2, tm, w), lambda i: (0, i, 0)),
                  pl.BlockSpec((tm, w), lambda i: (i, 6)),
                  pl.BlockSpec((1, w), lambda i: (0, 0))],
        out_specs=pl.BlockSpec((tm, w), lambda i: (i, 0)),
        out_shape=jax.ShapeDtypeStruct((nl, w), BF16),
        compiler_params=_cparams(("parallel",)),
        name="hgrn2_out",
    )(o, y, norm_w.reshape(1, w))


def _rot_cols(w):
    x1, x2 = jnp.split(w, 2, axis=-1)
    return jnp.concatenate([-x2, x1], axis=-1)


def _pad_lanes(w):
    return jnp.pad(w, [(0, 0)] * (w.ndim - 1) + [(0, LANES - w.shape[-1])])


def _attn_weights(w_in, uq, ukv):
    d = w_in.shape[0]
    kr_w = w_in[:, -MLA_ROPE:]
    w_in_p = jnp.concatenate([w_in[:, :-MLA_ROPE], _pad_lanes(kr_w), _pad_lanes(_rot_cols(kr_w))], axis=1)
    uq3 = uq.reshape(MLA_RANK, HEADS, MLA_QK)
    nope, rope = uq3[..., :HEAD_DIM], uq3[..., HEAD_DIM:]
    uq_p = jnp.concatenate([nope.reshape(MLA_RANK, HALF), _pad_lanes(rope).reshape(MLA_RANK, HALF),
                            _pad_lanes(_rot_cols(rope)).reshape(MLA_RANK, HALF)], axis=1)
    ukv3 = ukv.reshape(MLA_RANK, HEADS, 2 * HEAD_DIM)
    ukv_p = jnp.concatenate([ukv3[..., :HEAD_DIM].reshape(MLA_RANK, HALF),
                             ukv3[..., HEAD_DIM:].reshape(MLA_RANK, HALF)], axis=1)
    del d
    return w_in_p.astype(BF16), uq_p.astype(BF16), ukv_p.astype(BF16)


def _rope_tables(seq, ctx_len):
    rows = seq // GRID_W
    row_id = jnp.repeat(jnp.arange(rows), GRID_W).astype(F32)
    col_id = (jnp.arange(seq) % GRID_W).astype(F32)

    def table(dim):
        quarter = dim // 4
        inv_freq = ROPE_THETA ** (-jnp.arange(quarter, dtype=F32) / quarter)
        ang = jnp.concatenate([row_id[:, None] * inv_freq, col_id[:, None] * inv_freq], axis=-1)
        cos, sin = jnp.cos(ang), jnp.sin(ang)
        cos2 = jnp.concatenate([cos, cos], axis=-1)
        cos2 = jnp.concatenate([cos2, jnp.ones((ctx_len, dim), F32)], axis=0)
        return cos2, cos, sin

    ca, _, sin_a = table(HEAD_DIM)
    sna = jnp.concatenate([jnp.concatenate([-sin_a, sin_a], axis=-1), jnp.zeros((ctx_len, HEAD_DIM), F32)], axis=0)
    cb, _, sin_b = table(MLA_ROPE)
    snb = jnp.concatenate([jnp.concatenate([sin_b, sin_b], axis=-1), jnp.zeros((ctx_len, MLA_ROPE), F32)], axis=0)
    return ca, sna, _pad_lanes(cb), _pad_lanes(snb)


def kernel(x, c, ctx, c_ctx, mod_w, mod_b, norm_mix_w, norm_ffn_w, mix_out_w, ffn_gate_w, ffn_up_w, ffn_down_w, attn_in_w, gqa_q_norm_w, gqa_k_norm_w, mla_q_norm_w, mla_uq_w, mla_kv_norm_w, mla_ukv_w, rec_in_w, lru_conv_w, lru_conv_b, lru_ra_w, lru_ra_b, lru_ix_w, lru_ix_b, lru_lambda, hgrn_lb_logits, hgrn_norm_w, final_norm_w):
    batch, seq, d = x.shape
    ctx_len = ctx.shape[1]
    depth = mod_w.shape[0]
    assert depth == 2 and batch < 8 and seq % ctx_len == 0 and ctx_len == HG_BLOCK
    nl = batch * seq
    rows_all = nl + batch * ctx_len

    cc = jnp.zeros((8, d), F32).at[:batch].set(c).at[batch].set(c_ctx)
    mods = _modulation(cc, mod_w, mod_b).reshape(depth * 8 * MOD_CHUNKS, 1, d)
    x_rows, ctx_rows = x.reshape(nl, d), ctx.reshape(batch * ctx_len, d)
    nm = functools.partial(_norm_mod_matmul, seq=seq, batch=batch)
    um = functools.partial(_u_matmul, seq=seq)
    mrn = functools.partial(_matmul_residual_norm, seq=seq, batch=batch)
    mix_w, down_w = mix_out_w.astype(BF16), ffn_down_w.astype(BF16)
    base1 = 8 * MOD_CHUNKS

    def ffn_hidden(u, rows, l):
        return um(u, rows, [ffn_gate_w, ffn_up_w], l, "swiglu", BF16, tn_prefs=(512, 256, 128), name=f"ffn_up{l}")

    w_in_p, uq_p, ukv_p = _attn_weights(attn_in_w[0], mla_uq_w[0], mla_ukv_w[0])
    y = nm([x_rows, ctx_rows], rows_all, norm_mix_w[0], mods, 0, [w_in_p[None]], 0, "plain", BF16,
           tn_prefs=(1408, 704, 256, 128), name="attn_in")
    qg, kg, vg, qm, km, vm = _attn_prep(
        y, gqa_q_norm_w[0].reshape(1, -1), gqa_k_norm_w[0].reshape(1, -1), mla_q_norm_w[0].reshape(1, -1),
        mla_kv_norm_w[0].reshape(1, -1), uq_p, ukv_p, _rope_tables(seq, ctx_len), batch, seq, ctx_len)
    og = _attention(qg, kg, vg, GQA_GROUP, 1, batch, seq, ctx_len, "gqa")
    om = _attention(qm, km, vm, 4, 4, batch, seq, ctx_len, "mla")
    h, u = mrn([og, om], [(mix_w, 0, 0), (mix_w, 0, 1)], [x_rows, ctx_rows], rows_all, mods, 2,
               norm_ffn_w[0], 3, tm_prefs=(512, 256), name="mix_out0")
    h, u = mrn([ffn_hidden(u, rows_all, 0)], [(down_w, 0, 0)], [h], rows_all, mods, 5,
               norm_mix_w[1], base1, tm_prefs=(256,), name="ffn_down0")

    y = um(u, rows_all, [rec_in_w], 0, "plain", F32, tn_prefs=(1024, 512, 256, 128), name="rec_in")
    w_cat = jnp.concatenate([lru_ra_w[0, 0], lru_ix_w[0, 0], lru_ra_w[0, 1], lru_ix_w[0, 1]], axis=-1).astype(BF16)
    b_cat = jnp.concatenate([lru_ra_b[0, 0], lru_ix_b[0, 0], lru_ra_b[0, 1], lru_ix_b[0, 1]], axis=-1)[:, None, :]
    y_lru = _lru(y, lru_conv_w[0], lru_conv_b[0].reshape(1, -1), w_cat, b_cat, lru_lambda[0], batch, seq, ctx_len)
    lb_all = jnp.cumsum(jax.nn.softmax(hgrn_lb_logits.astype(F32), axis=1), axis=1)
    lb = (lb_all - lb_all[:, :1])[:, 1].reshape(2, 1, HALF)
    o_hg = _hgrn(y, lb, batch, seq, ctx_len)
    y_hg = _hg_out(o_hg, y, hgrn_norm_w[0])
    h, u = mrn([y_lru, y_hg], [(mix_w, 1, 0), (mix_w, 1, 1)], [h], nl, mods, base1 + 2,
               norm_ffn_w[1], base1 + 3, tm_prefs=(512, 256), name="mix_out1")
    out = mrn([ffn_hidden(u, nl, 1)], [(down_w, 1, 0)], [h], nl, mods, base1 + 5,
              final_norm_w, None, tm_prefs=(256,), name="ffn_down1")
    return out.reshape(batch, seq, d)
```

```python
import functools
import math

import numpy as np
import jax
import jax.numpy as jnp
from jax import lax
from jax.experimental import pallas as pl
from jax.experimental.pallas import tpu as pltpu

F32 = jnp.float32
BF16 = jnp.bfloat16

EPS = 1e-6
ROPE_THETA = 10000.0
GRID_W = 64
MOD_CHUNKS = 6
LANES = 128
HEADS = 8
HEAD_DIM = 128
GQA_KV_HEADS = 2
GQA_GROUP = HEADS // GQA_KV_HEADS
MLA_RANK = 512
MLA_ROPE = 64
MLA_QK = HEAD_DIM + MLA_ROPE
HALF = HEADS * HEAD_DIM
LRU_C = 8.0
HG_CHUNK = 64
HG_BLOCK = 256
VMEM_LIMIT = 56 * 1024 * 1024


def _pick(n, prefs):
    for p in prefs:
        if n % p == 0:
            return p
    raise ValueError(f"no tile for {n} in {prefs}")


def _cparams(sem):
    return pltpu.CompilerParams(dimension_semantics=sem, vmem_limit_bytes=VMEM_LIMIT)


def _dot(a, b):
    return jnp.dot(a, b, preferred_element_type=F32)


def _dot_nt(a, b):
    return lax.dot_general(a, b, (((1,), (1,)), ((), ())), preferred_element_type=F32)


def _dot_tn(a, b):
    return lax.dot_general(a, b, (((0,), (0,)), ((), ())), preferred_element_type=F32)


def _sigmoid(x):
    return 1.0 / (1.0 + jnp.exp(-x))


def _mod_kernel(c_ref, w_ref, b_ref, o_ref):
    c = c_ref[...]
    o_ref[0] = _dot(c * _sigmoid(c), w_ref[0]) + b_ref[0]


def _modulation(cc, mod_w, mod_b):
    depth, d, n = mod_w.shape
    tn = _pick(n, (1024, 512, 256, 128))
    return pl.pallas_call(
        _mod_kernel,
        grid=(depth, n // tn),
        in_specs=[pl.BlockSpec((8, d), lambda l, j: (0, 0)),
                  pl.BlockSpec((1, d, tn), lambda l, j: (l, 0, j)),
                  pl.BlockSpec((1, 1, tn), lambda l, j: (l, 0, j))],
        out_specs=pl.BlockSpec((1, 8, tn), lambda l, j: (l, 0, j)),
        out_shape=jax.ShapeDtypeStruct((depth, 8, n), F32),
        compiler_params=_cparams(("parallel", "parallel")),
        name="modulation",
    )(cc, mod_w, mod_b.reshape(depth, 1, n))


def _row_sources(h_parts, tm):
    n1 = h_parts[0].shape[0] // tm
    if len(h_parts) == 1:
        return n1, [lambda i: i]
    return n1, [lambda i: jnp.minimum(i, n1 - 1), lambda i: jnp.maximum(i - n1, 0)]


def _nm_kernel(n_h, n_w, n1, epilogue, tm, rc, *refs):
    h_refs = refs[:n_h]
    nw_ref, sh_ref, sc_ref = refs[n_h:n_h + 3]
    w_refs = refs[n_h + 3:n_h + 3 + n_w]
    o_ref, u_scr = refs[n_h + 3 + n_w:]
    i = pl.program_id(0)

    def prologue(h_ref):
        def chunk(r, carry):
            r0 = pl.multiple_of(r * rc, rc)
            x = h_ref[pl.ds(r0, rc), :]
            ms = jnp.mean(x * x, axis=-1, keepdims=True)
            y = x * lax.rsqrt(ms + EPS) * nw_ref[...]
            u_scr[pl.ds(r0, rc), :] = (y * (1.0 + sc_ref[0]) + sh_ref[0]).astype(BF16)
            return carry
        lax.fori_loop(0, tm // rc, chunk, 0, unroll=8)

    first = pl.program_id(1) == 0
    if n_h == 1:
        pl.when(first)(lambda: prologue(h_refs[0]))
    else:
        pl.when(first & (i < n1))(lambda: prologue(h_refs[0]))
        pl.when(first & (i >= n1))(lambda: prologue(h_refs[1]))

    u = u_scr[...]
    if epilogue == "plain":
        o_ref[...] = _dot(u, w_refs[0][0].astype(BF16)).astype(o_ref.dtype)
    else:
        g = _dot(u, w_refs[0][0].astype(BF16))
        up = _dot(u, w_refs[1][0].astype(BF16))
        o_ref[...] = (g * _sigmoid(g) * up).astype(o_ref.dtype)


def _norm_mod_matmul(h_parts, rows, norm_w, mods, mod_base, ws, layer, epilogue, out_dtype, seq, batch, tn_prefs, name):
    d = h_parts[0].shape[1]
    n = ws[0].shape[2]
    tm = _pick(math.gcd(seq, rows), (1024, 512, 256))
    tn = _pick(n, tn_prefs)
    rc = 32
    n1, rmaps = _row_sources(h_parts, tm)

    def mrow(i, k):
        return mod_base + jnp.minimum(i * tm // seq, batch) * MOD_CHUNKS + k

    in_specs = [pl.BlockSpec((tm, d), lambda i, j, m=m: (m(i), 0), pipeline_mode=(pl.Buffered(1) if k else None))
                for k, m in enumerate(rmaps)]
    in_specs += [pl.BlockSpec((1, d), lambda i, j: (0, 0)),
                 pl.BlockSpec((1, 1, d), lambda i, j: (mrow(i, 0), 0, 0)),
                 pl.BlockSpec((1, 1, d), lambda i, j: (mrow(i, 1), 0, 0))]
    in_specs += [pl.BlockSpec((1, d, tn), lambda i, j: (layer, 0, j)) for _ in ws]
    return pl.pallas_call(
        functools.partial(_nm_kernel, len(h_parts), len(ws), n1, epilogue, tm, rc),
        grid=(rows // tm, n // tn),
        in_specs=in_specs,
        out_specs=pl.BlockSpec((tm, tn), lambda i, j: (i, j)),
        out_shape=jax.ShapeDtypeStruct((rows, n), out_dtype),
        scratch_shapes=[pltpu.VMEM((tm, d), BF16)],
        compiler_params=_cparams(("parallel", "arbitrary")),
        name=name,
    )(*h_parts, norm_w.reshape(1, d), mods, mods, *ws)


def _um_kernel(n_w, epilogue, u_ref, *refs):
    w_refs, o_ref = refs[:n_w], refs[n_w]
    u = u_ref[...]
    if epilogue == "plain":
        o_ref[...] = _dot(u, w_refs[0][0].astype(BF16)).astype(o_ref.dtype)
    else:
        g = _dot(u, w_refs[0][0].astype(BF16))
        up = _dot(u, w_refs[1][0].astype(BF16))
        o_ref[...] = (g * _sigmoid(g) * up).astype(o_ref.dtype)


def _u_matmul(u, rows, ws, layer, epilogue, out_dtype, seq, tn_prefs, name):
    d = u.shape[1]
    n = ws[0].shape[2]
    tm = _pick(math.gcd(seq, rows), (1024, 512, 256))
    tn = _pick(n, tn_prefs)
    return pl.pallas_call(
        functools.partial(_um_kernel, len(ws), epilogue),
        grid=(rows // tm, n // tn),
        in_specs=[pl.BlockSpec((tm, d), lambda i, j: (i, 0))]
        + [pl.BlockSpec((1, d, tn), lambda i, j: (layer, 0, j)) for _ in ws],
        out_specs=pl.BlockSpec((tm, tn), lambda i, j: (i, j)),
        out_shape=jax.ShapeDtypeStruct((rows, n), out_dtype),
        compiler_params=_cparams(("parallel", "arbitrary")),
        name=name,
    )(u, *ws)


def _mrf_kernel(n_a, n_r, n1, mode, *refs):
    a_refs, w_refs = refs[:n_a], refs[n_a:2 * n_a]
    res_refs = refs[2 * n_a:2 * n_a + n_r]
    rest = refs[2 * n_a + n_r:]
    if mode == "next":
        g_ref, nw_ref, sh_ref, sc_ref, h_ref, u_ref = rest
    else:
        g_ref, nw_ref, o_ref = rest
    acc = _dot(a_refs[0][...], w_refs[0][0])
    for a, w in zip(a_refs[1:], w_refs[1:]):
        acc += _dot(a[...], w[0])
    upd = g_ref[0] * acc

    def finish(res_ref):
        h = res_ref[...] + upd
        y = h * lax.rsqrt(jnp.mean(h * h, axis=-1, keepdims=True) + EPS) * nw_ref[...]
        if mode == "next":
            h_ref[...] = h
            u_ref[...] = (y * (1.0 + sc_ref[0]) + sh_ref[0]).astype(BF16)
        else:
            o_ref[...] = y

    if n_r == 1:
        finish(res_refs[0])
    else:
        i = pl.program_id(0)
        pl.when(i < n1)(lambda: finish(res_refs[0]))
        pl.when(i >= n1)(lambda: finish(res_refs[1]))


def _matmul_residual_norm(a_list, w_specs, res_parts, rows, mods, gate_row0, norm_w, next_row0, seq, batch, tm_prefs, name):
    d = res_parts[0].shape[1]
    tm = _pick(math.gcd(seq, rows), tm_prefs)
    n1, rmaps = _row_sources(res_parts, tm)
    mode = "final" if next_row0 is None else "next"

    def mrow(i, base):
        return base + jnp.minimum(i * tm // seq, batch) * MOD_CHUNKS

    row_tile = lambda w: pl.BlockSpec((tm, w), lambda i: (i, 0))
    vec = lambda base: pl.BlockSpec((1, 1, d), lambda i: (mrow(i, base), 0, 0))
    in_specs = [row_tile(a.shape[1]) for a in a_list]
    in_specs += [pl.BlockSpec((1, a.shape[1], d), lambda i, l=l, kb=kb: (l, kb, 0), pipeline_mode=pl.Buffered(1))
                 for a, (_, l, kb) in zip(a_list, w_specs)]
    in_specs += [pl.BlockSpec((tm, d), lambda i, m=m: (m(i), 0)) for m in rmaps]
    in_specs += [vec(gate_row0), pl.BlockSpec((1, d), lambda i: (0, 0))]
    args = [*a_list, *[w for w, _, _ in w_specs], *res_parts, mods, norm_w.reshape(1, d)]
    if mode == "next":
        in_specs += [vec(next_row0), vec(next_row0 + 1)]
        args += [mods, mods]
        out_specs = [row_tile(d), row_tile(d)]
        out_shape = [jax.ShapeDtypeStruct((rows, d), F32), jax.ShapeDtypeStruct((rows, d), BF16)]
    else:
        out_specs = row_tile(d)
        out_shape = jax.ShapeDtypeStruct((rows, d), F32)
    return pl.pallas_call(
        functools.partial(_mrf_kernel, len(a_list), len(res_parts), n1, mode),
        grid=(rows // tm,),
        in_specs=in_specs,
        out_specs=out_specs,
        out_shape=out_shape,
        compiler_params=_cparams(("parallel",)),
        name=name,
    )(*args)


def _prep_kernel(sa, sb, y_ref, qw_ref, kw_ref, mqw_ref, mkw_ref, uq_ref, ukv_ref,
                 ca_ref, sna_ref, cb_ref, snb_ref,
                 qg_ref, kg_ref, vg_ref, qm_ref, km_ref, vm_ref):
    hd = HEAD_DIM
    ca, sna, cb, snb = ca_ref[...], sna_ref[...], cb_ref[...], snb_ref[...]
    ycols = lambda start, width: y_ref[:, start:start + width].astype(F32)

    def head_norm_rope(x, w):
        r = lax.rsqrt(jnp.mean(x * x, axis=-1, keepdims=True) + EPS)
        yh = x * r * w
        return yh * ca + pltpu.roll(yh, hd // 2, axis=1) * sna

    o = 0
    for h in range(HEADS):
        qg_ref[0, h] = (head_norm_rope(ycols(o + h * hd, hd), qw_ref[...]) * sa).astype(BF16)
    o += HALF
    for h in range(GQA_KV_HEADS):
        kg_ref[0, h] = head_norm_rope(ycols(o + h * hd, hd), kw_ref[...]).astype(BF16)
    o += GQA_KV_HEADS * hd
    ones_col = (lax.broadcasted_iota(jnp.int32, (y_ref.shape[0], hd), 1) == 0).astype(BF16)
    for h in range(GQA_KV_HEADS):
        vg_ref[0, h, :, :hd] = y_ref[:, o + h * hd:o + (h + 1) * hd].astype(BF16)
        vg_ref[0, h, :, hd:] = ones_col
    o += GQA_KV_HEADS * hd

    def rms(x, w):
        return (x * lax.rsqrt(jnp.mean(x * x, axis=-1, keepdims=True) + EPS) * w).astype(BF16)

    qb = _dot(rms(ycols(o, MLA_RANK), mqw_ref[...]), uq_ref[...])
    o += MLA_RANK
    kvb = _dot(rms(ycols(o, MLA_RANK), mkw_ref[...]), ukv_ref[...])
    o += MLA_RANK
    kr = (ycols(o, hd) * cb + ycols(o + hd, hd) * snb).astype(BF16)
    for h in range(HEADS):
        lo, hi = h * hd, (h + 1) * hd
        qm_ref[0, h, :, :hd] = (qb[:, lo:hi] * sb).astype(BF16)
        qm_ref[0, h, :, hd:] = ((qb[:, HALF + lo:HALF + hi] * cb + qb[:, 2 * HALF + lo:2 * HALF + hi] * snb) * sb).astype(BF16)
        km_ref[0, h, :, :hd] = kvb[:, lo:hi].astype(BF16)
        km_ref[0, h, :, hd:] = kr
        vm_ref[0, h, :, :hd] = kvb[:, HALF + lo:HALF + hi].astype(BF16)
        vm_ref[0, h, :, hd:] = ones_col


def _attn_prep(y, qw, kw, mqw, mkw, uq, ukv, tabs, batch, seq, ctx_len):
    rows, n = y.shape
    ts = ctx_len
    nl = seq // ts
    n_lat = batch * nl
    t_all = seq + ctx_len
    hd = HEAD_DIM

    def bidx(t):
        return jnp.where(t < n_lat, t // nl, t - n_lat)

    def sidx(t):
        return jnp.where(t < n_lat, t % nl + 1, 0)

    def ridx(t):
        return jnp.where(t < n_lat, t % nl, nl)

    def hm(width, heads):
        return pl.BlockSpec((1, heads, ts, width), lambda t: (bidx(t), 0, sidx(t), 0))

    full = lambda a: pl.BlockSpec(a.shape, lambda t: (0,) * a.ndim)
    tab = pl.BlockSpec((ts, hd), lambda t: (ridx(t), 0))
    outs = [((batch, HEADS, t_all, hd), hm(hd, HEADS)),
            ((batch, GQA_KV_HEADS, t_all, hd), hm(hd, GQA_KV_HEADS)),
            ((batch, GQA_KV_HEADS, t_all, 2 * hd), hm(2 * hd, GQA_KV_HEADS)),
            ((batch, HEADS, t_all, 2 * hd), hm(2 * hd, HEADS)),
            ((batch, HEADS, t_all, 2 * hd), hm(2 * hd, HEADS)),
            ((batch, HEADS, t_all, 2 * hd), hm(2 * hd, HEADS))]
    return pl.pallas_call(
        functools.partial(_prep_kernel, hd ** -0.5, MLA_QK ** -0.5),
        grid=(rows // ts,),
        in_specs=[pl.BlockSpec((ts, n), lambda t: (t, 0)), full(qw), full(kw), full(mqw), full(mkw),
                  full(uq), full(ukv), tab, tab, tab, tab],
        out_specs=[s for _, s in outs],
        out_shape=[jax.ShapeDtypeStruct(sh, BF16) for sh, _ in outs],
        compiler_params=_cparams(("parallel",)),
        name="attn_prep",
    )(y, qw, kw, mqw, mkw, uq, ukv, *tabs)


def _attn_kernel(group, kv_group, ctx_len, q_ref, k_ref, v_ref, o_ref):
    dv = HEAD_DIM

    def run(t_k):
        kv = lambda g: g if kv_group > 1 else 0
        ss = [_dot_nt(q_ref[0, g], k_ref[0, kv(g), :t_k, :]) for g in range(group)]
        for g, s in enumerate(ss):
            p = jnp.exp(s - jnp.max(s, axis=-1, keepdims=True))
            ov = _dot(p.astype(BF16), v_ref[0, kv(g), :t_k, :])
            o_ref[:, g * dv:(g + 1) * dv] = (ov[:, :dv] / ov[:, dv:dv + 1]).astype(o_ref.dtype)

    is_ctx = pl.program_id(2) == 0

    @pl.when(is_ctx)
    def _():
        run(ctx_len)

    @pl.when(jnp.logical_not(is_ctx))
    def _():
        run(k_ref.shape[2])


def _attention(q, k, v, group, kv_group, batch, seq, ctx_len, name):
    _, hq, t_all, dq = q.shape
    dvp = v.shape[-1]
    tq = ctx_len
    nq = seq // tq
    rows = batch * t_all

    def orow(b, qi):
        return jnp.where(qi == 0, batch * nq + b, b * nq + qi - 1)

    return pl.pallas_call(
        functools.partial(_attn_kernel, group, kv_group, ctx_len),
        grid=(batch, hq // group, nq + 1),
        in_specs=[pl.BlockSpec((1, group, tq, dq), lambda b, h, qi: (b, h, qi, 0)),
                  pl.BlockSpec((1, kv_group, t_all, dq), lambda b, h, qi: (b, h, 0, 0)),
                  pl.BlockSpec((1, kv_group, t_all, dvp), lambda b, h, qi: (b, h, 0, 0))],
        out_specs=pl.BlockSpec((tq, group * HEAD_DIM), lambda b, h, qi: (orow(b, qi), h)),
        out_shape=jax.ShapeDtypeStruct((rows, hq * HEAD_DIM), BF16),
        compiler_params=_cparams(("parallel", "parallel", "arbitrary")),
        name=name,
    )(q, k, v)


def _gelu(x):
    return 0.5 * x * (1.0 + jnp.tanh(0.7978845608028654 * (x + 0.044715 * x * x * x)))


def _lru_kernel(seq, ctx_len, rc, xl_ref, xc_ref, gate_ref, cw_ref, cb_ref, w_ref, b_ref, lam_ref,
                o_ref, xp_scr, af_scr, bf_scr, ab_scr, bb_scr, hs_scr):
    cb_w = xl_ref.shape[1]
    nblk = cb_w // LANES
    lam = lam_ref[...]
    log_a_unit = -LRU_C * (jnp.maximum(-lam, 0.0) + jnp.log(1.0 + jnp.exp(-jnp.abs(lam))))
    cw = cw_ref[...]
    cbias = cb_ref[...]
    zeros8 = jnp.zeros((8, cb_w), F32)

    def coeffs(src_ref, n_rows, dst0):
        xp_scr[pl.ds(0, 8), :] = zeros8
        xp_scr[pl.ds(8 + n_rows, 8), :] = zeros8

        def cp(r, carry):
            r0 = pl.multiple_of(r * rc, rc)
            xp_scr[pl.ds(8 + r0, rc), :] = src_ref[pl.ds(r0, rc), :]
            return carry
        lax.fori_loop(0, n_rows // rc, cp, 0)

        def chunk(r, carry):
            r0 = pl.multiple_of(r * rc, rc)
            win = xp_scr[pl.ds(r0, rc + 16), :]
            xc = cbias + cw[2:3] * win[8:8 + rc]
            xc += cw[0:1] * pltpu.roll(win, 2, axis=0)[8:8 + rc]
            xc += cw[1:2] * pltpu.roll(win, 1, axis=0)[8:8 + rc]
            xc += cw[3:4] * pltpu.roll(win, rc + 15, axis=0)[8:8 + rc]
            for n in range(nblk):
                lo, hi = n * LANES, (n + 1) * LANES
                xcn = xc[:, lo:hi]
                z = _dot(xcn.astype(BF16), w_ref[n]) + b_ref[n]
                for d, (a_scr, b_scr) in enumerate(((af_scr, bf_scr), (ab_scr, bb_scr))):
                    r_g = _sigmoid(z[:, (2 * d) * LANES:(2 * d + 1) * LANES])
                    i_g = _sigmoid(z[:, (2 * d + 1) * LANES:(2 * d + 2) * LANES])
                    a = jnp.exp(r_g * log_a_unit[d:d + 1, lo:hi])
                    a_scr[pl.ds(dst0 + r0, rc), lo:hi] = a
                    b_scr[pl.ds(dst0 + r0, rc), lo:hi] = jnp.sqrt(1.0 - a * a) * (i_g * xcn)
            return carry
        lax.fori_loop(0, n_rows // rc, chunk, 0)

    coeffs(xc_ref, ctx_len, 0)
    coeffs(xl_ref, seq, ctx_len)
    t_all = seq + ctx_len

    row = lax.broadcasted_iota(jnp.int32, (8, cb_w), 0)

    def group_scan(a, b, h_in, reverse):
        for sh in (1, 2, 4):
            keep = (row < 8 - sh) if reverse else (row >= sh)
            rot = (8 - sh) if reverse else sh
            a_s = jnp.where(keep, pltpu.roll(a, rot, axis=0), 1.0)
            b_s = jnp.where(keep, pltpu.roll(b, rot, axis=0), 0.0)
            b = a * b_s + b
            a = a * a_s
        h = a * h_in + b
        edge = 0 if reverse else 7
        return h, jnp.broadcast_to(h[edge:edge + 1], (8, cb_w))

    n_groups = t_all // 8
    n_ctx_groups = ctx_len // 8

    def step(g, carry):
        hf_in, hb_in = carry
        r0 = pl.multiple_of(g * 8, 8)
        hf, hf_in = group_scan(af_scr[pl.ds(r0, 8), :], bf_scr[pl.ds(r0, 8), :], hf_in, False)
        hs_scr[pl.ds(r0, 8), :] = hf
        gb = jnp.where(g < n_ctx_groups, n_ctx_groups - 1 - g, n_groups + n_ctx_groups - 1 - g)
        rb0 = pl.multiple_of(gb * 8, 8)
        hb, hb_in = group_scan(ab_scr[pl.ds(rb0, 8), :], bb_scr[pl.ds(rb0, 8), :], hb_in, True)
        bb_scr[pl.ds(rb0, 8), :] = hb
        return hf_in, hb_in
    h0 = jnp.zeros((8, cb_w), F32)
    lax.fori_loop(0, n_groups, step, (h0, h0), unroll=2)

    def fin(r, carry):
        r0 = pl.multiple_of(r * rc, rc)
        hsum = hs_scr[pl.ds(ctx_len + r0, rc), :] + bb_scr[pl.ds(ctx_len + r0, rc), :]
        o_ref[pl.ds(r0, rc), :] = (_gelu(gate_ref[pl.ds(r0, rc), :]) * hsum).astype(o_ref.dtype)
        return carry
    lax.fori_loop(0, seq // rc, fin, 0)


def _lru(y, conv_w, conv_b, w_cat, b_cat, lam, batch, seq, ctx_len):
    cb_w = 2 * LANES
    ncb = HALF // cb_w
    nl = batch * seq
    rc = 256
    t_all = seq + ctx_len
    scr = lambda r: pltpu.VMEM((r, cb_w), F32)
    return pl.pallas_call(
        functools.partial(_lru_kernel, seq, ctx_len, rc),
        grid=(batch, ncb),
        in_specs=[pl.BlockSpec((seq, cb_w), lambda b, c: (b, c)),
                  pl.BlockSpec((ctx_len, cb_w), lambda b, c: (nl // ctx_len + b, c)),
                  pl.BlockSpec((seq, cb_w), lambda b, c: (b, ncb + c)),
                  pl.BlockSpec((4, cb_w), lambda b, c: (0, c)),
                  pl.BlockSpec((1, cb_w), lambda b, c: (0, c)),
                  pl.BlockSpec((cb_w // LANES, LANES, 4 * LANES), lambda b, c: (c, 0, 0)),
                  pl.BlockSpec((cb_w // LANES, 1, 4 * LANES), lambda b, c: (c, 0, 0)),
                  pl.BlockSpec((2, cb_w), lambda b, c: (0, c))],
        out_specs=pl.BlockSpec((seq, cb_w), lambda b, c: (b, c)),
        out_shape=jax.ShapeDtypeStruct((nl, HALF), BF16),
        scratch_shapes=[scr(seq + 16), scr(t_all), scr(t_all), scr(t_all), scr(t_all), scr(t_all)],
        compiler_params=_cparams(("parallel", "parallel")),
        name="rglru",
    )(y, y, y, conv_w, conv_b, w_cat, b_cat, lam)


def _hg_tables():
    c = HG_CHUNK
    masks, pairs = [], []
    for d in range(2):
        r = np.arange(c) if d == 0 else c - 1 - np.arange(c)
        rt, rs = r[:, None], r[None, :]
        m = [rs <= rt]
        pm = []
        w = c // 2
        while w >= 1:
            bnd = w * (2 * (rt // (2 * w)) + 1)
            odd = (rt // w) % 2 == 1
            m.append(np.where(odd, (bnd <= rs) & (rs <= rt), (rt < rs) & (rs <= bnd - 1)))
            pm.append(odd & ((rs // w) % 2 == 0) & (rt // (2 * w) == rs // (2 * w)))
            w //= 2
        pm.append(rt == rs)
        m.append(np.ones((8, c), bool))
        masks.append(np.concatenate(m, axis=0))
        pairs.append(np.stack(pm))
    return np.stack(masks).astype(np.float32), np.stack(pairs).astype(np.float32)


def _hg_kernel(d, final, n_ctx_blk, n_levels, q_ref, f_ref, v_ref, lb_ref, mall_ref, pm_ref, *rest):
    if final:
        oprev_ref, g_ref, nw_ref, o_ref, s_scr = rest
    else:
        o_ref, s_scr = rest
    c, hd = HG_CHUNK, HEAD_DIM
    st = pl.program_id(1)
    n_chunks = HG_BLOCK // c

    @pl.when(st == 0)
    def _():
        s_scr[...] = jnp.zeros_like(s_scr)

    def make_chunk(with_out):
        def chunk(i, carry):
            ci = i if d == 0 else n_chunks - 1 - i
            r0 = pl.multiple_of(ci * c, c)
            mall = mall_ref[...]
            heads = range(HEADS)
            sl = [slice(h * hd, (h + 1) * hd) for h in heads]
            ks, cs2s = [], []
            for h in heads:
                lb = lb_ref[0, :, sl[h]]
                f = lb + (1.0 - lb) * _sigmoid(f_ref[pl.ds(r0, c), sl[h]])
                logf = jnp.log(f)
                lf_hi = logf.astype(BF16)
                lf_lo = (logf - lf_hi.astype(F32)).astype(BF16)
                ks.append(1.0 - f)
                cs2s.append(_dot(mall, jnp.concatenate([lf_hi, lf_lo], axis=1)))
            css = [cs2[:, :hd] + cs2[:, hd:] for cs2 in cs2s]
            es = [jnp.exp(cs) for cs in css]
            rows = lambda h, n: es[h][n * c:(n + 1) * c]
            v16s = [v_ref[pl.ds(r0, c), sl[h]].astype(BF16) for h in heads]
            s_ts = [s_scr[h] for h in heads]
            if with_out:
                qs = []
                for h in heads:
                    qx = q_ref[pl.ds(r0, c), sl[h]]
                    qs.append(qx * _sigmoid(qx))
                a_s = []
                for h in heads:
                    a = pm_ref[n_levels] * _dot_nt(qs[h].astype(BF16), ks[h].astype(BF16))
                    for lv in range(n_levels):
                        el = rows(h, 1 + lv)
                        a += pm_ref[lv] * _dot_nt((qs[h] * el).astype(BF16), (ks[h] * el).astype(BF16))
                    a_s.append(a.astype(BF16))
                for h in heads:
                    o = _dot(a_s[h], v16s[h]) + _dot_nt((qs[h] * rows(h, 0)).astype(BF16), s_ts[h].astype(BF16))
                    if final:
                        o = o + oprev_ref[pl.ds(r0, c), sl[h]]
                        o = o * lax.rsqrt(jnp.mean(o * o, axis=-1, keepdims=True) + EPS) * nw_ref[:, sl[h]]
                        g = g_ref[pl.ds(r0, c), sl[h]]
                        o = o * (g * _sigmoid(g))
                    o_ref[pl.ds(r0, c), sl[h]] = o.astype(o_ref.dtype)
            base = (1 + n_levels) * c
            for h in heads:
                after = jnp.exp(css[h][base:base + 1] - css[h][:c])
                s_scr[h] = es[h][base:base + 1] * s_ts[h] + _dot_tn(v16s[h], (ks[h] * after).astype(BF16))
            return carry
        return chunk

    @pl.when(st >= n_ctx_blk)
    def _():
        lax.fori_loop(0, n_chunks, make_chunk(True), 0, unroll=True)

    @pl.when(st < n_ctx_blk)
    def _():
        lax.fori_loop(0, n_chunks, make_chunk(False), 0)


def _hgrn(y, lb, d, batch, seq, ctx_len, o_prev=None, norm_w=None):
    rb = HG_BLOCK
    n_ctx_blk = ctx_len // rb
    n_lat_blk = seq // rb
    nl = batch * seq
    final = o_prev is not None
    mall_np, pm_np = _hg_tables()
    n_levels = pm_np.shape[1] - 1
    mall = jnp.asarray(mall_np[d], BF16)
    pm = jnp.asarray(pm_np[d], F32)
    q_col, f_col, v_col, g_col = 2, 3 + d, 5, 6

    def lat_blk(st):
        lc = jnp.maximum(st - n_ctx_blk, 0)
        return lc if d == 0 else n_lat_blk - 1 - lc

    def row_blk(b, st):
        cc = st if d == 0 else n_ctx_blk - 1 - st
        return jnp.where(st < n_ctx_blk, nl // rb + b * n_ctx_blk + cc, b * n_lat_blk + lat_blk(st))

    col = lambda cidx: pl.BlockSpec((rb, HALF), lambda b, st: (row_blk(b, st), cidx))
    lat = lambda cidx: pl.BlockSpec((rb, HALF), lambda b, st: (b * n_lat_blk + lat_blk(st), cidx))
    const = lambda a: pl.BlockSpec(a.shape, lambda b, st: (0,) * a.ndim)
    in_specs = [col(q_col), col(f_col), col(v_col),
                pl.BlockSpec((1, 1, HALF), lambda b, st: (d, 0, 0)), const(mall), const(pm)]
    args = [y, y, y, lb, mall, pm]
    if final:
        nw = norm_w.reshape(1, HALF)
        in_specs += [lat(0), lat(g_col), const(nw)]
        args += [o_prev, y, nw]
    return pl.pallas_call(
        functools.partial(_hg_kernel, d, final, n_ctx_blk, n_levels),
        grid=(batch, n_ctx_blk + n_lat_blk),
        in_specs=in_specs,
        out_specs=lat(0),
        out_shape=jax.ShapeDtypeStruct((nl, HALF), BF16 if final else F32),
        scratch_shapes=[pltpu.VMEM((HEADS, HEAD_DIM, HEAD_DIM), F32)],
        compiler_params=_cparams(("parallel", "arbitrary")),
        name=f"hgrn2_dir{d}",
    )(*args)


def _rot_cols(w):
    x1, x2 = jnp.split(w, 2, axis=-1)
    return jnp.concatenate([-x2, x1], axis=-1)


def _pad_lanes(w):
    return jnp.pad(w, [(0, 0)] * (w.ndim - 1) + [(0, LANES - w.shape[-1])])


def _attn_weights(w_in, uq, ukv):
    d = w_in.shape[0]
    kr_w = w_in[:, -MLA_ROPE:]
    w_in_p = jnp.concatenate([w_in[:, :-MLA_ROPE], _pad_lanes(kr_w), _pad_lanes(_rot_cols(kr_w))], axis=1)
    uq3 = uq.reshape(MLA_RANK, HEADS, MLA_QK)
    nope, rope = uq3[..., :HEAD_DIM], uq3[..., HEAD_DIM:]
    uq_p = jnp.concatenate([nope.reshape(MLA_RANK, HALF), _pad_lanes(rope).reshape(MLA_RANK, HALF),
                            _pad_lanes(_rot_cols(rope)).reshape(MLA_RANK, HALF)], axis=1)
    ukv3 = ukv.reshape(MLA_RANK, HEADS, 2 * HEAD_DIM)
    ukv_p = jnp.concatenate([ukv3[..., :HEAD_DIM].reshape(MLA_RANK, HALF),
                             ukv3[..., HEAD_DIM:].reshape(MLA_RANK, HALF)], axis=1)
    del d
    return w_in_p.astype(BF16), uq_p.astype(BF16), ukv_p.astype(BF16)


def _rope_tables(seq, ctx_len):
    rows = seq // GRID_W
    row_id = jnp.repeat(jnp.arange(rows), GRID_W).astype(F32)
    col_id = (jnp.arange(seq) % GRID_W).astype(F32)

    def table(dim):
        quarter = dim // 4
        inv_freq = ROPE_THETA ** (-jnp.arange(quarter, dtype=F32) / quarter)
        ang = jnp.concatenate([row_id[:, None] * inv_freq, col_id[:, None] * inv_freq], axis=-1)
        cos, sin = jnp.cos(ang), jnp.sin(ang)
        cos2 = jnp.concatenate([cos, cos], axis=-1)
        cos2 = jnp.concatenate([cos2, jnp.ones((ctx_len, dim), F32)], axis=0)
        return cos2, cos, sin

    ca, _, sin_a = table(HEAD_DIM)
    sna = jnp.concatenate([jnp.concatenate([-sin_a, sin_a], axis=-1), jnp.zeros((ctx_len, HEAD_DIM), F32)], axis=0)
    cb, _, sin_b = table(MLA_ROPE)
    snb = jnp.concatenate([jnp.concatenate([sin_b, sin_b], axis=-1), jnp.zeros((ctx_len, MLA_ROPE), F32)], axis=0)
    return ca, sna, _pad_lanes(cb), _pad_lanes(snb)


def kernel(x, c, ctx, c_ctx, mod_w, mod_b, norm_mix_w, norm_ffn_w, mix_out_w, ffn_gate_w, ffn_up_w, ffn_down_w, attn_in_w, gqa_q_norm_w, gqa_k_norm_w, mla_q_norm_w, mla_uq_w, mla_kv_norm_w, mla_ukv_w, rec_in_w, lru_conv_w, lru_conv_b, lru_ra_w, lru_ra_b, lru_ix_w, lru_ix_b, lru_lambda, hgrn_lb_logits, hgrn_norm_w, final_norm_w):
    batch, seq, d = x.shape
    ctx_len = ctx.shape[1]
    depth = mod_w.shape[0]
    assert depth == 2 and batch < 8 and seq % ctx_len == 0 and ctx_len == HG_BLOCK
    nl = batch * seq
    rows_all = nl + batch * ctx_len

    cc = jnp.zeros((8, d), F32).at[:batch].set(c).at[batch].set(c_ctx)
    mods = _modulation(cc, mod_w, mod_b).reshape(depth * 8 * MOD_CHUNKS, 1, d)
    x_rows, ctx_rows = x.reshape(nl, d), ctx.reshape(batch * ctx_len, d)
    nm = functools.partial(_norm_mod_matmul, seq=seq, batch=batch)
    um = functools.partial(_u_matmul, seq=seq)
    mrn = functools.partial(_matmul_residual_norm, seq=seq, batch=batch)
    mix_w, down_w = mix_out_w.astype(BF16), ffn_down_w.astype(BF16)
    base1 = 8 * MOD_CHUNKS

    def ffn_hidden(u, rows, l):
        return um(u, rows, [ffn_gate_w, ffn_up_w], l, "swiglu", BF16, tn_prefs=(512, 256, 128), name=f"ffn_up{l}")

    w_in_p, uq_p, ukv_p = _attn_weights(attn_in_w[0], mla_uq_w[0], mla_ukv_w[0])
    y = nm([x_rows, ctx_rows], rows_all, norm_mix_w[0], mods, 0, [w_in_p[None]], 0, "plain", BF16,
           tn_prefs=(1408, 704, 256, 128), name="attn_in")
    qg, kg, vg, qm, km, vm = _attn_prep(
        y, gqa_q_norm_w[0].reshape(1, -1), gqa_k_norm_w[0].reshape(1, -1), mla_q_norm_w[0].reshape(1, -1),
        mla_kv_norm_w[0].reshape(1, -1), uq_p, ukv_p, _rope_tables(seq, ctx_len), batch, seq, ctx_len)
    og = _attention(qg, kg, vg, GQA_GROUP, 1, batch, seq, ctx_len, "gqa")
    om = _attention(qm, km, vm, 4, 4, batch, seq, ctx_len, "mla")
    h, u = mrn([og, om], [(mix_w, 0, 0), (mix_w, 0, 1)], [x_rows, ctx_rows], rows_all, mods, 2,
               norm_ffn_w[0], 3, tm_prefs=(512, 256), name="mix_out0")
    h, u = mrn([ffn_hidden(u, rows_all, 0)], [(down_w, 0, 0)], [h], rows_all, mods, 5,
               norm_mix_w[1], base1, tm_prefs=(256,), name="ffn_down0")

    y = um(u, rows_all, [rec_in_w], 0, "plain", F32, tn_prefs=(1024, 512, 256, 128), name="rec_in")
    w_cat = jnp.concatenate([lru_ra_w[0, 0], lru_ix_w[0, 0], lru_ra_w[0, 1], lru_ix_w[0, 1]], axis=-1).astype(BF16)
    b_cat = jnp.concatenate([lru_ra_b[0, 0], lru_ix_b[0, 0], lru_ra_b[0, 1], lru_ix_b[0, 1]], axis=-1)[:, None, :]
    y_lru = _lru(y, lru_conv_w[0], lru_conv_b[0].reshape(1, -1), w_cat, b_cat, lru_lambda[0], batch, seq, ctx_len)
    lb_all = jnp.cumsum(jax.nn.softmax(hgrn_lb_logits.astype(F32), axis=1), axis=1)
    lb = (lb_all - lb_all[:, :1])[:, 1].reshape(2, 1, HALF)
    o_fwd = _hgrn(y, lb, 0, batch, seq, ctx_len)
    y_hg = _hgrn(y, lb, 1, batch, seq, ctx_len, o_prev=o_fwd, norm_w=hgrn_norm_w[0])
    h, u = mrn([y_lru, y_hg], [(mix_w, 1, 0), (mix_w, 1, 1)], [h], nl, mods, base1 + 2,
               norm_ffn_w[1], base1 + 3, tm_prefs=(512, 256), name="mix_out1")
    out = mrn([ffn_hidden(u, nl, 1)], [(down_w, 1, 0)], [h], nl, mods, base1 + 5,
              final_norm_w, None, tm_prefs=(256,), name="ffn_down1")
    return out.reshape(batch, seq, d)
```

```python
import functools
import math

import numpy as np
import jax
import jax.numpy as jnp
from jax import lax
from jax.experimental import pallas as pl
from jax.experimental.pallas import tpu as pltpu

F32 = jnp.float32
BF16 = jnp.bfloat16

EPS = 1e-6
ROPE_THETA = 10000.0
GRID_W = 64
MOD_CHUNKS = 6
LANES = 128
HEADS = 8
HEAD_DIM = 128
GQA_KV_HEADS = 2
GQA_GROUP = HEADS // GQA_KV_HEADS
MLA_RANK = 512
MLA_ROPE = 64
MLA_QK = HEAD_DIM + MLA_ROPE
HALF = HEADS * HEAD_DIM
LRU_C = 8.0
HG_CHUNK = 64
HG_BLOCK = 256
HG_VREG_LEVEL = 8
VMEM_LIMIT = 56 * 1024 * 1024


def _pick(n, prefs):
    for p in prefs:
        if n % p == 0:
            return p
    raise ValueError(f"no tile for {n} in {prefs}")


def _cparams(sem):
    return pltpu.CompilerParams(dimension_semantics=sem, vmem_limit_bytes=VMEM_LIMIT)


def _dot(a, b):
    return jnp.dot(a, b, preferred_element_type=F32)


def _dot_nt(a, b):
    return lax.dot_general(a, b, (((1,), (1,)), ((), ())), preferred_element_type=F32)


def _dot_tn(a, b):
    return lax.dot_general(a, b, (((0,), (0,)), ((), ())), preferred_element_type=F32)


def _sigmoid(x):
    return 1.0 / (1.0 + jnp.exp(-x))


def _mod_kernel(c_ref, w_ref, b_ref, o_ref):
    c = c_ref[...]
    o_ref[0] = _dot(c * _sigmoid(c), w_ref[0]) + b_ref[0]


def _modulation(cc, mod_w, mod_b):
    depth, d, n = mod_w.shape
    tn = _pick(n, (1024, 512, 256, 128))
    return pl.pallas_call(
        _mod_kernel,
        grid=(depth, n // tn),
        in_specs=[pl.BlockSpec((8, d), lambda l, j: (0, 0)),
                  pl.BlockSpec((1, d, tn), lambda l, j: (l, 0, j)),
                  pl.BlockSpec((1, 1, tn), lambda l, j: (l, 0, j))],
        out_specs=pl.BlockSpec((1, 8, tn), lambda l, j: (l, 0, j)),
        out_shape=jax.ShapeDtypeStruct((depth, 8, n), F32),
        compiler_params=_cparams(("parallel", "parallel")),
        name="modulation",
    )(cc, mod_w, mod_b.reshape(depth, 1, n))


def _row_sources(h_parts, tm):
    n1 = h_parts[0].shape[0] // tm
    if len(h_parts) == 1:
        return n1, [lambda i: i]
    return n1, [lambda i: jnp.minimum(i, n1 - 1), lambda i: jnp.maximum(i - n1, 0)]


def _nm_kernel(n_h, n_w, n1, epilogue, tm, rc, *refs):
    h_refs = refs[:n_h]
    nw_ref, sh_ref, sc_ref = refs[n_h:n_h + 3]
    w_refs = refs[n_h + 3:n_h + 3 + n_w]
    o_ref, u_scr = refs[n_h + 3 + n_w:]
    i = pl.program_id(0)

    def prologue(h_ref):
        def chunk(r, carry):
            r0 = pl.multiple_of(r * rc, rc)
            x = h_ref[pl.ds(r0, rc), :]
            ms = jnp.mean(x * x, axis=-1, keepdims=True)
            y = x * lax.rsqrt(ms + EPS) * nw_ref[...]
            u_scr[pl.ds(r0, rc), :] = (y * (1.0 + sc_ref[0]) + sh_ref[0]).astype(BF16)
            return carry
        lax.fori_loop(0, tm // rc, chunk, 0, unroll=8)

    first = pl.program_id(1) == 0
    if n_h == 1:
        pl.when(first)(lambda: prologue(h_refs[0]))
    else:
        pl.when(first & (i < n1))(lambda: prologue(h_refs[0]))
        pl.when(first & (i >= n1))(lambda: prologue(h_refs[1]))

    u = u_scr[...]
    if epilogue == "plain":
        o_ref[...] = _dot(u, w_refs[0][0].astype(BF16)).astype(o_ref.dtype)
    else:
        g = _dot(u, w_refs[0][0].astype(BF16))
        up = _dot(u, w_refs[1][0].astype(BF16))
        o_ref[...] = (g * _sigmoid(g) * up).astype(o_ref.dtype)


def _norm_mod_matmul(h_parts, rows, norm_w, mods, mod_base, ws, layer, epilogue, out_dtype, seq, batch, tn_prefs, name):
    d = h_parts[0].shape[1]
    n = ws[0].shape[2]
    tm = _pick(math.gcd(seq, rows), (1024, 512, 256))
    tn = _pick(n, tn_prefs)
    rc = 32
    n1, rmaps = _row_sources(h_parts, tm)

    def mrow(i, k):
        return mod_base + jnp.minimum(i * tm // seq, batch) * MOD_CHUNKS + k

    in_specs = [pl.BlockSpec((tm, d), lambda i, j, m=m: (m(i), 0), pipeline_mode=(pl.Buffered(1) if k else None))
                for k, m in enumerate(rmaps)]
    in_specs += [pl.BlockSpec((1, d), lambda i, j: (0, 0)),
                 pl.BlockSpec((1, 1, d), lambda i, j: (mrow(i, 0), 0, 0)),
                 pl.BlockSpec((1, 1, d), lambda i, j: (mrow(i, 1), 0, 0))]
    in_specs += [pl.BlockSpec((1, d, tn), lambda i, j: (layer, 0, j)) for _ in ws]
    return pl.pallas_call(
        functools.partial(_nm_kernel, len(h_parts), len(ws), n1, epilogue, tm, rc),
        grid=(rows // tm, n // tn),
        in_specs=in_specs,
        out_specs=pl.BlockSpec((tm, tn), lambda i, j: (i, j)),
        out_shape=jax.ShapeDtypeStruct((rows, n), out_dtype),
        scratch_shapes=[pltpu.VMEM((tm, d), BF16)],
        compiler_params=_cparams(("parallel", "arbitrary")),
        name=name,
    )(*h_parts, norm_w.reshape(1, d), mods, mods, *ws)


def _um_kernel(n_w, epilogue, u_ref, *refs):
    w_refs, o_ref = refs[:n_w], refs[n_w]
    u = u_ref[...]
    if epilogue == "plain":
        o_ref[...] = _dot(u, w_refs[0][0].astype(BF16)).astype(o_ref.dtype)
    else:
        g = _dot(u, w_refs[0][0].astype(BF16))
        up = _dot(u, w_refs[1][0].astype(BF16))
        o_ref[...] = (g * _sigmoid(g) * up).astype(o_ref.dtype)


def _u_matmul(u, rows, ws, layer, epilogue, out_dtype, seq, tn_prefs, name):
    d = u.shape[1]
    n = ws[0].shape[2]
    tm = _pick(math.gcd(seq, rows), (1024, 512, 256))
    tn = _pick(n, tn_prefs)
    return pl.pallas_call(
        functools.partial(_um_kernel, len(ws), epilogue),
        grid=(rows // tm, n // tn),
        in_specs=[pl.BlockSpec((tm, d), lambda i, j: (i, 0))]
        + [pl.BlockSpec((1, d, tn), lambda i, j: (layer, 0, j)) for _ in ws],
        out_specs=pl.BlockSpec((tm, tn), lambda i, j: (i, j)),
        out_shape=jax.ShapeDtypeStruct((rows, n), out_dtype),
        compiler_params=_cparams(("parallel", "arbitrary")),
        name=name,
    )(u, *ws)


def _mrf_kernel(n_a, n_r, n1, mode, *refs):
    a_refs, w_refs = refs[:n_a], refs[n_a:2 * n_a]
    res_refs = refs[2 * n_a:2 * n_a + n_r]
    rest = refs[2 * n_a + n_r:]
    if mode == "next":
        g_ref, nw_ref, sh_ref, sc_ref, h_ref, u_ref = rest
    else:
        g_ref, nw_ref, o_ref = rest
    acc = _dot(a_refs[0][...], w_refs[0][0])
    for a, w in zip(a_refs[1:], w_refs[1:]):
        acc += _dot(a[...], w[0])
    upd = g_ref[0] * acc

    def finish(res):
        h = res + upd
        y = h * lax.rsqrt(jnp.mean(h * h, axis=-1, keepdims=True) + EPS) * nw_ref[...]
        if mode == "next":
            h_ref[...] = h
            u_ref[...] = (y * (1.0 + sc_ref[0]) + sh_ref[0]).astype(BF16)
        else:
            o_ref[...] = y

    if n_r == 1:
        finish(res_refs[0][...])
    else:
        finish(jnp.where(pl.program_id(0) < n1, res_refs[0][...], res_refs[1][...]))


def _matmul_residual_norm(a_list, w_specs, res_parts, rows, mods, gate_row0, norm_w, next_row0, seq, batch, tm_prefs, name):
    d = res_parts[0].shape[1]
    tm = _pick(math.gcd(seq, rows), tm_prefs)
    n1, rmaps = _row_sources(res_parts, tm)
    mode = "final" if next_row0 is None else "next"

    def mrow(i, base):
        return base + jnp.minimum(i * tm // seq, batch) * MOD_CHUNKS

    row_tile = lambda w: pl.BlockSpec((tm, w), lambda i: (i, 0))
    vec = lambda base: pl.BlockSpec((1, 1, d), lambda i: (mrow(i, base), 0, 0))
    in_specs = [row_tile(a.shape[1]) for a in a_list]
    in_specs += [pl.BlockSpec((1, a.shape[1], d), lambda i, l=l, kb=kb: (l, kb, 0), pipeline_mode=pl.Buffered(1))
                 for a, (_, l, kb) in zip(a_list, w_specs)]
    in_specs += [pl.BlockSpec((tm, d), lambda i, m=m: (m(i), 0)) for m in rmaps]
    in_specs += [vec(gate_row0), pl.BlockSpec((1, d), lambda i: (0, 0))]
    args = [*a_list, *[w for w, _, _ in w_specs], *res_parts, mods, norm_w.reshape(1, d)]
    if mode == "next":
        in_specs += [vec(next_row0), vec(next_row0 + 1)]
        args += [mods, mods]
        out_specs = [row_tile(d), row_tile(d)]
        out_shape = [jax.ShapeDtypeStruct((rows, d), F32), jax.ShapeDtypeStruct((rows, d), BF16)]
    else:
        out_specs = row_tile(d)
        out_shape = jax.ShapeDtypeStruct((rows, d), F32)
    return pl.pallas_call(
        functools.partial(_mrf_kernel, len(a_list), len(res_parts), n1, mode),
        grid=(rows // tm,),
        in_specs=in_specs,
        out_specs=out_specs,
        out_shape=out_shape,
        compiler_params=_cparams(("parallel",)),
        name=name,
    )(*args)


def _prep_kernel(sa, sb, y_ref, qw_ref, kw_ref, mqw_ref, mkw_ref, uq_ref, ukv_ref,
                 ca_ref, sna_ref, cb_ref, snb_ref,
                 qg_ref, kg_ref, vg_ref, qm_ref, km_ref, vm_ref):
    hd = HEAD_DIM
    ca, sna, cb, snb = ca_ref[...], sna_ref[...], cb_ref[...], snb_ref[...]
    ycols = lambda start, width: y_ref[:, start:start + width].astype(F32)

    def head_norm_rope(x, w):
        r = lax.rsqrt(jnp.mean(x * x, axis=-1, keepdims=True) + EPS)
        yh = x * r * w
        return yh * ca + pltpu.roll(yh, hd // 2, axis=1) * sna

    o = 0
    for h in range(HEADS):
        qg_ref[0, h] = (head_norm_rope(ycols(o + h * hd, hd), qw_ref[...]) * sa).astype(BF16)
    o += HALF
    for h in range(GQA_KV_HEADS):
        kg_ref[0, h] = head_norm_rope(ycols(o + h * hd, hd), kw_ref[...]).astype(BF16)
    o += GQA_KV_HEADS * hd
    ones_col = (lax.broadcasted_iota(jnp.int32, (y_ref.shape[0], hd), 1) == 0).astype(BF16)
    for h in range(GQA_KV_HEADS):
        vg_ref[0, h, :, :hd] = y_ref[:, o + h * hd:o + (h + 1) * hd].astype(BF16)
        vg_ref[0, h, :, hd:] = ones_col
    o += GQA_KV_HEADS * hd

    def rms(x, w):
        return (x * lax.rsqrt(jnp.mean(x * x, axis=-1, keepdims=True) + EPS) * w).astype(BF16)

    qb = _dot(rms(ycols(o, MLA_RANK), mqw_ref[...]), uq_ref[...])
    o += MLA_RANK
    kvb = _dot(rms(ycols(o, MLA_RANK), mkw_ref[...]), ukv_ref[...])
    o += MLA_RANK
    kr = (ycols(o, hd) * cb + ycols(o + hd, hd) * snb).astype(BF16)
    for h in range(HEADS):
        lo, hi = h * hd, (h + 1) * hd
        qm_ref[0, h, :, :hd] = (qb[:, lo:hi] * sb).astype(BF16)
        qm_ref[0, h, :, hd:] = ((qb[:, HALF + lo:HALF + hi] * cb + qb[:, 2 * HALF + lo:2 * HALF + hi] * snb) * sb).astype(BF16)
        km_ref[0, h, :, :hd] = kvb[:, lo:hi].astype(BF16)
        km_ref[0, h, :, hd:] = kr
        vm_ref[0, h, :, :hd] = kvb[:, HALF + lo:HALF + hi].astype(BF16)
        vm_ref[0, h, :, hd:] = ones_col


def _attn_prep(y, qw, kw, mqw, mkw, uq, ukv, tabs, batch, seq, ctx_len):
    rows, n = y.shape
    ts = ctx_len
    nl = seq // ts
    n_lat = batch * nl
    t_all = seq + ctx_len
    hd = HEAD_DIM

    def bidx(t):
        return jnp.where(t < n_lat, t // nl, t - n_lat)

    def sidx(t):
        return jnp.where(t < n_lat, t % nl + 1, 0)

    def ridx(t):
        return jnp.where(t < n_lat, t % nl, nl)

    def hm(width, heads):
        return pl.BlockSpec((1, heads, ts, width), lambda t: (bidx(t), 0, sidx(t), 0))

    full = lambda a: pl.BlockSpec(a.shape, lambda t: (0,) * a.ndim)
    tab = pl.BlockSpec((ts, hd), lambda t: (ridx(t), 0))
    outs = [((batch, HEADS, t_all, hd), hm(hd, HEADS)),
            ((batch, GQA_KV_HEADS, t_all, hd), hm(hd, GQA_KV_HEADS)),
            ((batch, GQA_KV_HEADS, t_all, 2 * hd), hm(2 * hd, GQA_KV_HEADS)),
            ((batch, HEADS, t_all, 2 * hd), hm(2 * hd, HEADS)),
            ((batch, HEADS, t_all, 2 * hd), hm(2 * hd, HEADS)),
            ((batch, HEADS, t_all, 2 * hd), hm(2 * hd, HEADS))]
    return pl.pallas_call(
        functools.partial(_prep_kernel, hd ** -0.5, MLA_QK ** -0.5),
        grid=(rows // ts,),
        in_specs=[pl.BlockSpec((ts, n), lambda t: (t, 0)), full(qw), full(kw), full(mqw), full(mkw),
                  full(uq), full(ukv), tab, tab, tab, tab],
        out_specs=[s for _, s in outs],
        out_shape=[jax.ShapeDtypeStruct(sh, BF16) for sh, _ in outs],
        compiler_params=_cparams(("parallel",)),
        name="attn_prep",
    )(y, qw, kw, mqw, mkw, uq, ukv, *tabs)


def _attn_kernel(group, kv_group, ctx_len, q_ref, k_ref, v_ref, o_ref):
    dv = HEAD_DIM

    def run(t_k):
        kv = lambda g: g if kv_group > 1 else 0
        ss = [_dot_nt(q_ref[0, g], k_ref[0, kv(g), :t_k, :]) for g in range(group)]
        for g, s in enumerate(ss):
            p = jnp.exp(s - jnp.max(s, axis=-1, keepdims=True))
            ov = _dot(p.astype(BF16), v_ref[0, kv(g), :t_k, :])
            o_ref[:, g * dv:(g + 1) * dv] = (ov[:, :dv] / ov[:, dv:dv + 1]).astype(o_ref.dtype)

    is_ctx = pl.program_id(2) == 0

    @pl.when(is_ctx)
    def _():
        run(ctx_len)

    @pl.when(jnp.logical_not(is_ctx))
    def _():
        run(k_ref.shape[2])


def _attention(q, k, v, group, kv_group, batch, seq, ctx_len, name):
    _, hq, t_all, dq = q.shape
    dvp = v.shape[-1]
    tq = ctx_len
    nq = seq // tq
    rows = batch * t_all

    def orow(b, qi):
        return jnp.where(qi == 0, batch * nq + b, b * nq + qi - 1)

    return pl.pallas_call(
        functools.partial(_attn_kernel, group, kv_group, ctx_len),
        grid=(batch, hq // group, nq + 1),
        in_specs=[pl.BlockSpec((1, group, tq, dq), lambda b, h, qi: (b, h, qi, 0)),
                  pl.BlockSpec((1, kv_group, t_all, dq), lambda b, h, qi: (b, h, 0, 0)),
                  pl.BlockSpec((1, kv_group, t_all, dvp), lambda b, h, qi: (b, h, 0, 0))],
        out_specs=pl.BlockSpec((tq, group * HEAD_DIM), lambda b, h, qi: (orow(b, qi), h)),
        out_shape=jax.ShapeDtypeStruct((rows, hq * HEAD_DIM), BF16),
        compiler_params=_cparams(("parallel", "parallel", "arbitrary")),
        name=name,
    )(q, k, v)


def _gelu(x):
    return 0.5 * x * (1.0 + jnp.tanh(0.7978845608028654 * (x + 0.044715 * x * x * x)))


def _lru_kernel(seq, ctx_len, rc, xl_ref, xc_ref, gate_ref, cw_ref, cb_ref, w_ref, b_ref, lam_ref,
                o_ref, xp_scr, af_scr, bf_scr, ab_scr, bb_scr, hs_scr):
    cb_w = xl_ref.shape[1]
    nblk = cb_w // LANES
    lam = lam_ref[...]
    log_a_unit = -LRU_C * (jnp.maximum(-lam, 0.0) + jnp.log(1.0 + jnp.exp(-jnp.abs(lam))))
    cw = cw_ref[...]
    cbias = cb_ref[...]
    zeros8 = jnp.zeros((8, cb_w), F32)

    def coeffs(src_ref, n_rows, dst0):
        xp_scr[pl.ds(0, 8), :] = zeros8
        xp_scr[pl.ds(8 + n_rows, 8), :] = zeros8

        def cp(r, carry):
            r0 = pl.multiple_of(r * rc, rc)
            xp_scr[pl.ds(8 + r0, rc), :] = src_ref[pl.ds(r0, rc), :]
            return carry
        lax.fori_loop(0, n_rows // rc, cp, 0)

        def chunk(r, carry):
            r0 = pl.multiple_of(r * rc, rc)
            win = xp_scr[pl.ds(r0, rc + 16), :]
            xc = cbias + cw[2:3] * win[8:8 + rc]
            xc += cw[0:1] * pltpu.roll(win, 2, axis=0)[8:8 + rc]
            xc += cw[1:2] * pltpu.roll(win, 1, axis=0)[8:8 + rc]
            xc += cw[3:4] * pltpu.roll(win, rc + 15, axis=0)[8:8 + rc]
            for n in range(nblk):
                lo, hi = n * LANES, (n + 1) * LANES
                xcn = xc[:, lo:hi]
                z = _dot(xcn.astype(BF16), w_ref[n]) + b_ref[n]
                for d, (a_scr, b_scr) in enumerate(((af_scr, bf_scr), (ab_scr, bb_scr))):
                    r_g = _sigmoid(z[:, (2 * d) * LANES:(2 * d + 1) * LANES])
                    i_g = _sigmoid(z[:, (2 * d + 1) * LANES:(2 * d + 2) * LANES])
                    a = jnp.exp(r_g * log_a_unit[d:d + 1, lo:hi])
                    a_scr[pl.ds(dst0 + r0, rc), lo:hi] = a
                    b_scr[pl.ds(dst0 + r0, rc), lo:hi] = jnp.sqrt(1.0 - a * a) * (i_g * xcn)
            return carry
        lax.fori_loop(0, n_rows // rc, chunk, 0)

    coeffs(xc_ref, ctx_len, 0)
    coeffs(xl_ref, seq, ctx_len)
    t_all = seq + ctx_len

    row = lax.broadcasted_iota(jnp.int32, (8, cb_w), 0)

    def group_scan(a, b, h_in, reverse):
        for sh in (1, 2, 4):
            keep = (row < 8 - sh) if reverse else (row >= sh)
            rot = (8 - sh) if reverse else sh
            a_s = jnp.where(keep, pltpu.roll(a, rot, axis=0), 1.0)
            b_s = jnp.where(keep, pltpu.roll(b, rot, axis=0), 0.0)
            b = a * b_s + b
            a = a * a_s
        h = a * h_in + b
        edge = 0 if reverse else 7
        return h, jnp.broadcast_to(h[edge:edge + 1], (8, cb_w))

    n_groups = t_all // 8
    n_ctx_groups = ctx_len // 8

    def step(g, carry):
        hf_in, hb_in = carry
        r0 = pl.multiple_of(g * 8, 8)
        hf, hf_in = group_scan(af_scr[pl.ds(r0, 8), :], bf_scr[pl.ds(r0, 8), :], hf_in, False)
        hs_scr[pl.ds(r0, 8), :] = hf
        gb = jnp.where(g < n_ctx_groups, n_ctx_groups - 1 - g, n_groups + n_ctx_groups - 1 - g)
        rb0 = pl.multiple_of(gb * 8, 8)
        hb, hb_in = group_scan(ab_scr[pl.ds(rb0, 8), :], bb_scr[pl.ds(rb0, 8), :], hb_in, True)
        bb_scr[pl.ds(rb0, 8), :] = hb
        return hf_in, hb_in
    h0 = jnp.zeros((8, cb_w), F32)
    lax.fori_loop(0, n_groups, step, (h0, h0), unroll=2)

    def fin(r, carry):
        r0 = pl.multiple_of(r * rc, rc)
        hsum = hs_scr[pl.ds(ctx_len + r0, rc), :] + bb_scr[pl.ds(ctx_len + r0, rc), :]
        o_ref[pl.ds(r0, rc), :] = (_gelu(gate_ref[pl.ds(r0, rc), :]) * hsum).astype(o_ref.dtype)
        return carry
    lax.fori_loop(0, seq // rc, fin, 0)


def _lru(y, conv_w, conv_b, w_cat, b_cat, lam, batch, seq, ctx_len):
    cb_w = 2 * LANES
    ncb = HALF // cb_w
    nl = batch * seq
    rc = 256
    t_all = seq + ctx_len
    scr = lambda r: pltpu.VMEM((r, cb_w), F32)
    return pl.pallas_call(
        functools.partial(_lru_kernel, seq, ctx_len, rc),
        grid=(batch, ncb),
        in_specs=[pl.BlockSpec((seq, cb_w), lambda b, c: (b, c)),
                  pl.BlockSpec((ctx_len, cb_w), lambda b, c: (nl // ctx_len + b, c)),
                  pl.BlockSpec((seq, cb_w), lambda b, c: (b, ncb + c)),
                  pl.BlockSpec((4, cb_w), lambda b, c: (0, c)),
                  pl.BlockSpec((1, cb_w), lambda b, c: (0, c)),
                  pl.BlockSpec((cb_w // LANES, LANES, 4 * LANES), lambda b, c: (c, 0, 0)),
                  pl.BlockSpec((cb_w // LANES, 1, 4 * LANES), lambda b, c: (c, 0, 0)),
                  pl.BlockSpec((2, cb_w), lambda b, c: (0, c))],
        out_specs=pl.BlockSpec((seq, cb_w), lambda b, c: (b, c)),
        out_shape=jax.ShapeDtypeStruct((nl, HALF), BF16),
        scratch_shapes=[scr(seq + 16), scr(t_all), scr(t_all), scr(t_all), scr(t_all), scr(t_all)],
        compiler_params=_cparams(("parallel", "parallel")),
        name="rglru",
    )(y, y, y, conv_w, conv_b, w_cat, b_cat, lam)


def _hg_tables():
    c = HG_CHUNK
    masks, pairs = [], []
    for d in range(2):
        r = np.arange(c) if d == 0 else c - 1 - np.arange(c)
        rt, rs = r[:, None], r[None, :]
        m = [rs <= rt]
        pm = []
        w = c // 2
        while w >= 1:
            bnd = w * (2 * (rt // (2 * w)) + 1)
            odd = (rt // w) % 2 == 1
            if w < HG_VREG_LEVEL:
                m.append(np.where(odd, (bnd <= rs) & (rs <= rt), (rt < rs) & (rs <= bnd - 1)))
            pm.append(odd & ((rs // w) % 2 == 0) & (rt // (2 * w) == rs // (2 * w)))
            w //= 2
        pm.append(rt == rs)
        m.append(np.ones((8, c), bool))
        masks.append(np.concatenate(m, axis=0))
        pairs.append(np.stack(pm))
    return np.stack(masks).astype(np.float32), np.stack(pairs).astype(np.float32)


def _hg_kernel(d, final, n_ctx_blk, n_levels, q_ref, f_ref, v_ref, lb_ref, mall_ref, pm_ref, *rest):
    if final:
        oprev_ref, g_ref, nw_ref, o_ref, s_scr = rest
    else:
        o_ref, s_scr = rest
    c, hd = HG_CHUNK, HEAD_DIM
    st = pl.program_id(1)
    n_chunks = HG_BLOCK // c

    @pl.when(st == 0)
    def _():
        s_scr[...] = jnp.zeros_like(s_scr)

    def make_chunk(with_out):
        def chunk(i, carry):
            ci = i if d == 0 else n_chunks - 1 - i
            r0 = pl.multiple_of(ci * c, c)
            mall = mall_ref[...]
            heads = range(HEADS)
            sl = [slice(h * hd, (h + 1) * hd) for h in heads]
            ks, cs2s = [], []
            for h in heads:
                lb = lb_ref[0, :, sl[h]]
                f = lb + (1.0 - lb) * _sigmoid(f_ref[pl.ds(r0, c), sl[h]])
                logf = jnp.log(f)
                lf_hi = logf.astype(BF16)
                lf_lo = (logf - lf_hi.astype(F32)).astype(BF16)
                ks.append(1.0 - f)
                cs2s.append(_dot(mall, jnp.concatenate([lf_hi, lf_lo], axis=1)))
            css = [cs2[:, :hd] + cs2[:, hd:] for cs2 in cs2s]
            es = [jnp.exp(cs) for cs in css]
            rows = lambda h, n: es[h][n * c:(n + 1) * c]
            n_wide = n_levels - (HG_VREG_LEVEL.bit_length() - 1)

            def level_decay(h, lv):
                if lv >= n_wide:
                    return rows(h, 1 + lv - n_wide)
                w = c >> (lv + 1)
                cum = css[h][:c]
                parts = [jnp.broadcast_to(cum[b:b + 1], (2 * w, hd))
                         for b in (a + w - 1 + d for a in range(0, c, 2 * w))]
                cum_b = parts[0] if len(parts) == 1 else jnp.concatenate(parts, axis=0)
                return jnp.exp(-jnp.abs(cum - cum_b))
            v16s = [v_ref[pl.ds(r0, c), sl[h]].astype(BF16) for h in heads]
            s_ts = [s_scr[h] for h in heads]
            if with_out:
                qs = []
                for h in heads:
                    qx = q_ref[pl.ds(r0, c), sl[h]]
                    qs.append(qx * _sigmoid(qx))
                a_s = []
                for h in heads:
                    a = pm_ref[n_levels] * _dot_nt(qs[h].astype(BF16), ks[h].astype(BF16))
                    for lv in range(n_levels):
                        el = level_decay(h, lv)
                        a += pm_ref[lv] * _dot_nt((qs[h] * el).astype(BF16), (ks[h] * el).astype(BF16))
                    a_s.append(a.astype(BF16))
                for h in heads:
                    o = _dot(a_s[h], v16s[h]) + _dot_nt((qs[h] * rows(h, 0)).astype(BF16), s_ts[h].astype(BF16))
                    if final:
                        o = o + oprev_ref[pl.ds(r0, c), sl[h]]
                        o = o * lax.rsqrt(jnp.mean(o * o, axis=-1, keepdims=True) + EPS) * nw_ref[:, sl[h]]
                        g = g_ref[pl.ds(r0, c), sl[h]]
                        o = o * (g * _sigmoid(g))
                    o_ref[pl.ds(r0, c), sl[h]] = o.astype(o_ref.dtype)
            base = (1 + n_levels - n_wide) * c
            for h in heads:
                after = jnp.exp(css[h][base:base + 1] - css[h][:c])
                s_scr[h] = es[h][base:base + 1] * s_ts[h] + _dot_tn(v16s[h], (ks[h] * after).astype(BF16))
            return carry
        return chunk

    @pl.when(st >= n_ctx_blk)
    def _():
        lax.fori_loop(0, n_chunks, make_chunk(True), 0, unroll=True)

    @pl.when(st < n_ctx_blk)
    def _():
        lax.fori_loop(0, n_chunks, make_chunk(False), 0)


def _hgrn(y, lb, d, batch, seq, ctx_len, o_prev=None, norm_w=None):
    rb = HG_BLOCK
    n_ctx_blk = ctx_len // rb
    n_lat_blk = seq // rb
    nl = batch * seq
    final = o_prev is not None
    mall_np, pm_np = _hg_tables()
    n_levels = pm_np.shape[1] - 1
    mall = jnp.asarray(mall_np[d], BF16)
    pm = jnp.asarray(pm_np[d], F32)
    q_col, f_col, v_col, g_col = 2, 3 + d, 5, 6

    def lat_blk(st):
        lc = jnp.maximum(st - n_ctx_blk, 0)
        return lc if d == 0 else n_lat_blk - 1 - lc

    def row_blk(b, st):
        cc = st if d == 0 else n_ctx_blk - 1 - st
        return jnp.where(st < n_ctx_blk, nl // rb + b * n_ctx_blk + cc, b * n_lat_blk + lat_blk(st))

    col = lambda cidx: pl.BlockSpec((rb, HALF), lambda b, st: (row_blk(b, st), cidx))
    lat = lambda cidx: pl.BlockSpec((rb, HALF), lambda b, st: (b * n_lat_blk + lat_blk(st), cidx))
    const = lambda a: pl.BlockSpec(a.shape, lambda b, st: (0,) * a.ndim)
    in_specs = [col(q_col), col(f_col), col(v_col),
                pl.BlockSpec((1, 1, HALF), lambda b, st: (d, 0, 0)), const(mall), const(pm)]
    args = [y, y, y, lb, mall, pm]
    if final:
        nw = norm_w.reshape(1, HALF)
        in_specs += [lat(0), lat(g_col), const(nw)]
        args += [o_prev, y, nw]
    return pl.pallas_call(
        functools.partial(_hg_kernel, d, final, n_ctx_blk, n_levels),
        grid=(batch, n_ctx_blk + n_lat_blk),
        in_specs=in_specs,
        out_specs=lat(0),
        out_shape=jax.ShapeDtypeStruct((nl, HALF), BF16 if final else F32),
        scratch_shapes=[pltpu.VMEM((HEADS, HEAD_DIM, HEAD_DIM), F32)],
        compiler_params=_cparams(("parallel", "arbitrary")),
        name=f"hgrn2_dir{d}",
    )(*args)


def _rot_cols(w):
    x1, x2 = jnp.split(w, 2, axis=-1)
    return jnp.concatenate([-x2, x1], axis=-1)


def _pad_lanes(w):
    return jnp.pad(w, [(0, 0)] * (w.ndim - 1) + [(0, LANES - w.shape[-1])])


def _attn_weights(w_in, uq, ukv):
    d = w_in.shape[0]
    kr_w = w_in[:, -MLA_ROPE:]
    w_in_p = jnp.concatenate([w_in[:, :-MLA_ROPE], _pad_lanes(kr_w), _pad_lanes(_rot_cols(kr_w))], axis=1)
    uq3 = uq.reshape(MLA_RANK, HEADS, MLA_QK)
    nope, rope = uq3[..., :HEAD_DIM], uq3[..., HEAD_DIM:]
    uq_p = jnp.concatenate([nope.reshape(MLA_RANK, HALF), _pad_lanes(rope).reshape(MLA_RANK, HALF),
                            _pad_lanes(_rot_cols(rope)).reshape(MLA_RANK, HALF)], axis=1)
    ukv3 = ukv.reshape(MLA_RANK, HEADS, 2 * HEAD_DIM)
    ukv_p = jnp.concatenate([ukv3[..., :HEAD_DIM].reshape(MLA_RANK, HALF),
                             ukv3[..., HEAD_DIM:].reshape(MLA_RANK, HALF)], axis=1)
    del d
    return w_in_p.astype(BF16), uq_p.astype(BF16), ukv_p.astype(BF16)


def _rope_tables(seq, ctx_len):
    rows = seq // GRID_W
    row_id = jnp.repeat(jnp.arange(rows), GRID_W).astype(F32)
    col_id = (jnp.arange(seq) % GRID_W).astype(F32)

    def table(dim):
        quarter = dim // 4
        inv_freq = ROPE_THETA ** (-jnp.arange(quarter, dtype=F32) / quarter)
        ang = jnp.concatenate([row_id[:, None] * inv_freq, col_id[:, None] * inv_freq], axis=-1)
        cos, sin = jnp.cos(ang), jnp.sin(ang)
        cos2 = jnp.concatenate([cos, cos], axis=-1)
        cos2 = jnp.concatenate([cos2, jnp.ones((ctx_len, dim), F32)], axis=0)
        return cos2, cos, sin

    ca, _, sin_a = table(HEAD_DIM)
    sna = jnp.concatenate([jnp.concatenate([-sin_a, sin_a], axis=-1), jnp.zeros((ctx_len, HEAD_DIM), F32)], axis=0)
    cb, _, sin_b = table(MLA_ROPE)
    snb = jnp.concatenate([jnp.concatenate([sin_b, sin_b], axis=-1), jnp.zeros((ctx_len, MLA_ROPE), F32)], axis=0)
    return ca, sna, _pad_lanes(cb), _pad_lanes(snb)


def kernel(x, c, ctx, c_ctx, mod_w, mod_b, norm_mix_w, norm_ffn_w, mix_out_w, ffn_gate_w, ffn_up_w, ffn_down_w, attn_in_w, gqa_q_norm_w, gqa_k_norm_w, mla_q_norm_w, mla_uq_w, mla_kv_norm_w, mla_ukv_w, rec_in_w, lru_conv_w, lru_conv_b, lru_ra_w, lru_ra_b, lru_ix_w, lru_ix_b, lru_lambda, hgrn_lb_logits, hgrn_norm_w, final_norm_w):
    batch, seq, d = x.shape
    ctx_len = ctx.shape[1]
    depth = mod_w.shape[0]
    assert depth == 2 and batch < 8 and seq % ctx_len == 0 and ctx_len == HG_BLOCK
    nl = batch * seq
    rows_all = nl + batch * ctx_len

    cc = jnp.zeros((8, d), F32).at[:batch].set(c).at[batch].set(c_ctx)
    mods = _modulation(cc, mod_w, mod_b).reshape(depth * 8 * MOD_CHUNKS, 1, d)
    x_rows, ctx_rows = x.reshape(nl, d), ctx.reshape(batch * ctx_len, d)
    nm = functools.partial(_norm_mod_matmul, seq=seq, batch=batch)
    um = functools.partial(_u_matmul, seq=seq)
    mrn = functools.partial(_matmul_residual_norm, seq=seq, batch=batch)
    mix_w, down_w = mix_out_w.astype(BF16), ffn_down_w.astype(BF16)
    base1 = 8 * MOD_CHUNKS

    def ffn_hidden(u, rows, l):
        return um(u, rows, [ffn_gate_w, ffn_up_w], l, "swiglu", BF16, tn_prefs=(512, 256, 128), name=f"ffn_up{l}")

    w_in_p, uq_p, ukv_p = _attn_weights(attn_in_w[0], mla_uq_w[0], mla_ukv_w[0])
    y = nm([x_rows, ctx_rows], rows_all, norm_mix_w[0], mods, 0, [w_in_p[None]], 0, "plain", BF16,
           tn_prefs=(1408, 704, 256, 128), name="attn_in")
    qg, kg, vg, qm, km, vm = _attn_prep(
        y, gqa_q_norm_w[0].reshape(1, -1), gqa_k_norm_w[0].reshape(1, -1), mla_q_norm_w[0].reshape(1, -1),
        mla_kv_norm_w[0].reshape(1, -1), uq_p, ukv_p, _rope_tables(seq, ctx_len), batch, seq, ctx_len)
    og = _attention(qg, kg, vg, GQA_GROUP, 1, batch, seq, ctx_len, "gqa")
    om = _attention(qm, km, vm, 4, 4, batch, seq, ctx_len, "mla")
    h, u = mrn([og, om], [(mix_w, 0, 0), (mix_w, 0, 1)], [x_rows, ctx_rows], rows_all, mods, 2,
               norm_ffn_w[0], 3, tm_prefs=(512, 256), name="mix_out0")
    h, u = mrn([ffn_hidden(u, rows_all, 0)], [(down_w, 0, 0)], [h], rows_all, mods, 5,
               norm_mix_w[1], base1, tm_prefs=(256,), name="ffn_down0")

    y = um(u, rows_all, [rec_in_w], 0, "plain", F32, tn_prefs=(1024, 512, 256, 128), name="rec_in")
    w_cat = jnp.concatenate([lru_ra_w[0, 0], lru_ix_w[0, 0], lru_ra_w[0, 1], lru_ix_w[0, 1]], axis=-1).astype(BF16)
    b_cat = jnp.concatenate([lru_ra_b[0, 0], lru_ix_b[0, 0], lru_ra_b[0, 1], lru_ix_b[0, 1]], axis=-1)[:, None, :]
    y_lru = _lru(y, lru_conv_w[0], lru_conv_b[0].reshape(1, -1), w_cat, b_cat, lru_lambda[0], batch, seq, ctx_len)
    lb_all = jnp.cumsum(jax.nn.softmax(hgrn_lb_logits.astype(F32), axis=1), axis=1)
    lb = (lb_all - lb_all[:, :1])[:, 1].reshape(2, 1, HALF)
    o_fwd = _hgrn(y, lb, 0, batch, seq, ctx_len)
    y_hg = _hgrn(y, lb, 1, batch, seq, ctx_len, o_prev=o_fwd, norm_w=hgrn_norm_w[0])
    h, u = mrn([y_lru, y_hg], [(mix_w, 1, 0), (mix_w, 1, 1)], [h], nl, mods, base1 + 2,
               norm_ffn_w[1], base1 + 3, tm_prefs=(512, 256), name="mix_out1")
    out = mrn([ffn_hidden(u, nl, 1)], [(down_w, 1, 0)], [h], nl, mods, base1 + 5,
              final_norm_w, None, tm_prefs=(256,), name="ffn_down1")
    return out.reshape(batch, seq, d)
```

```python
import functools
import math

import numpy as np
import jax
import jax.numpy as jnp
from jax import lax
from jax.experimental import pallas as pl
from jax.experimental.pallas import tpu as pltpu

F32 = jnp.float32
BF16 = jnp.bfloat16

EPS = 1e-6
ROPE_THETA = 10000.0
GRID_W = 64
MOD_CHUNKS = 6
LANES = 128
HEADS = 8
HEAD_DIM = 128
GQA_KV_HEADS = 2
GQA_GROUP = HEADS // GQA_KV_HEADS
MLA_RANK = 512
MLA_ROPE = 64
MLA_QK = HEAD_DIM + MLA_ROPE
HALF = HEADS * HEAD_DIM
LRU_C = 8.0
HG_CHUNK = 64
HG_BLOCK = 256
HG_VREG_LEVEL = 8
VMEM_LIMIT = 56 * 1024 * 1024


def _pick(n, prefs):
    for p in prefs:
        if n % p == 0:
            return p
    raise ValueError(f"no tile for {n} in {prefs}")


def _cparams(sem):
    return pltpu.CompilerParams(dimension_semantics=sem, vmem_limit_bytes=VMEM_LIMIT)


def _dot(a, b):
    return jnp.dot(a, b, preferred_element_type=F32)


def _dot_nt(a, b):
    return lax.dot_general(a, b, (((1,), (1,)), ((), ())), preferred_element_type=F32)


def _dot_tn(a, b):
    return lax.dot_general(a, b, (((0,), (0,)), ((), ())), preferred_element_type=F32)


def _sigmoid(x):
    return 0.5 * jnp.tanh(0.5 * x) + 0.5


def _mod_kernel(c_ref, w_ref, b_ref, o_ref):
    c = c_ref[...]
    o_ref[0] = _dot(c * _sigmoid(c), w_ref[0]) + b_ref[0]


def _modulation(cc, mod_w, mod_b):
    depth, d, n = mod_w.shape
    tn = _pick(n, (1024, 512, 256, 128))
    return pl.pallas_call(
        _mod_kernel,
        grid=(depth, n // tn),
        in_specs=[pl.BlockSpec((8, d), lambda l, j: (0, 0)),
                  pl.BlockSpec((1, d, tn), lambda l, j: (l, 0, j)),
                  pl.BlockSpec((1, 1, tn), lambda l, j: (l, 0, j))],
        out_specs=pl.BlockSpec((1, 8, tn), lambda l, j: (l, 0, j)),
        out_shape=jax.ShapeDtypeStruct((depth, 8, n), F32),
        compiler_params=_cparams(("parallel", "parallel")),
        name="modulation",
    )(cc, mod_w, mod_b.reshape(depth, 1, n))


def _row_sources(h_parts, tm):
    n1 = h_parts[0].shape[0] // tm
    if len(h_parts) == 1:
        return n1, [lambda i: i]
    return n1, [lambda i: jnp.minimum(i, n1 - 1), lambda i: jnp.maximum(i - n1, 0)]


def _nm_kernel(n_h, n_w, n1, epilogue, tm, rc, *refs):
    h_refs = refs[:n_h]
    nw_ref, sh_ref, sc_ref = refs[n_h:n_h + 3]
    w_refs = refs[n_h + 3:n_h + 3 + n_w]
    o_ref, u_scr = refs[n_h + 3 + n_w:]
    i = pl.program_id(0)

    def prologue(h_ref):
        def chunk(r, carry):
            r0 = pl.multiple_of(r * rc, rc)
            x = h_ref[pl.ds(r0, rc), :]
            ms = jnp.mean(x * x, axis=-1, keepdims=True)
            y = x * lax.rsqrt(ms + EPS) * nw_ref[...]
            u_scr[pl.ds(r0, rc), :] = (y * (1.0 + sc_ref[0]) + sh_ref[0]).astype(BF16)
            return carry
        lax.fori_loop(0, tm // rc, chunk, 0, unroll=8)

    first = pl.program_id(1) == 0
    if n_h == 1:
        pl.when(first)(lambda: prologue(h_refs[0]))
    else:
        pl.when(first & (i < n1))(lambda: prologue(h_refs[0]))
        pl.when(first & (i >= n1))(lambda: prologue(h_refs[1]))

    u = u_scr[...]
    if epilogue == "plain":
        o_ref[...] = _dot(u, w_refs[0][0].astype(BF16)).astype(o_ref.dtype)
    else:
        g = _dot(u, w_refs[0][0].astype(BF16))
        up = _dot(u, w_refs[1][0].astype(BF16))
        o_ref[...] = (g * _sigmoid(g) * up).astype(o_ref.dtype)


def _norm_mod_matmul(h_parts, rows, norm_w, mods, mod_base, ws, layer, epilogue, out_dtype, seq, batch, tn_prefs, name):
    d = h_parts[0].shape[1]
    n = ws[0].shape[2]
    tm = _pick(math.gcd(seq, rows), (1024, 512, 256))
    tn = _pick(n, tn_prefs)
    rc = 32
    n1, rmaps = _row_sources(h_parts, tm)

    def mrow(i, k):
        return mod_base + jnp.minimum(i * tm // seq, batch) * MOD_CHUNKS + k

    in_specs = [pl.BlockSpec((tm, d), lambda i, j, m=m: (m(i), 0), pipeline_mode=(pl.Buffered(1) if k else None))
                for k, m in enumerate(rmaps)]
    in_specs += [pl.BlockSpec((1, d), lambda i, j: (0, 0)),
                 pl.BlockSpec((1, 1, d), lambda i, j: (mrow(i, 0), 0, 0)),
                 pl.BlockSpec((1, 1, d), lambda i, j: (mrow(i, 1), 0, 0))]
    in_specs += [pl.BlockSpec((1, d, tn), lambda i, j: (layer, 0, j)) for _ in ws]
    return pl.pallas_call(
        functools.partial(_nm_kernel, len(h_parts), len(ws), n1, epilogue, tm, rc),
        grid=(rows // tm, n // tn),
        in_specs=in_specs,
        out_specs=pl.BlockSpec((tm, tn), lambda i, j: (i, j)),
        out_shape=jax.ShapeDtypeStruct((rows, n), out_dtype),
        scratch_shapes=[pltpu.VMEM((tm, d), BF16)],
        compiler_params=_cparams(("parallel", "arbitrary")),
        name=name,
    )(*h_parts, norm_w.reshape(1, d), mods, mods, *ws)


def _um_kernel(n_w, epilogue, u_ref, *refs):
    w_refs, o_ref = refs[:n_w], refs[n_w]
    u = u_ref[...]
    if epilogue == "plain":
        o_ref[...] = _dot(u, w_refs[0][0].astype(BF16)).astype(o_ref.dtype)
    else:
        g = _dot(u, w_refs[0][0].astype(BF16))
        up = _dot(u, w_refs[1][0].astype(BF16))
        o_ref[...] = (g * _sigmoid(g) * up).astype(o_ref.dtype)


def _u_matmul(u, rows, ws, layer, epilogue, out_dtype, seq, tn_prefs, name):
    d = u.shape[1]
    n = ws[0].shape[2]
    tm = _pick(math.gcd(seq, rows), (1024, 512, 256))
    tn = _pick(n, tn_prefs)
    return pl.pallas_call(
        functools.partial(_um_kernel, len(ws), epilogue),
        grid=(rows // tm, n // tn),
        in_specs=[pl.BlockSpec((tm, d), lambda i, j: (i, 0))]
        + [pl.BlockSpec((1, d, tn), lambda i, j: (layer, 0, j)) for _ in ws],
        out_specs=pl.BlockSpec((tm, tn), lambda i, j: (i, j)),
        out_shape=jax.ShapeDtypeStruct((rows, n), out_dtype),
        compiler_params=_cparams(("parallel", "arbitrary")),
        name=name,
    )(u, *ws)


def _mrf_kernel(n_a, n_r, n1, mode, *refs):
    a_refs, w_refs = refs[:n_a], refs[n_a:2 * n_a]
    res_refs = refs[2 * n_a:2 * n_a + n_r]
    rest = refs[2 * n_a + n_r:]
    if mode == "next":
        g_ref, nw_ref, sh_ref, sc_ref, h_ref, u_ref = rest
    else:
        g_ref, nw_ref, o_ref = rest
    acc = _dot(a_refs[0][...], w_refs[0][0])
    for a, w in zip(a_refs[1:], w_refs[1:]):
        acc += _dot(a[...], w[0])
    upd = g_ref[0] * acc

    def finish(res):
        h = res + upd
        y = h * lax.rsqrt(jnp.mean(h * h, axis=-1, keepdims=True) + EPS) * nw_ref[...]
        if mode == "next":
            h_ref[...] = h
            u_ref[...] = (y * (1.0 + sc_ref[0]) + sh_ref[0]).astype(BF16)
        else:
            o_ref[...] = y

    if n_r == 1:
        finish(res_refs[0][...])
    else:
        finish(jnp.where(pl.program_id(0) < n1, res_refs[0][...], res_refs[1][...]))


def _matmul_residual_norm(a_list, w_specs, res_parts, rows, mods, gate_row0, norm_w, next_row0, seq, batch, tm_prefs, name):
    d = res_parts[0].shape[1]
    tm = _pick(math.gcd(seq, rows), tm_prefs)
    n1, rmaps = _row_sources(res_parts, tm)
    mode = "final" if next_row0 is None else "next"

    def mrow(i, base):
        return base + jnp.minimum(i * tm // seq, batch) * MOD_CHUNKS

    row_tile = lambda w: pl.BlockSpec((tm, w), lambda i: (i, 0))
    vec = lambda base: pl.BlockSpec((1, 1, d), lambda i: (mrow(i, base), 0, 0))
    in_specs = [row_tile(a.shape[1]) for a in a_list]
    in_specs += [pl.BlockSpec((1, a.shape[1], d), lambda i, l=l, kb=kb: (l, kb, 0), pipeline_mode=pl.Buffered(1))
                 for a, (_, l, kb) in zip(a_list, w_specs)]
    in_specs += [pl.BlockSpec((tm, d), lambda i, m=m: (m(i), 0)) for m in rmaps]
    in_specs += [vec(gate_row0), pl.BlockSpec((1, d), lambda i: (0, 0))]
    args = [*a_list, *[w for w, _, _ in w_specs], *res_parts, mods, norm_w.reshape(1, d)]
    if mode == "next":
        in_specs += [vec(next_row0), vec(next_row0 + 1)]
        args += [mods, mods]
        out_specs = [row_tile(d), row_tile(d)]
        out_shape = [jax.ShapeDtypeStruct((rows, d), F32), jax.ShapeDtypeStruct((rows, d), BF16)]
    else:
        out_specs = row_tile(d)
        out_shape = jax.ShapeDtypeStruct((rows, d), F32)
    return pl.pallas_call(
        functools.partial(_mrf_kernel, len(a_list), len(res_parts), n1, mode),
        grid=(rows // tm,),
        in_specs=in_specs,
        out_specs=out_specs,
        out_shape=out_shape,
        compiler_params=_cparams(("parallel",)),
        name=name,
    )(*args)


def _prep_kernel(sa, sb, y_ref, qw_ref, kw_ref, mqw_ref, mkw_ref, uq_ref, ukv_ref,
                 ca_ref, sna_ref, cb_ref, snb_ref,
                 qg_ref, kg_ref, vg_ref, qm_ref, km_ref, vm_ref):
    hd = HEAD_DIM
    ca, sna, cb, snb = ca_ref[...], sna_ref[...], cb_ref[...], snb_ref[...]
    ycols = lambda start, width: y_ref[:, start:start + width].astype(F32)

    def head_norm_rope(x, w):
        r = lax.rsqrt(jnp.mean(x * x, axis=-1, keepdims=True) + EPS)
        yh = x * r * w
        return yh * ca + pltpu.roll(yh, hd // 2, axis=1) * sna

    o = 0
    for h in range(HEADS):
        qg_ref[0, h] = (head_norm_rope(ycols(o + h * hd, hd), qw_ref[...]) * sa).astype(BF16)
    o += HALF
    for h in range(GQA_KV_HEADS):
        kg_ref[0, h] = head_norm_rope(ycols(o + h * hd, hd), kw_ref[...]).astype(BF16)
    o += GQA_KV_HEADS * hd
    ones_col = (lax.broadcasted_iota(jnp.int32, (y_ref.shape[0], hd), 1) == 0).astype(BF16)
    for h in range(GQA_KV_HEADS):
        vg_ref[0, h, :, :hd] = y_ref[:, o + h * hd:o + (h + 1) * hd].astype(BF16)
        vg_ref[0, h, :, hd:] = ones_col
    o += GQA_KV_HEADS * hd

    def rms(x, w):
        return (x * lax.rsqrt(jnp.mean(x * x, axis=-1, keepdims=True) + EPS) * w).astype(BF16)

    qb = _dot(rms(ycols(o, MLA_RANK), mqw_ref[...]), uq_ref[...])
    o += MLA_RANK
    kvb = _dot(rms(ycols(o, MLA_RANK), mkw_ref[...]), ukv_ref[...])
    o += MLA_RANK
    kr = (ycols(o, hd) * cb + ycols(o + hd, hd) * snb).astype(BF16)
    for h in range(HEADS):
        lo, hi = h * hd, (h + 1) * hd
        qm_ref[0, h, :, :hd] = (qb[:, lo:hi] * sb).astype(BF16)
        qm_ref[0, h, :, hd:] = ((qb[:, HALF + lo:HALF + hi] * cb + qb[:, 2 * HALF + lo:2 * HALF + hi] * snb) * sb).astype(BF16)
        km_ref[0, h, :, :hd] = kvb[:, lo:hi].astype(BF16)
        km_ref[0, h, :, hd:] = kr
        vm_ref[0, h, :, :hd] = kvb[:, HALF + lo:HALF + hi].astype(BF16)
        vm_ref[0, h, :, hd:] = ones_col


def _attn_prep(y, qw, kw, mqw, mkw, uq, ukv, tabs, batch, seq, ctx_len):
    rows, n = y.shape
    ts = ctx_len
    nl = seq // ts
    n_lat = batch * nl
    t_all = seq + ctx_len
    hd = HEAD_DIM

    def bidx(t):
        return jnp.where(t < n_lat, t // nl, t - n_lat)

    def sidx(t):
        return jnp.where(t < n_lat, t % nl + 1, 0)

    def ridx(t):
        return jnp.where(t < n_lat, t % nl, nl)

    def hm(width, heads):
        return pl.BlockSpec((1, heads, ts, width), lambda t: (bidx(t), 0, sidx(t), 0))

    full = lambda a: pl.BlockSpec(a.shape, lambda t: (0,) * a.ndim)
    tab = pl.BlockSpec((ts, hd), lambda t: (ridx(t), 0))
    outs = [((batch, HEADS, t_all, hd), hm(hd, HEADS)),
            ((batch, GQA_KV_HEADS, t_all, hd), hm(hd, GQA_KV_HEADS)),
            ((batch, GQA_KV_HEADS, t_all, 2 * hd), hm(2 * hd, GQA_KV_HEADS)),
            ((batch, HEADS, t_all, 2 * hd), hm(2 * hd, HEADS)),
            ((batch, HEADS, t_all, 2 * hd), hm(2 * hd, HEADS)),
            ((batch, HEADS, t_all, 2 * hd), hm(2 * hd, HEADS))]
    return pl.pallas_call(
        functools.partial(_prep_kernel, hd ** -0.5, MLA_QK ** -0.5),
        grid=(rows // ts,),
        in_specs=[pl.BlockSpec((ts, n), lambda t: (t, 0)), full(qw), full(kw), full(mqw), full(mkw),
                  full(uq), full(ukv), tab, tab, tab, tab],
        out_specs=[s for _, s in outs],
        out_shape=[jax.ShapeDtypeStruct(sh, BF16) for sh, _ in outs],
        compiler_params=_cparams(("parallel",)),
        name="attn_prep",
    )(y, qw, kw, mqw, mkw, uq, ukv, *tabs)


def _attn_kernel(group, kv_group, ctx_len, q_ref, k_ref, v_ref, o_ref):
    dv = HEAD_DIM

    def run(t_k):
        kv = lambda g: g if kv_group > 1 else 0
        ss = [_dot_nt(q_ref[0, g], k_ref[0, kv(g), :t_k, :]) for g in range(group)]
        for g, s in enumerate(ss):
            p = jnp.exp(s - jnp.max(s, axis=-1, keepdims=True))
            ov = _dot(p.astype(BF16), v_ref[0, kv(g), :t_k, :])
            o_ref[:, g * dv:(g + 1) * dv] = (ov[:, :dv] / ov[:, dv:dv + 1]).astype(o_ref.dtype)

    is_ctx = pl.program_id(2) == 0

    @pl.when(is_ctx)
    def _():
        run(ctx_len)

    @pl.when(jnp.logical_not(is_ctx))
    def _():
        run(k_ref.shape[2])


def _attention(q, k, v, group, kv_group, batch, seq, ctx_len, name):
    _, hq, t_all, dq = q.shape
    dvp = v.shape[-1]
    tq = ctx_len
    nq = seq // tq
    rows = batch * t_all

    def orow(b, qi):
        return jnp.where(qi == 0, batch * nq + b, b * nq + qi - 1)

    return pl.pallas_call(
        functools.partial(_attn_kernel, group, kv_group, ctx_len),
        grid=(batch, hq // group, nq + 1),
        in_specs=[pl.BlockSpec((1, group, tq, dq), lambda b, h, qi: (b, h, qi, 0)),
                  pl.BlockSpec((1, kv_group, t_all, dq), lambda b, h, qi: (b, h, 0, 0)),
                  pl.BlockSpec((1, kv_group, t_all, dvp), lambda b, h, qi: (b, h, 0, 0))],
        out_specs=pl.BlockSpec((tq, group * HEAD_DIM), lambda b, h, qi: (orow(b, qi), h)),
        out_shape=jax.ShapeDtypeStruct((rows, hq * HEAD_DIM), BF16),
        compiler_params=_cparams(("parallel", "parallel", "arbitrary")),
        name=name,
    )(q, k, v)


def _gelu(x):
    return 0.5 * x * (1.0 + jnp.tanh(0.7978845608028654 * (x + 0.044715 * x * x * x)))


def _lru_kernel(seq, ctx_len, rc, xl_ref, xc_ref, gate_ref, cw_ref, cb_ref, w_ref, b_ref, lam_ref,
                o_ref, xp_scr, af_scr, bf_scr, ab_scr, bb_scr, hs_scr):
    cb_w = xl_ref.shape[1]
    nblk = cb_w // LANES
    lam = lam_ref[...]
    log_a_unit = -LRU_C * (jnp.maximum(-lam, 0.0) + jnp.log(1.0 + jnp.exp(-jnp.abs(lam))))
    cw = cw_ref[...]
    cbias = cb_ref[...]
    zeros8 = jnp.zeros((8, cb_w), F32)

    def coeffs(src_ref, n_rows, dst0):
        xp_scr[pl.ds(0, 8), :] = zeros8
        xp_scr[pl.ds(8 + n_rows, 8), :] = zeros8

        def cp(r, carry):
            r0 = pl.multiple_of(r * rc, rc)
            xp_scr[pl.ds(8 + r0, rc), :] = src_ref[pl.ds(r0, rc), :]
            return carry
        lax.fori_loop(0, n_rows // rc, cp, 0)

        def chunk(r, carry):
            r0 = pl.multiple_of(r * rc, rc)
            win = xp_scr[pl.ds(r0, rc + 16), :]
            xc = cbias + cw[2:3] * win[8:8 + rc]
            xc += cw[0:1] * pltpu.roll(win, 2, axis=0)[8:8 + rc]
            xc += cw[1:2] * pltpu.roll(win, 1, axis=0)[8:8 + rc]
            xc += cw[3:4] * pltpu.roll(win, rc + 15, axis=0)[8:8 + rc]
            for n in range(nblk):
                lo, hi = n * LANES, (n + 1) * LANES
                xcn = xc[:, lo:hi]
                z = _dot(xcn.astype(BF16), w_ref[n]) + b_ref[n]
                for d, (a_scr, b_scr) in enumerate(((af_scr, bf_scr), (ab_scr, bb_scr))):
                    r_g = _sigmoid(z[:, (2 * d) * LANES:(2 * d + 1) * LANES])
                    i_g = _sigmoid(z[:, (2 * d + 1) * LANES:(2 * d + 2) * LANES])
                    a = jnp.exp(r_g * log_a_unit[d:d + 1, lo:hi])
                    a_scr[pl.ds(dst0 + r0, rc), lo:hi] = a
                    om = 1.0 - a * a
                    mult = jnp.where(om > 0.0, om * lax.rsqrt(om), 0.0)
                    b_scr[pl.ds(dst0 + r0, rc), lo:hi] = mult * (i_g * xcn)
            return carry
        lax.fori_loop(0, n_rows // rc, chunk, 0)

    coeffs(xc_ref, ctx_len, 0)
    coeffs(xl_ref, seq, ctx_len)
    t_all = seq + ctx_len

    row = lax.broadcasted_iota(jnp.int32, (8, cb_w), 0)

    def group_scan(a, b, h_in, reverse):
        for sh in (1, 2, 4):
            keep = (row < 8 - sh) if reverse else (row >= sh)
            rot = (8 - sh) if reverse else sh
            a_s = jnp.where(keep, pltpu.roll(a, rot, axis=0), 1.0)
            b_s = jnp.where(keep, pltpu.roll(b, rot, axis=0), 0.0)
            b = a * b_s + b
            a = a * a_s
        h = a * h_in + b
        edge = 0 if reverse else 7
        return h, jnp.broadcast_to(h[edge:edge + 1], (8, cb_w))

    n_groups = t_all // 8
    n_ctx_groups = ctx_len // 8

    def step(g, carry):
        hf_in, hb_in = carry
        r0 = pl.multiple_of(g * 8, 8)
        hf, hf_in = group_scan(af_scr[pl.ds(r0, 8), :], bf_scr[pl.ds(r0, 8), :], hf_in, False)
        hs_scr[pl.ds(r0, 8), :] = hf
        gb = jnp.where(g < n_ctx_groups, n_ctx_groups - 1 - g, n_groups + n_ctx_groups - 1 - g)
        rb0 = pl.multiple_of(gb * 8, 8)
        hb, hb_in = group_scan(ab_scr[pl.ds(rb0, 8), :], bb_scr[pl.ds(rb0, 8), :], hb_in, True)
        bb_scr[pl.ds(rb0, 8), :] = hb
        return hf_in, hb_in
    h0 = jnp.zeros((8, cb_w), F32)
    lax.fori_loop(0, n_groups, step, (h0, h0), unroll=4)

    def fin(r, carry):
        r0 = pl.multiple_of(r * rc, rc)
        hsum = hs_scr[pl.ds(ctx_len + r0, rc), :] + bb_scr[pl.ds(ctx_len + r0, rc), :]
        o_ref[pl.ds(r0, rc), :] = (_gelu(gate_ref[pl.ds(r0, rc), :]) * hsum).astype(o_ref.dtype)
        return carry
    lax.fori_loop(0, seq // rc, fin, 0)


def _lru(y, conv_w, conv_b, w_cat, b_cat, lam, batch, seq, ctx_len):
    cb_w = 2 * LANES
    ncb = HALF // cb_w
    nl = batch * seq
    rc = 256
    t_all = seq + ctx_len
    scr = lambda r: pltpu.VMEM((r, cb_w), F32)
    return pl.pallas_call(
        functools.partial(_lru_kernel, seq, ctx_len, rc),
        grid=(batch, ncb),
        in_specs=[pl.BlockSpec((seq, cb_w), lambda b, c: (b, c)),
                  pl.BlockSpec((ctx_len, cb_w), lambda b, c: (nl // ctx_len + b, c)),
                  pl.BlockSpec((seq, cb_w), lambda b, c: (b, ncb + c)),
                  pl.BlockSpec((4, cb_w), lambda b, c: (0, c)),
                  pl.BlockSpec((1, cb_w), lambda b, c: (0, c)),
                  pl.BlockSpec((cb_w // LANES, LANES, 4 * LANES), lambda b, c: (c, 0, 0)),
                  pl.BlockSpec((cb_w // LANES, 1, 4 * LANES), lambda b, c: (c, 0, 0)),
                  pl.BlockSpec((2, cb_w), lambda b, c: (0, c))],
        out_specs=pl.BlockSpec((seq, cb_w), lambda b, c: (b, c)),
        out_shape=jax.ShapeDtypeStruct((nl, HALF), BF16),
        scratch_shapes=[scr(seq + 16), scr(t_all), scr(t_all), scr(t_all), scr(t_all), scr(t_all)],
        compiler_params=_cparams(("parallel", "parallel")),
        name="rglru",
    )(y, y, y, conv_w, conv_b, w_cat, b_cat, lam)


def _hg_tables():
    c = HG_CHUNK
    masks, pairs = [], []
    for d in range(2):
        r = np.arange(c) if d == 0 else c - 1 - np.arange(c)
        rt, rs = r[:, None], r[None, :]
        m = [rs <= rt]
        pm = []
        w = c // 2
        while w >= 1:
            bnd = w * (2 * (rt // (2 * w)) + 1)
            odd = (rt // w) % 2 == 1
            if w < HG_VREG_LEVEL:
                m.append(np.where(odd, (bnd <= rs) & (rs <= rt), (rt < rs) & (rs <= bnd - 1)))
            pm.append(odd & ((rs // w) % 2 == 0) & (rt // (2 * w) == rs // (2 * w)))
            w //= 2
        pm.append(rt == rs)
        m.append(np.ones((8, c), bool))
        masks.append(np.concatenate(m, axis=0))
        pairs.append(np.stack(pm))
    return np.stack(masks).astype(np.float32), np.stack(pairs).astype(np.float32)


def _hg_kernel(d, final, n_ctx_blk, n_levels, q_ref, f_ref, v_ref, lb_ref, mall_ref, pm_ref, *rest):
    if final:
        oprev_ref, g_ref, nw_ref, o_ref, s_scr = rest
    else:
        o_ref, s_scr = rest
    c, hd = HG_CHUNK, HEAD_DIM
    st = pl.program_id(1)
    n_chunks = HG_BLOCK // c

    @pl.when(st == 0)
    def _():
        s_scr[...] = jnp.zeros_like(s_scr)

    def make_chunk(with_out):
        def chunk(i, carry):
            ci = i if d == 0 else n_chunks - 1 - i
            r0 = pl.multiple_of(ci * c, c)
            mall = mall_ref[...]
            heads = range(HEADS)
            sl = [slice(h * hd, (h + 1) * hd) for h in heads]
            ks, cs2s = [], []
            for h in heads:
                lb = lb_ref[0, :, sl[h]]
                f = lb + (1.0 - lb) * _sigmoid(f_ref[pl.ds(r0, c), sl[h]])
                logf = jnp.log(f)
                lf_hi = logf.astype(BF16)
                lf_lo = (logf - lf_hi.astype(F32)).astype(BF16)
                ks.append(1.0 - f)
                cs2s.append(_dot(mall, jnp.concatenate([lf_hi, lf_lo], axis=1)))
            css = [cs2[:, :hd] + cs2[:, hd:] for cs2 in cs2s]
            es = [jnp.exp(cs) for cs in css]
            rows = lambda h, n: es[h][n * c:(n + 1) * c]
            n_wide = n_levels - (HG_VREG_LEVEL.bit_length() - 1)

            def level_decay(h, lv):
                if lv >= n_wide:
                    return rows(h, 1 + lv - n_wide)
                w = c >> (lv + 1)
                cum = css[h][:c]
                parts = [jnp.broadcast_to(cum[b:b + 1], (2 * w, hd))
                         for b in (a + w - 1 + d for a in range(0, c, 2 * w))]
                cum_b = parts[0] if len(parts) == 1 else jnp.concatenate(parts, axis=0)
                return jnp.exp(-jnp.abs(cum - cum_b))
            v16s = [v_ref[pl.ds(r0, c), sl[h]].astype(BF16) for h in heads]
            s_ts = [s_scr[h] for h in heads]
            if with_out:
                qs = []
                for h in heads:
                    qx = q_ref[pl.ds(r0, c), sl[h]]
                    qs.append(qx * _sigmoid(qx))
                a_s = []
                for h in heads:
                    a = pm_ref[n_levels] * _dot_nt(qs[h].astype(BF16), ks[h].astype(BF16))
                    for lv in range(n_levels):
                        el = level_decay(h, lv)
                        a += pm_ref[lv] * _dot_nt((qs[h] * el).astype(BF16), (ks[h] * el).astype(BF16))
                    a_s.append(a.astype(BF16))
                for h in heads:
                    o = _dot(a_s[h], v16s[h]) + _dot_nt((qs[h] * rows(h, 0)).astype(BF16), s_ts[h].astype(BF16))
                    if final:
                        o = o + oprev_ref[pl.ds(r0, c), sl[h]]
                        o = o * lax.rsqrt(jnp.mean(o * o, axis=-1, keepdims=True) + EPS) * nw_ref[:, sl[h]]
                        g = g_ref[pl.ds(r0, c), sl[h]]
                        o = o * (g * _sigmoid(g))
                    o_ref[pl.ds(r0, c), sl[h]] = o.astype(o_ref.dtype)
            base = (1 + n_levels - n_wide) * c
            for h in heads:
                after = jnp.exp(css[h][base:base + 1] - css[h][:c])
                s_scr[h] = es[h][base:base + 1] * s_ts[h] + _dot_tn(v16s[h], (ks[h] * after).astype(BF16))
            return carry
        return chunk

    @pl.when(st >= n_ctx_blk)
    def _():
        lax.fori_loop(0, n_chunks, make_chunk(True), 0, unroll=True)

    @pl.when(st < n_ctx_blk)
    def _():
        lax.fori_loop(0, n_chunks, make_chunk(False), 0)


def _hgrn(y, lb, d, batch, seq, ctx_len, o_prev=None, norm_w=None):
    rb = HG_BLOCK
    n_ctx_blk = ctx_len // rb
    n_lat_blk = seq // rb
    nl = batch * seq
    final = o_prev is not None
    mall_np, pm_np = _hg_tables()
    n_levels = pm_np.shape[1] - 1
    mall = jnp.asarray(mall_np[d], BF16)
    pm = jnp.asarray(pm_np[d], F32)
    q_col, f_col, v_col, g_col = 2, 3 + d, 5, 6

    def lat_blk(st):
        lc = jnp.maximum(st - n_ctx_blk, 0)
        return lc if d == 0 else n_lat_blk - 1 - lc

    def row_blk(b, st):
        cc = st if d == 0 else n_ctx_blk - 1 - st
        return jnp.where(st < n_ctx_blk, nl // rb + b * n_ctx_blk + cc, b * n_lat_blk + lat_blk(st))

    col = lambda cidx: pl.BlockSpec((rb, HALF), lambda b, st: (row_blk(b, st), cidx))
    lat = lambda cidx: pl.BlockSpec((rb, HALF), lambda b, st: (b * n_lat_blk + lat_blk(st), cidx))
    const = lambda a: pl.BlockSpec(a.shape, lambda b, st: (0,) * a.ndim)
    in_specs = [col(q_col), col(f_col), col(v_col),
                pl.BlockSpec((1, 1, HALF), lambda b, st: (d, 0, 0)), const(mall), const(pm)]
    args = [y, y, y, lb, mall, pm]
    if final:
        nw = norm_w.reshape(1, HALF)
        in_specs += [lat(0), lat(g_col), const(nw)]
        args += [o_prev, y, nw]
    return pl.pallas_call(
        functools.partial(_hg_kernel, d, final, n_ctx_blk, n_levels),
        grid=(batch, n_ctx_blk + n_lat_blk),
        in_specs=in_specs,
        out_specs=lat(0),
        out_shape=jax.ShapeDtypeStruct((nl, HALF), BF16 if final else F32),
        scratch_shapes=[pltpu.VMEM((HEADS, HEAD_DIM, HEAD_DIM), F32)],
        compiler_params=_cparams(("parallel", "arbitrary")),
        name=f"hgrn2_dir{d}",
    )(*args)


def _rot_cols(w):
    x1, x2 = jnp.split(w, 2, axis=-1)
    return jnp.concatenate([-x2, x1], axis=-1)


def _pad_lanes(w):
    return jnp.pad(w, [(0, 0)] * (w.ndim - 1) + [(0, LANES - w.shape[-1])])


def _attn_weights(w_in, uq, ukv):
    d = w_in.shape[0]
    kr_w = w_in[:, -MLA_ROPE:]
    w_in_p = jnp.concatenate([w_in[:, :-MLA_ROPE], _pad_lanes(kr_w), _pad_lanes(_rot_cols(kr_w))], axis=1)
    uq3 = uq.reshape(MLA_RANK, HEADS, MLA_QK)
    nope, rope = uq3[..., :HEAD_DIM], uq3[..., HEAD_DIM:]
    uq_p = jnp.concatenate([nope.reshape(MLA_RANK, HALF), _pad_lanes(rope).reshape(MLA_RANK, HALF),
                            _pad_lanes(_rot_cols(rope)).reshape(MLA_RANK, HALF)], axis=1)
    ukv3 = ukv.reshape(MLA_RANK, HEADS, 2 * HEAD_DIM)
    ukv_p = jnp.concatenate([ukv3[..., :HEAD_DIM].reshape(MLA_RANK, HALF),
                             ukv3[..., HEAD_DIM:].reshape(MLA_RANK, HALF)], axis=1)
    del d
    return w_in_p.astype(BF16), uq_p.astype(BF16), ukv_p.astype(BF16)


def _rope_tables(seq, ctx_len):
    rows = seq // GRID_W
    row_id = jnp.repeat(jnp.arange(rows), GRID_W).astype(F32)
    col_id = (jnp.arange(seq) % GRID_W).astype(F32)

    def table(dim):
        quarter = dim // 4
        inv_freq = ROPE_THETA ** (-jnp.arange(quarter, dtype=F32) / quarter)
        ang = jnp.concatenate([row_id[:, None] * inv_freq, col_id[:, None] * inv_freq], axis=-1)
        cos, sin = jnp.cos(ang), jnp.sin(ang)
        cos2 = jnp.concatenate([cos, cos], axis=-1)
        cos2 = jnp.concatenate([cos2, jnp.ones((ctx_len, dim), F32)], axis=0)
        return cos2, cos, sin

    ca, _, sin_a = table(HEAD_DIM)
    sna = jnp.concatenate([jnp.concatenate([-sin_a, sin_a], axis=-1), jnp.zeros((ctx_len, HEAD_DIM), F32)], axis=0)
    cb, _, sin_b = table(MLA_ROPE)
    snb = jnp.concatenate([jnp.concatenate([sin_b, sin_b], axis=-1), jnp.zeros((ctx_len, MLA_ROPE), F32)], axis=0)
    return ca, sna, _pad_lanes(cb), _pad_lanes(snb)


def kernel(x, c, ctx, c_ctx, mod_w, mod_b, norm_mix_w, norm_ffn_w, mix_out_w, ffn_gate_w, ffn_up_w, ffn_down_w, attn_in_w, gqa_q_norm_w, gqa_k_norm_w, mla_q_norm_w, mla_uq_w, mla_kv_norm_w, mla_ukv_w, rec_in_w, lru_conv_w, lru_conv_b, lru_ra_w, lru_ra_b, lru_ix_w, lru_ix_b, lru_lambda, hgrn_lb_logits, hgrn_norm_w, final_norm_w):
    batch, seq, d = x.shape
    ctx_len = ctx.shape[1]
    depth = mod_w.shape[0]
    assert depth == 2 and batch < 8 and seq % ctx_len == 0 and ctx_len == HG_BLOCK
    nl = batch * seq
    rows_all = nl + batch * ctx_len

    cc = jnp.zeros((8, d), F32).at[:batch].set(c).at[batch].set(c_ctx)
    mods = _modulation(cc, mod_w, mod_b).reshape(depth * 8 * MOD_CHUNKS, 1, d)
    x_rows, ctx_rows = x.reshape(nl, d), ctx.reshape(batch * ctx_len, d)
    nm = functools.partial(_norm_mod_matmul, seq=seq, batch=batch)
    um = functools.partial(_u_matmul, seq=seq)
    mrn = functools.partial(_matmul_residual_norm, seq=seq, batch=batch)
    mix_w, down_w = mix_out_w.astype(BF16), ffn_down_w.astype(BF16)
    base1 = 8 * MOD_CHUNKS

    def ffn_hidden(u, rows, l):
        return um(u, rows, [ffn_gate_w, ffn_up_w], l, "swiglu", BF16, tn_prefs=(512, 256, 128), name=f"ffn_up{l}")

    w_in_p, uq_p, ukv_p = _attn_weights(attn_in_w[0], mla_uq_w[0], mla_ukv_w[0])
    y = nm([x_rows, ctx_rows], rows_all, norm_mix_w[0], mods, 0, [w_in_p[None]], 0, "plain", BF16,
           tn_prefs=(1408, 704, 256, 128), name="attn_in")
    qg, kg, vg, qm, km, vm = _attn_prep(
        y, gqa_q_norm_w[0].reshape(1, -1), gqa_k_norm_w[0].reshape(1, -1), mla_q_norm_w[0].reshape(1, -1),
        mla_kv_norm_w[0].reshape(1, -1), uq_p, ukv_p, _rope_tables(seq, ctx_len), batch, seq, ctx_len)
    og = _attention(qg, kg, vg, GQA_GROUP, 1, batch, seq, ctx_len, "gqa")
    om = _attention(qm, km, vm, 4, 4, batch, seq, ctx_len, "mla")
    h, u = mrn([og, om], [(mix_w, 0, 0), (mix_w, 0, 1)], [x_rows, ctx_rows], rows_all, mods, 2,
               norm_ffn_w[0], 3, tm_prefs=(512, 256), name="mix_out0")
    h, u = mrn([ffn_hidden(u, rows_all, 0)], [(down_w, 0, 0)], [h], rows_all, mods, 5,
               norm_mix_w[1], base1, tm_prefs=(256,), name="ffn_down0")

    y = um(u, rows_all, [rec_in_w], 0, "plain", F32, tn_prefs=(1024, 512, 256, 128), name="rec_in")
    w_cat = jnp.concatenate([lru_ra_w[0, 0], lru_ix_w[0, 0], lru_ra_w[0, 1], lru_ix_w[0, 1]], axis=-1).astype(BF16)
    b_cat = jnp.concatenate([lru_ra_b[0, 0], lru_ix_b[0, 0], lru_ra_b[0, 1], lru_ix_b[0, 1]], axis=-1)[:, None, :]
    y_lru = _lru(y, lru_conv_w[0], lru_conv_b[0].reshape(1, -1), w_cat, b_cat, lru_lambda[0], batch, seq, ctx_len)
    lb_all = jnp.cumsum(jax.nn.softmax(hgrn_lb_logits.astype(F32), axis=1), axis=1)
    lb = (lb_all - lb_all[:, :1])[:, 1].reshape(2, 1, HALF)
    o_fwd = _hgrn(y, lb, 0, batch, seq, ctx_len)
    y_hg = _hgrn(y, lb, 1, batch, seq, ctx_len, o_prev=o_fwd, norm_w=hgrn_norm_w[0])
    h, u = mrn([y_lru, y_hg], [(mix_w, 1, 0), (mix_w, 1, 1)], [h], nl, mods, base1 + 2,
               norm_ffn_w[1], base1 + 3, tm_prefs=(512, 256), name="mix_out1")
    out = mrn([ffn_hidden(u, nl, 1)], [(down_w, 1, 0)], [h], nl, mods, base1 + 5,
              final_norm_w, None, tm_prefs=(256,), name="ffn_down1")
    return out.reshape(batch, seq, d)
```

```python
import functools
import math

import numpy as np
import jax
import jax.numpy as jnp
from jax import lax
from jax.experimental import pallas as pl
from jax.experimental.pallas import tpu as pltpu

F32 = jnp.float32
BF16 = jnp.bfloat16

EPS = 1e-6
ROPE_THETA = 10000.0
GRID_W = 64
MOD_CHUNKS = 6
LANES = 128
HEADS = 8
HEAD_DIM = 128
GQA_KV_HEADS = 2
GQA_GROUP = HEADS // GQA_KV_HEADS
MLA_RANK = 512
MLA_ROPE = 64
MLA_QK = HEAD_DIM + MLA_ROPE
HALF = HEADS * HEAD_DIM
LRU_C = 8.0
HG_CHUNK = 64
HG_BLOCK = 256
HG_VREG_LEVEL = 8
VMEM_LIMIT = 56 * 1024 * 1024


def _pick(n, prefs):
    for p in prefs:
        if n % p == 0:
            return p
    raise ValueError(f"no tile for {n} in {prefs}")


def _cparams(sem):
    return pltpu.CompilerParams(dimension_semantics=sem, vmem_limit_bytes=VMEM_LIMIT)


def _dot(a, b):
    return jnp.dot(a, b, preferred_element_type=F32)


def _dot_nt(a, b):
    return lax.dot_general(a, b, (((1,), (1,)), ((), ())), preferred_element_type=F32)


def _dot_tn(a, b):
    return lax.dot_general(a, b, (((0,), (0,)), ((), ())), preferred_element_type=F32)


def _sigmoid(x):
    return 0.5 * jnp.tanh(0.5 * x) + 0.5


def _mod_kernel(c_ref, w_ref, b_ref, o_ref):
    c = c_ref[...]
    o_ref[0] = _dot(c * _sigmoid(c), w_ref[0]) + b_ref[0]


def _modulation(cc, mod_w, mod_b):
    depth, d, n = mod_w.shape
    tn = _pick(n, (1024, 512, 256, 128))
    return pl.pallas_call(
        _mod_kernel,
        grid=(depth, n // tn),
        in_specs=[pl.BlockSpec((8, d), lambda l, j: (0, 0)),
                  pl.BlockSpec((1, d, tn), lambda l, j: (l, 0, j)),
                  pl.BlockSpec((1, 1, tn), lambda l, j: (l, 0, j))],
        out_specs=pl.BlockSpec((1, 8, tn), lambda l, j: (l, 0, j)),
        out_shape=jax.ShapeDtypeStruct((depth, 8, n), F32),
        compiler_params=_cparams(("parallel", "parallel")),
        name="modulation",
    )(cc, mod_w, mod_b.reshape(depth, 1, n))


def _row_sources(h_parts, tm):
    n1 = h_parts[0].shape[0] // tm
    if len(h_parts) == 1:
        return n1, [lambda i: i]
    return n1, [lambda i: jnp.minimum(i, n1 - 1), lambda i: jnp.maximum(i - n1, 0)]


def _nm_kernel(n_h, n_w, n1, epilogue, tm, rc, *refs):
    h_refs = refs[:n_h]
    nw_ref, sh_ref, sc_ref = refs[n_h:n_h + 3]
    w_refs = refs[n_h + 3:n_h + 3 + n_w]
    o_ref, u_scr = refs[n_h + 3 + n_w:]
    i = pl.program_id(0)

    def prologue(h_ref):
        def chunk(r, carry):
            r0 = pl.multiple_of(r * rc, rc)
            x = h_ref[pl.ds(r0, rc), :]
            ms = jnp.mean(x * x, axis=-1, keepdims=True)
            y = x * lax.rsqrt(ms + EPS) * nw_ref[...]
            u_scr[pl.ds(r0, rc), :] = (y * (1.0 + sc_ref[0]) + sh_ref[0]).astype(BF16)
            return carry
        lax.fori_loop(0, tm // rc, chunk, 0, unroll=8)

    first = pl.program_id(1) == 0
    if n_h == 1:
        pl.when(first)(lambda: prologue(h_refs[0]))
    else:
        pl.when(first & (i < n1))(lambda: prologue(h_refs[0]))
        pl.when(first & (i >= n1))(lambda: prologue(h_refs[1]))

    u = u_scr[...]
    if epilogue == "plain":
        o_ref[...] = _dot(u, w_refs[0][0].astype(BF16)).astype(o_ref.dtype)
    else:
        g = _dot(u, w_refs[0][0].astype(BF16))
        up = _dot(u, w_refs[1][0].astype(BF16))
        o_ref[...] = (g * _sigmoid(g) * up).astype(o_ref.dtype)


def _norm_mod_matmul(h_parts, rows, norm_w, mods, mod_base, ws, layer, epilogue, out_dtype, seq, batch, tn_prefs, name):
    d = h_parts[0].shape[1]
    n = ws[0].shape[2]
    tm = _pick(math.gcd(seq, rows), (1024, 512, 256))
    tn = _pick(n, tn_prefs)
    rc = 32
    n1, rmaps = _row_sources(h_parts, tm)

    def mrow(i, k):
        return mod_base + jnp.minimum(i * tm // seq, batch) * MOD_CHUNKS + k

    in_specs = [pl.BlockSpec((tm, d), lambda i, j, m=m: (m(i), 0), pipeline_mode=(pl.Buffered(1) if k else None))
                for k, m in enumerate(rmaps)]
    in_specs += [pl.BlockSpec((1, d), lambda i, j: (0, 0)),
                 pl.BlockSpec((1, 1, d), lambda i, j: (mrow(i, 0), 0, 0)),
                 pl.BlockSpec((1, 1, d), lambda i, j: (mrow(i, 1), 0, 0))]
    in_specs += [pl.BlockSpec((1, d, tn), lambda i, j: (layer, 0, j)) for _ in ws]
    return pl.pallas_call(
        functools.partial(_nm_kernel, len(h_parts), len(ws), n1, epilogue, tm, rc),
        grid=(rows // tm, n // tn),
        in_specs=in_specs,
        out_specs=pl.BlockSpec((tm, tn), lambda i, j: (i, j)),
        out_shape=jax.ShapeDtypeStruct((rows, n), out_dtype),
        scratch_shapes=[pltpu.VMEM((tm, d), BF16)],
        compiler_params=_cparams(("parallel", "arbitrary")),
        name=name,
    )(*h_parts, norm_w.reshape(1, d), mods, mods, *ws)


def _um_kernel(n_w, epilogue, u_ref, *refs):
    w_refs, o_ref = refs[:n_w], refs[n_w]
    u = u_ref[...]
    if epilogue == "plain":
        o_ref[...] = _dot(u, w_refs[0][0].astype(BF16)).astype(o_ref.dtype)
    else:
        g = _dot(u, w_refs[0][0].astype(BF16))
        up = _dot(u, w_refs[1][0].astype(BF16))
        o_ref[...] = (g * _sigmoid(g) * up).astype(o_ref.dtype)


def _u_matmul(u, rows, ws, layer, epilogue, out_dtype, seq, tn_prefs, name):
    d = u.shape[1]
    n = ws[0].shape[2]
    tm = _pick(math.gcd(seq, rows), (1024, 512, 256))
    tn = _pick(n, tn_prefs)
    return pl.pallas_call(
        functools.partial(_um_kernel, len(ws), epilogue),
        grid=(rows // tm, n // tn),
        in_specs=[pl.BlockSpec((tm, d), lambda i, j: (i, 0))]
        + [pl.BlockSpec((1, d, tn), lambda i, j: (layer, 0, j)) for _ in ws],
        out_specs=pl.BlockSpec((tm, tn), lambda i, j: (i, j)),
        out_shape=jax.ShapeDtypeStruct((rows, n), out_dtype),
        compiler_params=_cparams(("parallel", "arbitrary")),
        name=name,
    )(u, *ws)


def _mrf_kernel(n_a, n_r, n1, mode, n_split, *refs):
    a_refs, w_refs = refs[:n_a], refs[n_a:2 * n_a]
    res_refs = refs[2 * n_a:2 * n_a + n_r]
    rest = refs[2 * n_a + n_r:]
    if mode == "next":
        g_ref, nw_ref, sh_ref, sc_ref, h_ref, u_ref = rest
    else:
        g_ref, nw_ref, o_ref = rest
    rs = a_refs[0].shape[0] // n_split
    parts = [slice(s * rs, (s + 1) * rs) for s in range(n_split)]
    accs = []
    for sl in parts:
        acc = _dot(a_refs[0][sl, :], w_refs[0][0])
        for a, w in zip(a_refs[1:], w_refs[1:]):
            acc += _dot(a[sl, :], w[0])
        accs.append(acc)
    for sl, acc in zip(parts, accs):
        if n_r == 1:
            res = res_refs[0][sl, :]
        else:
            res = jnp.where(pl.program_id(0) < n1, res_refs[0][sl, :], res_refs[1][sl, :])
        h = res + g_ref[0] * acc
        y = h * lax.rsqrt(jnp.mean(h * h, axis=-1, keepdims=True) + EPS) * nw_ref[...]
        if mode == "next":
            h_ref[sl, :] = h
            u_ref[sl, :] = (y * (1.0 + sc_ref[0]) + sh_ref[0]).astype(BF16)
        else:
            o_ref[sl, :] = y


def _matmul_residual_norm(a_list, w_specs, res_parts, rows, mods, gate_row0, norm_w, next_row0, seq, batch, tm_prefs, name,
                          n_split=1):
    d = res_parts[0].shape[1]
    tm = _pick(math.gcd(seq, rows), tm_prefs)
    n1, rmaps = _row_sources(res_parts, tm)
    mode = "final" if next_row0 is None else "next"

    def mrow(i, base):
        return base + jnp.minimum(i * tm // seq, batch) * MOD_CHUNKS

    row_tile = lambda w: pl.BlockSpec((tm, w), lambda i: (i, 0))
    vec = lambda base: pl.BlockSpec((1, 1, d), lambda i: (mrow(i, base), 0, 0))
    in_specs = [row_tile(a.shape[1]) for a in a_list]
    in_specs += [pl.BlockSpec((1, a.shape[1], d), lambda i, l=l, kb=kb: (l, kb, 0), pipeline_mode=pl.Buffered(1))
                 for a, (_, l, kb) in zip(a_list, w_specs)]
    in_specs += [pl.BlockSpec((tm, d), lambda i, m=m: (m(i), 0)) for m in rmaps]
    in_specs += [vec(gate_row0), pl.BlockSpec((1, d), lambda i: (0, 0))]
    args = [*a_list, *[w for w, _, _ in w_specs], *res_parts, mods, norm_w.reshape(1, d)]
    if mode == "next":
        in_specs += [vec(next_row0), vec(next_row0 + 1)]
        args += [mods, mods]
        out_specs = [row_tile(d), row_tile(d)]
        out_shape = [jax.ShapeDtypeStruct((rows, d), F32), jax.ShapeDtypeStruct((rows, d), BF16)]
    else:
        out_specs = row_tile(d)
        out_shape = jax.ShapeDtypeStruct((rows, d), F32)
    return pl.pallas_call(
        functools.partial(_mrf_kernel, len(a_list), len(res_parts), n1, mode, n_split),
        grid=(rows // tm,),
        in_specs=in_specs,
        out_specs=out_specs,
        out_shape=out_shape,
        compiler_params=_cparams(("parallel",)),
        name=name,
    )(*args)


def _prep_kernel(sa, sb, y_ref, qw_ref, kw_ref, mqw_ref, mkw_ref, uq_ref, ukv_ref,
                 ca_ref, sna_ref, cb_ref, snb_ref,
                 qg_ref, kg_ref, vg_ref, qm_ref, km_ref, vm_ref):
    hd = HEAD_DIM
    ca, sna, cb, snb = ca_ref[...], sna_ref[...], cb_ref[...], snb_ref[...]
    ycols = lambda start, width: y_ref[:, start:start + width].astype(F32)

    def head_norm_rope(x, w):
        r = lax.rsqrt(jnp.mean(x * x, axis=-1, keepdims=True) + EPS)
        yh = x * r * w
        return yh * ca + pltpu.roll(yh, hd // 2, axis=1) * sna

    o = 0
    for h in range(HEADS):
        qg_ref[0, h] = (head_norm_rope(ycols(o + h * hd, hd), qw_ref[...]) * sa).astype(BF16)
    o += HALF
    for h in range(GQA_KV_HEADS):
        kg_ref[0, h] = head_norm_rope(ycols(o + h * hd, hd), kw_ref[...]).astype(BF16)
    o += GQA_KV_HEADS * hd
    ones_col = (lax.broadcasted_iota(jnp.int32, (y_ref.shape[0], hd), 1) == 0).astype(BF16)
    for h in range(GQA_KV_HEADS):
        vg_ref[0, h, :, :hd] = y_ref[:, o + h * hd:o + (h + 1) * hd].astype(BF16)
        vg_ref[0, h, :, hd:] = ones_col
    o += GQA_KV_HEADS * hd

    def rms(x, w):
        return (x * lax.rsqrt(jnp.mean(x * x, axis=-1, keepdims=True) + EPS) * w).astype(BF16)

    qb = _dot(rms(ycols(o, MLA_RANK), mqw_ref[...]), uq_ref[...])
    o += MLA_RANK
    kvb = _dot(rms(ycols(o, MLA_RANK), mkw_ref[...]), ukv_ref[...])
    o += MLA_RANK
    kr = (ycols(o, hd) * cb + ycols(o + hd, hd) * snb).astype(BF16)
    for h in range(HEADS):
        lo, hi = h * hd, (h + 1) * hd
        qm_ref[0, h, :, :hd] = (qb[:, lo:hi] * sb).astype(BF16)
        qm_ref[0, h, :, hd:] = ((qb[:, HALF + lo:HALF + hi] * cb + qb[:, 2 * HALF + lo:2 * HALF + hi] * snb) * sb).astype(BF16)
        km_ref[0, h, :, :hd] = kvb[:, lo:hi].astype(BF16)
        km_ref[0, h, :, hd:] = kr
        vm_ref[0, h, :, :hd] = kvb[:, HALF + lo:HALF + hi].astype(BF16)
        vm_ref[0, h, :, hd:] = ones_col


def _attn_prep(y, qw, kw, mqw, mkw, uq, ukv, tabs, batch, seq, ctx_len):
    rows, n = y.shape
    ts = ctx_len
    nl = seq // ts
    n_lat = batch * nl
    t_all = seq + ctx_len
    hd = HEAD_DIM

    def bidx(t):
        return jnp.where(t < n_lat, t // nl, t - n_lat)

    def sidx(t):
        return jnp.where(t < n_lat, t % nl + 1, 0)

    def ridx(t):
        return jnp.where(t < n_lat, t % nl, nl)

    def hm(width, heads):
        return pl.BlockSpec((1, heads, ts, width), lambda t: (bidx(t), 0, sidx(t), 0))

    full = lambda a: pl.BlockSpec(a.shape, lambda t: (0,) * a.ndim)
    tab = pl.BlockSpec((ts, hd), lambda t: (ridx(t), 0))
    outs = [((batch, HEADS, t_all, hd), hm(hd, HEADS)),
            ((batch, GQA_KV_HEADS, t_all, hd), hm(hd, GQA_KV_HEADS)),
            ((batch, GQA_KV_HEADS, t_all, 2 * hd), hm(2 * hd, GQA_KV_HEADS)),
            ((batch, HEADS, t_all, 2 * hd), hm(2 * hd, HEADS)),
            ((batch, HEADS, t_all, 2 * hd), hm(2 * hd, HEADS)),
            ((batch, HEADS, t_all, 2 * hd), hm(2 * hd, HEADS))]
    return pl.pallas_call(
        functools.partial(_prep_kernel, hd ** -0.5, MLA_QK ** -0.5),
        grid=(rows // ts,),
        in_specs=[pl.BlockSpec((ts, n), lambda t: (t, 0)), full(qw), full(kw), full(mqw), full(mkw),
                  full(uq), full(ukv), tab, tab, tab, tab],
        out_specs=[s for _, s in outs],
        out_shape=[jax.ShapeDtypeStruct(sh, BF16) for sh, _ in outs],
        compiler_params=_cparams(("parallel",)),
        name="attn_prep",
    )(y, qw, kw, mqw, mkw, uq, ukv, *tabs)


def _attn_kernel(group, kv_group, ctx_len, q_ref, k_ref, v_ref, o_ref):
    dv = HEAD_DIM

    def run(t_k):
        kv = lambda g: g if kv_group > 1 else 0
        ss = [_dot_nt(q_ref[0, g], k_ref[0, kv(g), :t_k, :]) for g in range(group)]
        for g, s in enumerate(ss):
            p = jnp.exp(s - jnp.max(s, axis=-1, keepdims=True))
            ov = _dot(p.astype(BF16), v_ref[0, kv(g), :t_k, :])
            o_ref[:, g * dv:(g + 1) * dv] = (ov[:, :dv] / ov[:, dv:dv + 1]).astype(o_ref.dtype)

    is_ctx = pl.program_id(2) == 0

    @pl.when(is_ctx)
    def _():
        run(ctx_len)

    @pl.when(jnp.logical_not(is_ctx))
    def _():
        run(k_ref.shape[2])


def _attention(q, k, v, group, kv_group, batch, seq, ctx_len, name):
    _, hq, t_all, dq = q.shape
    dvp = v.shape[-1]
    tq = ctx_len
    nq = seq // tq
    rows = batch * t_all

    def orow(b, qi):
        return jnp.where(qi == 0, batch * nq + b, b * nq + qi - 1)

    return pl.pallas_call(
        functools.partial(_attn_kernel, group, kv_group, ctx_len),
        grid=(batch, hq // group, nq + 1),
        in_specs=[pl.BlockSpec((1, group, tq, dq), lambda b, h, qi: (b, h, qi, 0)),
                  pl.BlockSpec((1, kv_group, t_all, dq), lambda b, h, qi: (b, h, 0, 0)),
                  pl.BlockSpec((1, kv_group, t_all, dvp), lambda b, h, qi: (b, h, 0, 0))],
        out_specs=pl.BlockSpec((tq, group * HEAD_DIM), lambda b, h, qi: (orow(b, qi), h)),
        out_shape=jax.ShapeDtypeStruct((rows, hq * HEAD_DIM), BF16),
        compiler_params=_cparams(("parallel", "parallel", "arbitrary")),
        name=name,
    )(q, k, v)


def _gelu(x):
    return 0.5 * x * (1.0 + jnp.tanh(0.7978845608028654 * (x + 0.044715 * x * x * x)))


def _lru_kernel(seq, ctx_len, rc, xl_ref, xc_ref, gate_ref, cw_ref, cb_ref, w_ref, b_ref, lam_ref,
                o_ref, xp_scr, af_scr, bf_scr, ab_scr, bb_scr, hs_scr):
    cb_w = xl_ref.shape[1]
    nblk = cb_w // LANES
    lam = lam_ref[...]
    log_a_unit = -LRU_C * (jnp.maximum(-lam, 0.0) + jnp.log(1.0 + jnp.exp(-jnp.abs(lam))))
    cw = cw_ref[...]
    cbias = cb_ref[...]
    zeros8 = jnp.zeros((8, cb_w), F32)

    def coeffs(src_ref, n_rows, dst0):
        xp_scr[pl.ds(0, 8), :] = zeros8
        xp_scr[pl.ds(8 + n_rows, 8), :] = zeros8

        def cp(r, carry):
            r0 = pl.multiple_of(r * rc, rc)
            xp_scr[pl.ds(8 + r0, rc), :] = src_ref[pl.ds(r0, rc), :]
            return carry
        lax.fori_loop(0, n_rows // rc, cp, 0)

        def chunk(r, carry):
            r0 = pl.multiple_of(r * rc, rc)
            win = xp_scr[pl.ds(r0, rc + 16), :]
            xc = cbias + cw[2:3] * win[8:8 + rc]
            xc += cw[0:1] * pltpu.roll(win, 2, axis=0)[8:8 + rc]
            xc += cw[1:2] * pltpu.roll(win, 1, axis=0)[8:8 + rc]
            xc += cw[3:4] * pltpu.roll(win, rc + 15, axis=0)[8:8 + rc]
            for n in range(nblk):
                lo, hi = n * LANES, (n + 1) * LANES
                xcn = xc[:, lo:hi]
                z = _dot(xcn.astype(BF16), w_ref[n]) + b_ref[n]
                for d, (a_scr, b_scr) in enumerate(((af_scr, bf_scr), (ab_scr, bb_scr))):
                    r_g = _sigmoid(z[:, (2 * d) * LANES:(2 * d + 1) * LANES])
                    i_g = _sigmoid(z[:, (2 * d + 1) * LANES:(2 * d + 2) * LANES])
                    a = jnp.exp(r_g * log_a_unit[d:d + 1, lo:hi])
                    a_scr[pl.ds(dst0 + r0, rc), lo:hi] = a
                    om = 1.0 - a * a
                    mult = jnp.where(om > 0.0, om * lax.rsqrt(om), 0.0)
                    b_scr[pl.ds(dst0 + r0, rc), lo:hi] = mult * (i_g * xcn)
            return carry
        lax.fori_loop(0, n_rows // rc, chunk, 0)

    coeffs(xc_ref, ctx_len, 0)
    coeffs(xl_ref, seq, ctx_len)
    t_all = seq + ctx_len

    row = lax.broadcasted_iota(jnp.int32, (8, cb_w), 0)

    def group_scan(a, b, h_in, reverse):
        for sh in (1, 2, 4):
            keep = (row < 8 - sh) if reverse else (row >= sh)
            rot = (8 - sh) if reverse else sh
            a_s = jnp.where(keep, pltpu.roll(a, rot, axis=0), 1.0)
            b_s = jnp.where(keep, pltpu.roll(b, rot, axis=0), 0.0)
            b = a * b_s + b
            a = a * a_s
        h = a * h_in + b
        edge = 0 if reverse else 7
        return h, jnp.broadcast_to(h[edge:edge + 1], (8, cb_w))

    n_groups = t_all // 8
    n_ctx_groups = ctx_len // 8

    def step(g, carry):
        hf_in, hb_in = carry
        r0 = pl.multiple_of(g * 8, 8)
        hf, hf_in = group_scan(af_scr[pl.ds(r0, 8), :], bf_scr[pl.ds(r0, 8), :], hf_in, False)
        hs_scr[pl.ds(r0, 8), :] = hf
        gb = jnp.where(g < n_ctx_groups, n_ctx_groups - 1 - g, n_groups + n_ctx_groups - 1 - g)
        rb0 = pl.multiple_of(gb * 8, 8)
        hb, hb_in = group_scan(ab_scr[pl.ds(rb0, 8), :], bb_scr[pl.ds(rb0, 8), :], hb_in, True)
        bb_scr[pl.ds(rb0, 8), :] = hb
        return hf_in, hb_in
    h0 = jnp.zeros((8, cb_w), F32)
    lax.fori_loop(0, n_groups, step, (h0, h0), unroll=4)

    def fin(r, carry):
        r0 = pl.multiple_of(r * rc, rc)
        hsum = hs_scr[pl.ds(ctx_len + r0, rc), :] + bb_scr[pl.ds(ctx_len + r0, rc), :]
        o_ref[pl.ds(r0, rc), :] = (_gelu(gate_ref[pl.ds(r0, rc), :]) * hsum).astype(o_ref.dtype)
        return carry
    lax.fori_loop(0, seq // rc, fin, 0)


def _lru(y, conv_w, conv_b, w_cat, b_cat, lam, batch, seq, ctx_len):
    cb_w = 2 * LANES
    ncb = HALF // cb_w
    nl = batch * seq
    rc = 256
    t_all = seq + ctx_len
    scr = lambda r: pltpu.VMEM((r, cb_w), F32)
    return pl.pallas_call(
        functools.partial(_lru_kernel, seq, ctx_len, rc),
        grid=(batch, ncb),
        in_specs=[pl.BlockSpec((seq, cb_w), lambda b, c: (b, c)),
                  pl.BlockSpec((ctx_len, cb_w), lambda b, c: (nl // ctx_len + b, c)),
                  pl.BlockSpec((seq, cb_w), lambda b, c: (b, ncb + c)),
                  pl.BlockSpec((4, cb_w), lambda b, c: (0, c)),
                  pl.BlockSpec((1, cb_w), lambda b, c: (0, c)),
                  pl.BlockSpec((cb_w // LANES, LANES, 4 * LANES), lambda b, c: (c, 0, 0)),
                  pl.BlockSpec((cb_w // LANES, 1, 4 * LANES), lambda b, c: (c, 0, 0)),
                  pl.BlockSpec((2, cb_w), lambda b, c: (0, c))],
        out_specs=pl.BlockSpec((seq, cb_w), lambda b, c: (b, c)),
        out_shape=jax.ShapeDtypeStruct((nl, HALF), BF16),
        scratch_shapes=[scr(seq + 16), scr(t_all), scr(t_all), scr(t_all), scr(t_all), scr(t_all)],
        compiler_params=_cparams(("parallel", "parallel")),
        name="rglru",
    )(y, y, y, conv_w, conv_b, w_cat, b_cat, lam)


def _hg_tables():
    c = HG_CHUNK
    masks, pairs = [], []
    for d in range(2):
        r = np.arange(c) if d == 0 else c - 1 - np.arange(c)
        rt, rs = r[:, None], r[None, :]
        m = [rs <= rt]
        pm = []
        w = c // 2
        while w >= 1:
            bnd = w * (2 * (rt // (2 * w)) + 1)
            odd = (rt // w) % 2 == 1
            if w < HG_VREG_LEVEL:
                m.append(np.where(odd, (bnd <= rs) & (rs <= rt), (rt < rs) & (rs <= bnd - 1)))
            pm.append(odd & ((rs // w) % 2 == 0) & (rt // (2 * w) == rs // (2 * w)))
            w //= 2
        pm.append(rt == rs)
        m.append(np.ones((8, c), bool))
        masks.append(np.concatenate(m, axis=0))
        pairs.append(np.stack(pm))
    return np.stack(masks).astype(np.float32), np.stack(pairs).astype(np.float32)


def _hg_kernel(d, final, n_ctx_blk, n_levels, q_ref, f_ref, v_ref, lb_ref, mall_ref, pm_ref, *rest):
    if final:
        oprev_ref, g_ref, nw_ref, o_ref, s_scr = rest
    else:
        o_ref, s_scr = rest
    c, hd = HG_CHUNK, HEAD_DIM
    st = pl.program_id(1)
    n_chunks = HG_BLOCK // c

    @pl.when(st == 0)
    def _():
        s_scr[...] = jnp.zeros_like(s_scr)

    def make_chunk(with_out):
        def chunk(i, carry):
            ci = i if d == 0 else n_chunks - 1 - i
            r0 = pl.multiple_of(ci * c, c)
            mall = mall_ref[...]
            heads = range(HEADS)
            sl = [slice(h * hd, (h + 1) * hd) for h in heads]
            ks, cs2s = [], []
            for h in heads:
                lb = lb_ref[0, :, sl[h]]
                f = lb + (1.0 - lb) * _sigmoid(f_ref[pl.ds(r0, c), sl[h]])
                logf = jnp.log(f)
                lf_hi = logf.astype(BF16)
                lf_lo = (logf - lf_hi.astype(F32)).astype(BF16)
                ks.append(1.0 - f)
                cs2s.append(_dot(mall, jnp.concatenate([lf_hi, lf_lo], axis=1)))
            css = [cs2[:, :hd] + cs2[:, hd:] for cs2 in cs2s]
            es = [jnp.exp(cs) for cs in css]
            rows = lambda h, n: es[h][n * c:(n + 1) * c]
            n_wide = n_levels - (HG_VREG_LEVEL.bit_length() - 1)

            def level_decay(h, lv):
                if lv >= n_wide:
                    return rows(h, 1 + lv - n_wide)
                w = c >> (lv + 1)
                cum = css[h][:c]
                parts = [jnp.broadcast_to(cum[b:b + 1], (2 * w, hd))
                         for b in (a + w - 1 + d for a in range(0, c, 2 * w))]
                cum_b = parts[0] if len(parts) == 1 else jnp.concatenate(parts, axis=0)
                return jnp.exp(-jnp.abs(cum - cum_b))
            v16s = [v_ref[pl.ds(r0, c), sl[h]].astype(BF16) for h in heads]
            s_ts = [s_scr[h] for h in heads]
            if with_out:
                qs = []
                for h in heads:
                    qx = q_ref[pl.ds(r0, c), sl[h]]
                    qs.append(qx * _sigmoid(qx))
                a_s = []
                for h in heads:
                    a = pm_ref[n_levels] * _dot_nt(qs[h].astype(BF16), ks[h].astype(BF16))
                    for lv in range(n_levels):
                        el = level_decay(h, lv)
                        a += pm_ref[lv] * _dot_nt((qs[h] * el).astype(BF16), (ks[h] * el).astype(BF16))
                    a_s.append(a.astype(BF16))
                for h in heads:
                    o = _dot(a_s[h], v16s[h]) + _dot_nt((qs[h] * rows(h, 0)).astype(BF16), s_ts[h].astype(BF16))
                    if final:
                        o = o + oprev_ref[pl.ds(r0, c), sl[h]]
                        o = o * lax.rsqrt(jnp.mean(o * o, axis=-1, keepdims=True) + EPS) * nw_ref[:, sl[h]]
                        g = g_ref[pl.ds(r0, c), sl[h]]
                        o = o * (g * _sigmoid(g))
                    o_ref[pl.ds(r0, c), sl[h]] = o.astype(o_ref.dtype)
            base = (1 + n_levels - n_wide) * c
            for h in heads:
                after = jnp.exp(css[h][base:base + 1] - css[h][:c])
                s_scr[h] = es[h][base:base + 1] * s_ts[h] + _dot_tn(v16s[h], (ks[h] * after).astype(BF16))
            return carry
        return chunk

    @pl.when(st >= n_ctx_blk)
    def _():
        lax.fori_loop(0, n_chunks, make_chunk(True), 0, unroll=True)

    @pl.when(st < n_ctx_blk)
    def _():
        lax.fori_loop(0, n_chunks, make_chunk(False), 0)


def _hgrn(y, lb, d, batch, seq, ctx_len, o_prev=None, norm_w=None):
    rb = HG_BLOCK
    n_ctx_blk = ctx_len // rb
    n_lat_blk = seq // rb
    nl = batch * seq
    final = o_prev is not None
    mall_np, pm_np = _hg_tables()
    n_levels = pm_np.shape[1] - 1
    mall = jnp.asarray(mall_np[d], BF16)
    pm = jnp.asarray(pm_np[d], F32)
    q_col, f_col, v_col, g_col = 2, 3 + d, 5, 6

    def lat_blk(st):
        lc = jnp.maximum(st - n_ctx_blk, 0)
        return lc if d == 0 else n_lat_blk - 1 - lc

    def row_blk(b, st):
        cc = st if d == 0 else n_ctx_blk - 1 - st
        return jnp.where(st < n_ctx_blk, nl // rb + b * n_ctx_blk + cc, b * n_lat_blk + lat_blk(st))

    col = lambda cidx: pl.BlockSpec((rb, HALF), lambda b, st: (row_blk(b, st), cidx))
    lat = lambda cidx: pl.BlockSpec((rb, HALF), lambda b, st: (b * n_lat_blk + lat_blk(st), cidx))
    const = lambda a: pl.BlockSpec(a.shape, lambda b, st: (0,) * a.ndim)
    in_specs = [col(q_col), col(f_col), col(v_col),
                pl.BlockSpec((1, 1, HALF), lambda b, st: (d, 0, 0)), const(mall), const(pm)]
    args = [y, y, y, lb, mall, pm]
    if final:
        nw = norm_w.reshape(1, HALF)
        in_specs += [lat(0), lat(g_col), const(nw)]
        args += [o_prev, y, nw]
    return pl.pallas_call(
        functools.partial(_hg_kernel, d, final, n_ctx_blk, n_levels),
        grid=(batch, n_ctx_blk + n_lat_blk),
        in_specs=in_specs,
        out_specs=lat(0),
        out_shape=jax.ShapeDtypeStruct((nl, HALF), BF16 if final else F32),
        scratch_shapes=[pltpu.VMEM((HEADS, HEAD_DIM, HEAD_DIM), F32)],
        compiler_params=_cparams(("parallel", "arbitrary")),
        name=f"hgrn2_dir{d}",
    )(*args)


def _rot_cols(w):
    x1, x2 = jnp.split(w, 2, axis=-1)
    return jnp.concatenate([-x2, x1], axis=-1)


def _pad_lanes(w):
    return jnp.pad(w, [(0, 0)] * (w.ndim - 1) + [(0, LANES - w.shape[-1])])


def _attn_weights(w_in, uq, ukv):
    d = w_in.shape[0]
    kr_w = w_in[:, -MLA_ROPE:]
    w_in_p = jnp.concatenate([w_in[:, :-MLA_ROPE], _pad_lanes(kr_w), _pad_lanes(_rot_cols(kr_w))], axis=1)
    uq3 = uq.reshape(MLA_RANK, HEADS, MLA_QK)
    nope, rope = uq3[..., :HEAD_DIM], uq3[..., HEAD_DIM:]
    uq_p = jnp.concatenate([nope.reshape(MLA_RANK, HALF), _pad_lanes(rope).reshape(MLA_RANK, HALF),
                            _pad_lanes(_rot_cols(rope)).reshape(MLA_RANK, HALF)], axis=1)
    ukv3 = ukv.reshape(MLA_RANK, HEADS, 2 * HEAD_DIM)
    ukv_p = jnp.concatenate([ukv3[..., :HEAD_DIM].reshape(MLA_RANK, HALF),
                             ukv3[..., HEAD_DIM:].reshape(MLA_RANK, HALF)], axis=1)
    del d
    return w_in_p.astype(BF16), uq_p.astype(BF16), ukv_p.astype(BF16)


def _rope_tables(seq, ctx_len):
    rows = seq // GRID_W
    row_id = jnp.repeat(jnp.arange(rows), GRID_W).astype(F32)
    col_id = (jnp.arange(seq) % GRID_W).astype(F32)

    def table(dim):
        quarter = dim // 4
        inv_freq = ROPE_THETA ** (-jnp.arange(quarter, dtype=F32) / quarter)
        ang = jnp.concatenate([row_id[:, None] * inv_freq, col_id[:, None] * inv_freq], axis=-1)
        cos, sin = jnp.cos(ang), jnp.sin(ang)
        cos2 = jnp.concatenate([cos, cos], axis=-1)
        cos2 = jnp.concatenate([cos2, jnp.ones((ctx_len, dim), F32)], axis=0)
        return cos2, cos, sin

    ca, _, sin_a = table(HEAD_DIM)
    sna = jnp.concatenate([jnp.concatenate([-sin_a, sin_a], axis=-1), jnp.zeros((ctx_len, HEAD_DIM), F32)], axis=0)
    cb, _, sin_b = table(MLA_ROPE)
    snb = jnp.concatenate([jnp.concatenate([sin_b, sin_b], axis=-1), jnp.zeros((ctx_len, MLA_ROPE), F32)], axis=0)
    return ca, sna, _pad_lanes(cb), _pad_lanes(snb)


def kernel(x, c, ctx, c_ctx, mod_w, mod_b, norm_mix_w, norm_ffn_w, mix_out_w, ffn_gate_w, ffn_up_w, ffn_down_w, attn_in_w, gqa_q_norm_w, gqa_k_norm_w, mla_q_norm_w, mla_uq_w, mla_kv_norm_w, mla_ukv_w, rec_in_w, lru_conv_w, lru_conv_b, lru_ra_w, lru_ra_b, lru_ix_w, lru_ix_b, lru_lambda, hgrn_lb_logits, hgrn_norm_w, final_norm_w):
    batch, seq, d = x.shape
    ctx_len = ctx.shape[1]
    depth = mod_w.shape[0]
    assert depth == 2 and batch < 8 and seq % ctx_len == 0 and ctx_len == HG_BLOCK
    nl = batch * seq
    rows_all = nl + batch * ctx_len

    cc = jnp.zeros((8, d), F32).at[:batch].set(c).at[batch].set(c_ctx)
    mods = _modulation(cc, mod_w, mod_b).reshape(depth * 8 * MOD_CHUNKS, 1, d)
    x_rows, ctx_rows = x.reshape(nl, d), ctx.reshape(batch * ctx_len, d)
    nm = functools.partial(_norm_mod_matmul, seq=seq, batch=batch)
    um = functools.partial(_u_matmul, seq=seq)
    mrn = functools.partial(_matmul_residual_norm, seq=seq, batch=batch)
    mix_w, down_w = mix_out_w.astype(BF16), ffn_down_w.astype(BF16)
    base1 = 8 * MOD_CHUNKS

    def ffn_hidden(u, rows, l):
        return um(u, rows, [ffn_gate_w, ffn_up_w], l, "swiglu", BF16, tn_prefs=(512, 256, 128), name=f"ffn_up{l}")

    w_in_p, uq_p, ukv_p = _attn_weights(attn_in_w[0], mla_uq_w[0], mla_ukv_w[0])
    y = nm([x_rows, ctx_rows], rows_all, norm_mix_w[0], mods, 0, [w_in_p[None]], 0, "plain", BF16,
           tn_prefs=(1408, 704, 256, 128), name="attn_in")
    qg, kg, vg, qm, km, vm = _attn_prep(
        y, gqa_q_norm_w[0].reshape(1, -1), gqa_k_norm_w[0].reshape(1, -1), mla_q_norm_w[0].reshape(1, -1),
        mla_kv_norm_w[0].reshape(1, -1), uq_p, ukv_p, _rope_tables(seq, ctx_len), batch, seq, ctx_len)
    og = _attention(qg, kg, vg, GQA_GROUP, 1, batch, seq, ctx_len, "gqa")
    om = _attention(qm, km, vm, 4, 4, batch, seq, ctx_len, "mla")
    h, u = mrn([og, om], [(mix_w, 0, 0), (mix_w, 0, 1)], [x_rows, ctx_rows], rows_all, mods, 2,
               norm_ffn_w[0], 3, tm_prefs=(512, 256), name="mix_out0", n_split=4)
    h, u = mrn([ffn_hidden(u, rows_all, 0)], [(down_w, 0, 0)], [h], rows_all, mods, 5,
               norm_mix_w[1], base1, tm_prefs=(256,), name="ffn_down0")

    y = um(u, rows_all, [rec_in_w], 0, "plain", F32, tn_prefs=(1024, 512, 256, 128), name="rec_in")
    w_cat = jnp.concatenate([lru_ra_w[0, 0], lru_ix_w[0, 0], lru_ra_w[0, 1], lru_ix_w[0, 1]], axis=-1).astype(BF16)
    b_cat = jnp.concatenate([lru_ra_b[0, 0], lru_ix_b[0, 0], lru_ra_b[0, 1], lru_ix_b[0, 1]], axis=-1)[:, None, :]
    y_lru = _lru(y, lru_conv_w[0], lru_conv_b[0].reshape(1, -1), w_cat, b_cat, lru_lambda[0], batch, seq, ctx_len)
    lb_all = jnp.cumsum(jax.nn.softmax(hgrn_lb_logits.astype(F32), axis=1), axis=1)
    lb = (lb_all - lb_all[:, :1])[:, 1].reshape(2, 1, HALF)
    o_fwd = _hgrn(y, lb, 0, batch, seq, ctx_len)
    y_hg = _hgrn(y, lb, 1, batch, seq, ctx_len, o_prev=o_fwd, norm_w=hgrn_norm_w[0])
    h, u = mrn([y_lru, y_hg], [(mix_w, 1, 0), (mix_w, 1, 1)], [h], nl, mods, base1 + 2,
               norm_ffn_w[1], base1 + 3, tm_prefs=(512, 256), name="mix_out1", n_split=4)
    out = mrn([ffn_hidden(u, nl, 1)], [(down_w, 1, 0)], [h], nl, mods, base1 + 5,
              final_norm_w, None, tm_prefs=(256,), name="ffn_down1")
    return out.reshape(batch, seq, d)
```

```python
import functools
import math

import numpy as np
import jax
import jax.numpy as jnp
from jax import lax
from jax.experimental import pallas as pl
from jax.experimental.pallas import tpu as pltpu

F32 = jnp.float32
BF16 = jnp.bfloat16

EPS = 1e-6
ROPE_THETA = 10000.0
GRID_W = 64
MOD_CHUNKS = 6
LANES = 128
HEADS = 8
HEAD_DIM = 128
GQA_KV_HEADS = 2
GQA_GROUP = HEADS // GQA_KV_HEADS
MLA_RANK = 512
MLA_ROPE = 64
MLA_QK = HEAD_DIM + MLA_ROPE
HALF = HEADS * HEAD_DIM
LRU_C = 8.0
HG_CHUNK = 64
HG_BLOCK = 256
HG_VREG_LEVEL = 8
VMEM_LIMIT = 56 * 1024 * 1024


def _pick(n, prefs):
    for p in prefs:
        if n % p == 0:
            return p
    raise ValueError(f"no tile for {n} in {prefs}")


def _cparams(sem):
    return pltpu.CompilerParams(dimension_semantics=sem, vmem_limit_bytes=VMEM_LIMIT)


def _dot(a, b):
    return jnp.dot(a, b, preferred_element_type=F32)


def _dot_nt(a, b):
    return lax.dot_general(a, b, (((1,), (1,)), ((), ())), preferred_element_type=F32)


def _dot_tn(a, b):
    return lax.dot_general(a, b, (((0,), (0,)), ((), ())), preferred_element_type=F32)


def _sigmoid(x):
    return 0.5 * jnp.tanh(0.5 * x) + 0.5


def _mod_kernel(c_ref, w_ref, b_ref, o_ref):
    c = c_ref[...]
    o_ref[0] = _dot(c * _sigmoid(c), w_ref[0]) + b_ref[0]


def _modulation(cc, mod_w, mod_b):
    depth, d, n = mod_w.shape
    tn = _pick(n, (1024, 512, 256, 128))
    return pl.pallas_call(
        _mod_kernel,
        grid=(depth, n // tn),
        in_specs=[pl.BlockSpec((8, d), lambda l, j: (0, 0)),
                  pl.BlockSpec((1, d, tn), lambda l, j: (l, 0, j)),
                  pl.BlockSpec((1, 1, tn), lambda l, j: (l, 0, j))],
        out_specs=pl.BlockSpec((1, 8, tn), lambda l, j: (l, 0, j)),
        out_shape=jax.ShapeDtypeStruct((depth, 8, n), F32),
        compiler_params=_cparams(("parallel", "parallel")),
        name="modulation",
    )(cc, mod_w, mod_b.reshape(depth, 1, n))


def _row_sources(h_parts, tm):
    n1 = h_parts[0].shape[0] // tm
    if len(h_parts) == 1:
        return n1, [lambda i: i]
    return n1, [lambda i: jnp.minimum(i, n1 - 1), lambda i: jnp.maximum(i - n1, 0)]


def _nm_kernel(n_h, n_w, n1, epilogue, tm, rc, *refs):
    h_refs = refs[:n_h]
    nw_ref, sh_ref, sc_ref = refs[n_h:n_h + 3]
    w_refs = refs[n_h + 3:n_h + 3 + n_w]
    o_ref, u_scr = refs[n_h + 3 + n_w:]
    i = pl.program_id(0)

    def prologue(h_ref):
        def chunk(r, carry):
            r0 = pl.multiple_of(r * rc, rc)
            x = h_ref[pl.ds(r0, rc), :]
            ms = jnp.mean(x * x, axis=-1, keepdims=True)
            y = x * lax.rsqrt(ms + EPS) * nw_ref[...]
            u_scr[pl.ds(r0, rc), :] = (y * (1.0 + sc_ref[0]) + sh_ref[0]).astype(BF16)
            return carry
        lax.fori_loop(0, tm // rc, chunk, 0, unroll=8)

    first = pl.program_id(1) == 0
    if n_h == 1:
        pl.when(first)(lambda: prologue(h_refs[0]))
    else:
        pl.when(first & (i < n1))(lambda: prologue(h_refs[0]))
        pl.when(first & (i >= n1))(lambda: prologue(h_refs[1]))

    u = u_scr[...]
    if epilogue == "plain":
        o_ref[...] = _dot(u, w_refs[0][0].astype(BF16)).astype(o_ref.dtype)
    else:
        g = _dot(u, w_refs[0][0].astype(BF16))
        up = _dot(u, w_refs[1][0].astype(BF16))
        o_ref[...] = (g * _sigmoid(g) * up).astype(o_ref.dtype)


def _norm_mod_matmul(h_parts, rows, norm_w, mods, mod_base, ws, layer, epilogue, out_dtype, seq, batch, tn_prefs, name):
    d = h_parts[0].shape[1]
    n = ws[0].shape[2]
    tm = _pick(math.gcd(seq, rows), (1024, 512, 256))
    tn = _pick(n, tn_prefs)
    rc = 32
    n1, rmaps = _row_sources(h_parts, tm)

    def mrow(i, k):
        return mod_base + jnp.minimum(i * tm // seq, batch) * MOD_CHUNKS + k

    in_specs = [pl.BlockSpec((tm, d), lambda i, j, m=m: (m(i), 0), pipeline_mode=(pl.Buffered(1) if k else None))
                for k, m in enumerate(rmaps)]
    in_specs += [pl.BlockSpec((1, d), lambda i, j: (0, 0)),
                 pl.BlockSpec((1, 1, d), lambda i, j: (mrow(i, 0), 0, 0)),
                 pl.BlockSpec((1, 1, d), lambda i, j: (mrow(i, 1), 0, 0))]
    in_specs += [pl.BlockSpec((1, d, tn), lambda i, j: (layer, 0, j)) for _ in ws]
    return pl.pallas_call(
        functools.partial(_nm_kernel, len(h_parts), len(ws), n1, epilogue, tm, rc),
        grid=(rows // tm, n // tn),
        in_specs=in_specs,
        out_specs=pl.BlockSpec((tm, tn), lambda i, j: (i, j)),
        out_shape=jax.ShapeDtypeStruct((rows, n), out_dtype),
        scratch_shapes=[pltpu.VMEM((tm, d), BF16)],
        compiler_params=_cparams(("parallel", "arbitrary")),
        name=name,
    )(*h_parts, norm_w.reshape(1, d), mods, mods, *ws)


def _um_kernel(n_w, epilogue, u_ref, *refs):
    w_refs, o_ref = refs[:n_w], refs[n_w]
    u = u_ref[...]
    if epilogue == "plain":
        o_ref[...] = _dot(u, w_refs[0][0].astype(BF16)).astype(o_ref.dtype)
    else:
        g = _dot(u, w_refs[0][0].astype(BF16))
        up = _dot(u, w_refs[1][0].astype(BF16))
        o_ref[...] = (g * _sigmoid(g) * up).astype(o_ref.dtype)


def _u_matmul(u, rows, ws, layer, epilogue, out_dtype, seq, tn_prefs, name):
    d = u.shape[1]
    n = ws[0].shape[2]
    tm = _pick(math.gcd(seq, rows), (1024, 512, 256))
    tn = _pick(n, tn_prefs)
    return pl.pallas_call(
        functools.partial(_um_kernel, len(ws), epilogue),
        grid=(rows // tm, n // tn),
        in_specs=[pl.BlockSpec((tm, d), lambda i, j: (i, 0))]
        + [pl.BlockSpec((1, d, tn), lambda i, j: (layer, 0, j)) for _ in ws],
        out_specs=pl.BlockSpec((tm, tn), lambda i, j: (i, j)),
        out_shape=jax.ShapeDtypeStruct((rows, n), out_dtype),
        compiler_params=_cparams(("parallel", "arbitrary")),
        name=name,
    )(u, *ws)


def _mrf_kernel(n_a, n_r, n1, mode, n_split, *refs):
    a_refs, w_refs = refs[:n_a], refs[n_a:2 * n_a]
    res_refs = refs[2 * n_a:2 * n_a + n_r]
    rest = refs[2 * n_a + n_r:]
    if mode == "next":
        g_ref, nw_ref, sh_ref, sc_ref, h_ref, u_ref = rest
    else:
        g_ref, nw_ref, o_ref = rest
    rs = a_refs[0].shape[0] // n_split
    parts = [slice(s * rs, (s + 1) * rs) for s in range(n_split)]
    accs = []
    for sl in parts:
        acc = _dot(a_refs[0][sl, :], w_refs[0][0])
        for a, w in zip(a_refs[1:], w_refs[1:]):
            acc += _dot(a[sl, :], w[0])
        accs.append(acc)
    for sl, acc in zip(parts, accs):
        if n_r == 1:
            res = res_refs[0][sl, :]
        else:
            res = jnp.where(pl.program_id(0) < n1, res_refs[0][sl, :], res_refs[1][sl, :])
        h = res + g_ref[0] * acc
        y = h * lax.rsqrt(jnp.mean(h * h, axis=-1, keepdims=True) + EPS) * nw_ref[...]
        if mode == "next":
            h_ref[sl, :] = h
            u_ref[sl, :] = (y * (1.0 + sc_ref[0]) + sh_ref[0]).astype(BF16)
        else:
            o_ref[sl, :] = y


def _matmul_residual_norm(a_list, w_specs, res_parts, rows, mods, gate_row0, norm_w, next_row0, seq, batch, tm_prefs, name,
                          n_split=1):
    d = res_parts[0].shape[1]
    tm = _pick(math.gcd(seq, rows), tm_prefs)
    n1, rmaps = _row_sources(res_parts, tm)
    mode = "final" if next_row0 is None else "next"

    def mrow(i, base):
        return base + jnp.minimum(i * tm // seq, batch) * MOD_CHUNKS

    row_tile = lambda w: pl.BlockSpec((tm, w), lambda i: (i, 0))
    vec = lambda base: pl.BlockSpec((1, 1, d), lambda i: (mrow(i, base), 0, 0))
    in_specs = [row_tile(a.shape[1]) for a in a_list]
    in_specs += [pl.BlockSpec((1, a.shape[1], d), lambda i, l=l, kb=kb: (l, kb, 0), pipeline_mode=pl.Buffered(1))
                 for a, (_, l, kb) in zip(a_list, w_specs)]
    in_specs += [pl.BlockSpec((tm, d), lambda i, m=m: (m(i), 0)) for m in rmaps]
    in_specs += [vec(gate_row0), pl.BlockSpec((1, d), lambda i: (0, 0))]
    args = [*a_list, *[w for w, _, _ in w_specs], *res_parts, mods, norm_w.reshape(1, d)]
    if mode == "next":
        in_specs += [vec(next_row0), vec(next_row0 + 1)]
        args += [mods, mods]
        out_specs = [row_tile(d), row_tile(d)]
        out_shape = [jax.ShapeDtypeStruct((rows, d), F32), jax.ShapeDtypeStruct((rows, d), BF16)]
    else:
        out_specs = row_tile(d)
        out_shape = jax.ShapeDtypeStruct((rows, d), F32)
    return pl.pallas_call(
        functools.partial(_mrf_kernel, len(a_list), len(res_parts), n1, mode, n_split),
        grid=(rows // tm,),
        in_specs=in_specs,
        out_specs=out_specs,
        out_shape=out_shape,
        compiler_params=_cparams(("parallel",)),
        name=name,
    )(*args)


def _prep_kernel(sa, sb, y_ref, qw_ref, kw_ref, mqw_ref, mkw_ref, uq_ref, ukv_ref,
                 ca_ref, sna_ref, cb_ref, snb_ref,
                 qg_ref, kg_ref, vg_ref, qm_ref, km_ref, vm_ref):
    hd = HEAD_DIM
    ca, sna, cb, snb = ca_ref[...], sna_ref[...], cb_ref[...], snb_ref[...]
    ycols = lambda start, width: y_ref[:, start:start + width].astype(F32)

    def head_norm_rope(x, w):
        r = lax.rsqrt(jnp.mean(x * x, axis=-1, keepdims=True) + EPS)
        yh = x * r * w
        return yh * ca + pltpu.roll(yh, hd // 2, axis=1) * sna

    o = 0
    for h in range(HEADS):
        qg_ref[0, h] = (head_norm_rope(ycols(o + h * hd, hd), qw_ref[...]) * sa).astype(BF16)
    o += HALF
    for h in range(GQA_KV_HEADS):
        kg_ref[0, h] = head_norm_rope(ycols(o + h * hd, hd), kw_ref[...]).astype(BF16)
    o += GQA_KV_HEADS * hd
    ones_col = (lax.broadcasted_iota(jnp.int32, (y_ref.shape[0], hd), 1) == 0).astype(BF16)
    for h in range(GQA_KV_HEADS):
        vg_ref[0, h, :, :hd] = y_ref[:, o + h * hd:o + (h + 1) * hd].astype(BF16)
        vg_ref[0, h, :, hd:] = ones_col
    o += GQA_KV_HEADS * hd

    def rms(x, w):
        return (x * lax.rsqrt(jnp.mean(x * x, axis=-1, keepdims=True) + EPS) * w).astype(BF16)

    qb = _dot(rms(ycols(o, MLA_RANK), mqw_ref[...]), uq_ref[...])
    o += MLA_RANK
    kvb = _dot(rms(ycols(o, MLA_RANK), mkw_ref[...]), ukv_ref[...])
    o += MLA_RANK
    kr = (ycols(o, hd) * cb + ycols(o + hd, hd) * snb).astype(BF16)
    for h in range(HEADS):
        lo, hi = h * hd, (h + 1) * hd
        qm_ref[0, h, :, :hd] = (qb[:, lo:hi] * sb).astype(BF16)
        qm_ref[0, h, :, hd:] = ((qb[:, HALF + lo:HALF + hi] * cb + qb[:, 2 * HALF + lo:2 * HALF + hi] * snb) * sb).astype(BF16)
        km_ref[0, h, :, :hd] = kvb[:, lo:hi].astype(BF16)
        km_ref[0, h, :, hd:] = kr
        vm_ref[0, h, :, :hd] = kvb[:, HALF + lo:HALF + hi].astype(BF16)
        vm_ref[0, h, :, hd:] = ones_col


def _attn_prep(y, qw, kw, mqw, mkw, uq, ukv, tabs, batch, seq, ctx_len):
    rows, n = y.shape
    ts = ctx_len
    nl = seq // ts
    n_lat = batch * nl
    t_all = seq + ctx_len
    hd = HEAD_DIM

    def bidx(t):
        return jnp.where(t < n_lat, t // nl, t - n_lat)

    def sidx(t):
        return jnp.where(t < n_lat, t % nl + 1, 0)

    def ridx(t):
        return jnp.where(t < n_lat, t % nl, nl)

    def hm(width, heads):
        return pl.BlockSpec((1, heads, ts, width), lambda t: (bidx(t), 0, sidx(t), 0))

    full = lambda a: pl.BlockSpec(a.shape, lambda t: (0,) * a.ndim)
    tab = pl.BlockSpec((ts, hd), lambda t: (ridx(t), 0))
    outs = [((batch, HEADS, t_all, hd), hm(hd, HEADS)),
            ((batch, GQA_KV_HEADS, t_all, hd), hm(hd, GQA_KV_HEADS)),
            ((batch, GQA_KV_HEADS, t_all, 2 * hd), hm(2 * hd, GQA_KV_HEADS)),
            ((batch, HEADS, t_all, 2 * hd), hm(2 * hd, HEADS)),
            ((batch, HEADS, t_all, 2 * hd), hm(2 * hd, HEADS)),
            ((batch, HEADS, t_all, 2 * hd), hm(2 * hd, HEADS))]
    return pl.pallas_call(
        functools.partial(_prep_kernel, math.log2(math.e) * hd ** -0.5, math.log2(math.e) * MLA_QK ** -0.5),
        grid=(rows // ts,),
        in_specs=[pl.BlockSpec((ts, n), lambda t: (t, 0)), full(qw), full(kw), full(mqw), full(mkw),
                  full(uq), full(ukv), tab, tab, tab, tab],
        out_specs=[s for _, s in outs],
        out_shape=[jax.ShapeDtypeStruct(sh, BF16) for sh, _ in outs],
        compiler_params=_cparams(("parallel",)),
        name="attn_prep",
    )(y, qw, kw, mqw, mkw, uq, ukv, *tabs)


def _attn_kernel(group, kv_group, ctx_len, q_ref, k_ref, v_ref, o_ref):
    dv = HEAD_DIM

    def run(t_k):
        kv = lambda g: g if kv_group > 1 else 0
        ss = [_dot_nt(q_ref[0, g], k_ref[0, kv(g), :t_k, :]) for g in range(group)]
        for g, s in enumerate(ss):
            p = jnp.exp2(s - jnp.max(s, axis=-1, keepdims=True))
            ov = _dot(p.astype(BF16), v_ref[0, kv(g), :t_k, :])
            o_ref[:, g * dv:(g + 1) * dv] = (ov[:, :dv] / ov[:, dv:dv + 1]).astype(o_ref.dtype)

    is_ctx = pl.program_id(2) == 0

    @pl.when(is_ctx)
    def _():
        run(ctx_len)

    @pl.when(jnp.logical_not(is_ctx))
    def _():
        run(k_ref.shape[2])


def _attention(q, k, v, group, kv_group, batch, seq, ctx_len, name):
    _, hq, t_all, dq = q.shape
    dvp = v.shape[-1]
    tq = ctx_len
    nq = seq // tq
    rows = batch * t_all

    def orow(b, qi):
        return jnp.where(qi == 0, batch * nq + b, b * nq + qi - 1)

    return pl.pallas_call(
        functools.partial(_attn_kernel, group, kv_group, ctx_len),
        grid=(batch, hq // group, nq + 1),
        in_specs=[pl.BlockSpec((1, group, tq, dq), lambda b, h, qi: (b, h, qi, 0)),
                  pl.BlockSpec((1, kv_group, t_all, dq), lambda b, h, qi: (b, h, 0, 0)),
                  pl.BlockSpec((1, kv_group, t_all, dvp), lambda b, h, qi: (b, h, 0, 0))],
        out_specs=pl.BlockSpec((tq, group * HEAD_DIM), lambda b, h, qi: (orow(b, qi), h)),
        out_shape=jax.ShapeDtypeStruct((rows, hq * HEAD_DIM), BF16),
        compiler_params=_cparams(("parallel", "parallel", "arbitrary")),
        name=name,
    )(q, k, v)


def _gelu(x):
    return 0.5 * x * (1.0 + jnp.tanh(0.7978845608028654 * (x + 0.044715 * x * x * x)))


def _lru_kernel(seq, ctx_len, rc, xl_ref, xc_ref, gate_ref, cw_ref, cb_ref, w_ref, b_ref, lam_ref,
                o_ref, xp_scr, af_scr, bf_scr, ab_scr, bb_scr, hs_scr):
    cb_w = xl_ref.shape[1]
    nblk = cb_w // LANES
    lam = lam_ref[...]
    log_a_unit = -LRU_C * (jnp.maximum(-lam, 0.0) + jnp.log(1.0 + jnp.exp(-jnp.abs(lam))))
    cw = cw_ref[...]
    cbias = cb_ref[...]
    zeros8 = jnp.zeros((8, cb_w), F32)

    def coeffs(src_ref, n_rows, dst0):
        xp_scr[pl.ds(0, 8), :] = zeros8
        xp_scr[pl.ds(8 + n_rows, 8), :] = zeros8

        def cp(r, carry):
            r0 = pl.multiple_of(r * rc, rc)
            xp_scr[pl.ds(8 + r0, rc), :] = src_ref[pl.ds(r0, rc), :]
            return carry
        lax.fori_loop(0, n_rows // rc, cp, 0)

        def chunk(r, carry):
            r0 = pl.multiple_of(r * rc, rc)
            win = xp_scr[pl.ds(r0, rc + 16), :]
            xc = cbias + cw[2:3] * win[8:8 + rc]
            xc += cw[0:1] * pltpu.roll(win, 2, axis=0)[8:8 + rc]
            xc += cw[1:2] * pltpu.roll(win, 1, axis=0)[8:8 + rc]
            xc += cw[3:4] * pltpu.roll(win, rc + 15, axis=0)[8:8 + rc]
            for n in range(nblk):
                lo, hi = n * LANES, (n + 1) * LANES
                xcn = xc[:, lo:hi]
                z = _dot(xcn.astype(BF16), w_ref[n]) + b_ref[n]
                for d, (a_scr, b_scr) in enumerate(((af_scr, bf_scr), (ab_scr, bb_scr))):
                    r_g = _sigmoid(z[:, (2 * d) * LANES:(2 * d + 1) * LANES])
                    i_g = _sigmoid(z[:, (2 * d + 1) * LANES:(2 * d + 2) * LANES])
                    a = jnp.exp(r_g * log_a_unit[d:d + 1, lo:hi])
                    a_scr[pl.ds(dst0 + r0, rc), lo:hi] = a
                    om = 1.0 - a * a
                    mult = jnp.where(om > 0.0, om * lax.rsqrt(om), 0.0)
                    b_scr[pl.ds(dst0 + r0, rc), lo:hi] = mult * (i_g * xcn)
            return carry
        lax.fori_loop(0, n_rows // rc, chunk, 0)

    coeffs(xc_ref, ctx_len, 0)
    coeffs(xl_ref, seq, ctx_len)
    t_all = seq + ctx_len

    row = lax.broadcasted_iota(jnp.int32, (8, cb_w), 0)

    def group_scan(a, b, h_in, reverse):
        for sh in (1, 2, 4):
            keep = (row < 8 - sh) if reverse else (row >= sh)
            rot = (8 - sh) if reverse else sh
            a_s = jnp.where(keep, pltpu.roll(a, rot, axis=0), 1.0)
            b_s = jnp.where(keep, pltpu.roll(b, rot, axis=0), 0.0)
            b = a * b_s + b
            a = a * a_s
        h = a * h_in + b
        edge = 0 if reverse else 7
        return h, jnp.broadcast_to(h[edge:edge + 1], (8, cb_w))

    n_groups = t_all // 8
    n_ctx_groups = ctx_len // 8

    def step(g, carry):
        hf_in, hb_in = carry
        r0 = pl.multiple_of(g * 8, 8)
        hf, hf_in = group_scan(af_scr[pl.ds(r0, 8), :], bf_scr[pl.ds(r0, 8), :], hf_in, False)
        hs_scr[pl.ds(r0, 8), :] = hf
        gb = jnp.where(g < n_ctx_groups, n_ctx_groups - 1 - g, n_groups + n_ctx_groups - 1 - g)
        rb0 = pl.multiple_of(gb * 8, 8)
        hb, hb_in = group_scan(ab_scr[pl.ds(rb0, 8), :], bb_scr[pl.ds(rb0, 8), :], hb_in, True)
        bb_scr[pl.ds(rb0, 8), :] = hb
        return hf_in, hb_in
    h0 = jnp.zeros((8, cb_w), F32)
    lax.fori_loop(0, n_groups, step, (h0, h0), unroll=4)

    def fin(r, carry):
        r0 = pl.multiple_of(r * rc, rc)
        hsum = hs_scr[pl.ds(ctx_len + r0, rc), :] + bb_scr[pl.ds(ctx_len + r0, rc), :]
        o_ref[pl.ds(r0, rc), :] = (_gelu(gate_ref[pl.ds(r0, rc), :]) * hsum).astype(o_ref.dtype)
        return carry
    lax.fori_loop(0, seq // rc, fin, 0)


def _lru(y, conv_w, conv_b, w_cat, b_cat, lam, batch, seq, ctx_len):
    cb_w = 2 * LANES
    ncb = HALF // cb_w
    nl = batch * seq
    rc = 256
    t_all = seq + ctx_len
    scr = lambda r: pltpu.VMEM((r, cb_w), F32)
    return pl.pallas_call(
        functools.partial(_lru_kernel, seq, ctx_len, rc),
        grid=(batch, ncb),
        in_specs=[pl.BlockSpec((seq, cb_w), lambda b, c: (b, c)),
                  pl.BlockSpec((ctx_len, cb_w), lambda b, c: (nl // ctx_len + b, c)),
                  pl.BlockSpec((seq, cb_w), lambda b, c: (b, ncb + c)),
                  pl.BlockSpec((4, cb_w), lambda b, c: (0, c)),
                  pl.BlockSpec((1, cb_w), lambda b, c: (0, c)),
                  pl.BlockSpec((cb_w // LANES, LANES, 4 * LANES), lambda b, c: (c, 0, 0)),
                  pl.BlockSpec((cb_w // LANES, 1, 4 * LANES), lambda b, c: (c, 0, 0)),
                  pl.BlockSpec((2, cb_w), lambda b, c: (0, c))],
        out_specs=pl.BlockSpec((seq, cb_w), lambda b, c: (b, c)),
        out_shape=jax.ShapeDtypeStruct((nl, HALF), BF16),
        scratch_shapes=[scr(seq + 16), scr(t_all), scr(t_all), scr(t_all), scr(t_all), scr(t_all)],
        compiler_params=_cparams(("parallel", "parallel")),
        name="rglru",
    )(y, y, y, conv_w, conv_b, w_cat, b_cat, lam)


def _hg_tables():
    c = HG_CHUNK
    masks, pairs = [], []
    for d in range(2):
        r = np.arange(c) if d == 0 else c - 1 - np.arange(c)
        rt, rs = r[:, None], r[None, :]
        m = [rs <= rt]
        pm = []
        w = c // 2
        while w >= 1:
            bnd = w * (2 * (rt // (2 * w)) + 1)
            odd = (rt // w) % 2 == 1
            if w < HG_VREG_LEVEL:
                m.append(np.where(odd, (bnd <= rs) & (rs <= rt), (rt < rs) & (rs <= bnd - 1)))
            pm.append(odd & ((rs // w) % 2 == 0) & (rt // (2 * w) == rs // (2 * w)))
            w //= 2
        pm.append(rt == rs)
        m.append(np.ones((8, c), bool))
        masks.append(np.concatenate(m, axis=0))
        pairs.append(np.stack(pm))
    return np.stack(masks).astype(np.float32), np.stack(pairs).astype(np.float32)


def _hg_kernel(d, final, n_ctx_blk, n_levels, q_ref, f_ref, v_ref, lb_ref, mall_ref, pm_ref, *rest):
    if final:
        oprev_ref, g_ref, nw_ref, o_ref, s_scr = rest
    else:
        o_ref, s_scr = rest
    c, hd = HG_CHUNK, HEAD_DIM
    st = pl.program_id(1)
    n_chunks = HG_BLOCK // c

    @pl.when(st == 0)
    def _():
        s_scr[...] = jnp.zeros_like(s_scr)

    def make_chunk(with_out):
        def chunk(i, carry):
            ci = i if d == 0 else n_chunks - 1 - i
            r0 = pl.multiple_of(ci * c, c)
            mall = mall_ref[...]
            heads = range(HEADS)
            sl = [slice(h * hd, (h + 1) * hd) for h in heads]
            ks, cs2s = [], []
            for h in heads:
                lb = lb_ref[0, :, sl[h]]
                f = lb + (1.0 - lb) * _sigmoid(f_ref[pl.ds(r0, c), sl[h]])
                logf = jnp.log(f)
                lf_hi = logf.astype(BF16)
                lf_lo = (logf - lf_hi.astype(F32)).astype(BF16)
                ks.append(1.0 - f)
                cs2s.append(_dot(mall, jnp.concatenate([lf_hi, lf_lo], axis=1)))
            css = [cs2[:, :hd] + cs2[:, hd:] for cs2 in cs2s]
            es = [jnp.exp(cs) for cs in css]
            rows = lambda h, n: es[h][n * c:(n + 1) * c]
            n_wide = n_levels - (HG_VREG_LEVEL.bit_length() - 1)

            def level_decay(h, lv):
                if lv >= n_wide:
                    return rows(h, 1 + lv - n_wide)
                w = c >> (lv + 1)
                cum = css[h][:c]
                parts = [jnp.broadcast_to(cum[b:b + 1], (2 * w, hd))
                         for b in (a + w - 1 + d for a in range(0, c, 2 * w))]
                cum_b = parts[0] if len(parts) == 1 else jnp.concatenate(parts, axis=0)
                return jnp.exp(-jnp.abs(cum - cum_b))
            v16s = [v_ref[pl.ds(r0, c), sl[h]].astype(BF16) for h in heads]
            s_ts = [s_scr[h] for h in heads]
            if with_out:
                qs = []
                for h in heads:
                    qx = q_ref[pl.ds(r0, c), sl[h]]
                    qs.append(qx * _sigmoid(qx))
                a_s = []
                for h in heads:
                    a = pm_ref[n_levels] * _dot_nt(qs[h].astype(BF16), ks[h].astype(BF16))
                    for lv in range(n_levels):
                        el = level_decay(h, lv)
                        a += pm_ref[lv] * _dot_nt((qs[h] * el).astype(BF16), (ks[h] * el).astype(BF16))
                    a_s.append(a.astype(BF16))
                for h in heads:
                    o = _dot(a_s[h], v16s[h]) + _dot_nt((qs[h] * rows(h, 0)).astype(BF16), s_ts[h].astype(BF16))
                    if final:
                        o = o + oprev_ref[pl.ds(r0, c), sl[h]]
                        o = o * lax.rsqrt(jnp.mean(o * o, axis=-1, keepdims=True) + EPS) * nw_ref[:, sl[h]]
                        g = g_ref[pl.ds(r0, c), sl[h]]
                        o = o * (g * _sigmoid(g))
                    o_ref[pl.ds(r0, c), sl[h]] = o.astype(o_ref.dtype)
            base = (1 + n_levels - n_wide) * c
            for h in heads:
                after = jnp.exp(css[h][base:base + 1] - css[h][:c])
                s_scr[h] = es[h][base:base + 1] * s_ts[h] + _dot_tn(v16s[h], (ks[h] * after).astype(BF16))
            return carry
        return chunk

    @pl.when(st >= n_ctx_blk)
    def _():
        lax.fori_loop(0, n_chunks, make_chunk(True), 0, unroll=True)

    @pl.when(st < n_ctx_blk)
    def _():
        lax.fori_loop(0, n_chunks, make_chunk(False), 0)


def _hgrn(y, lb, d, batch, seq, ctx_len, o_prev=None, norm_w=None):
    rb = HG_BLOCK
    n_ctx_blk = ctx_len // rb
    n_lat_blk = seq // rb
    nl = batch * seq
    final = o_prev is not None
    mall_np, pm_np = _hg_tables()
    n_levels = pm_np.shape[1] - 1
    mall = jnp.asarray(mall_np[d], BF16)
    pm = jnp.asarray(pm_np[d], F32)
    q_col, f_col, v_col, g_col = 2, 3 + d, 5, 6

    def lat_blk(st):
        lc = jnp.maximum(st - n_ctx_blk, 0)
        return lc if d == 0 else n_lat_blk - 1 - lc

    def row_blk(b, st):
        cc = st if d == 0 else n_ctx_blk - 1 - st
        return jnp.where(st < n_ctx_blk, nl // rb + b * n_ctx_blk + cc, b * n_lat_blk + lat_blk(st))

    col = lambda cidx: pl.BlockSpec((rb, HALF), lambda b, st: (row_blk(b, st), cidx))
    lat = lambda cidx: pl.BlockSpec((rb, HALF), lambda b, st: (b * n_lat_blk + lat_blk(st), cidx))
    const = lambda a: pl.BlockSpec(a.shape, lambda b, st: (0,) * a.ndim)
    in_specs = [col(q_col), col(f_col), col(v_col),
                pl.BlockSpec((1, 1, HALF), lambda b, st: (d, 0, 0)), const(mall), const(pm)]
    args = [y, y, y, lb, mall, pm]
    if final:
        nw = norm_w.reshape(1, HALF)
        in_specs += [lat(0), lat(g_col), const(nw)]
        args += [o_prev, y, nw]
    return pl.pallas_call(
        functools.partial(_hg_kernel, d, final, n_ctx_blk, n_levels),
        grid=(batch, n_ctx_blk + n_lat_blk),
        in_specs=in_specs,
        out_specs=lat(0),
        out_shape=jax.ShapeDtypeStruct((nl, HALF), BF16 if final else F32),
        scratch_shapes=[pltpu.VMEM((HEADS, HEAD_DIM, HEAD_DIM), F32)],
        compiler_params=_cparams(("parallel", "arbitrary")),
        name=f"hgrn2_dir{d}",
    )(*args)


def _rot_cols(w):
    x1, x2 = jnp.split(w, 2, axis=-1)
    return jnp.concatenate([-x2, x1], axis=-1)


def _pad_lanes(w):
    return jnp.pad(w, [(0, 0)] * (w.ndim - 1) + [(0, LANES - w.shape[-1])])


def _attn_weights(w_in, uq, ukv):
    d = w_in.shape[0]
    kr_w = w_in[:, -MLA_ROPE:]
    w_in_p = jnp.concatenate([w_in[:, :-MLA_ROPE], _pad_lanes(kr_w), _pad_lanes(_rot_cols(kr_w))], axis=1)
    uq3 = uq.reshape(MLA_RANK, HEADS, MLA_QK)
    nope, rope = uq3[..., :HEAD_DIM], uq3[..., HEAD_DIM:]
    uq_p = jnp.concatenate([nope.reshape(MLA_RANK, HALF), _pad_lanes(rope).reshape(MLA_RANK, HALF),
                            _pad_lanes(_rot_cols(rope)).reshape(MLA_RANK, HALF)], axis=1)
    ukv3 = ukv.reshape(MLA_RANK, HEADS, 2 * HEAD_DIM)
    ukv_p = jnp.concatenate([ukv3[..., :HEAD_DIM].reshape(MLA_RANK, HALF),
                             ukv3[..., HEAD_DIM:].reshape(MLA_RANK, HALF)], axis=1)
    del d
    return w_in_p.astype(BF16), uq_p.astype(BF16), ukv_p.astype(BF16)


def _rope_tables(seq, ctx_len):
    rows = seq // GRID_W
    row_id = jnp.repeat(jnp.arange(rows), GRID_W).astype(F32)
    col_id = (jnp.arange(seq) % GRID_W).astype(F32)

    def table(dim):
        quarter = dim // 4
        inv_freq = ROPE_THETA ** (-jnp.arange(quarter, dtype=F32) / quarter)
        ang = jnp.concatenate([row_id[:, None] * inv_freq, col_id[:, None] * inv_freq], axis=-1)
        cos, sin = jnp.cos(ang), jnp.sin(ang)
        cos2 = jnp.concatenate([cos, cos], axis=-1)
        cos2 = jnp.concatenate([cos2, jnp.ones((ctx_len, dim), F32)], axis=0)
        return cos2, cos, sin

    ca, _, sin_a = table(HEAD_DIM)
    sna = jnp.concatenate([jnp.concatenate([-sin_a, sin_a], axis=-1), jnp.zeros((ctx_len, HEAD_DIM), F32)], axis=0)
    cb, _, sin_b = table(MLA_ROPE)
    snb = jnp.concatenate([jnp.concatenate([sin_b, sin_b], axis=-1), jnp.zeros((ctx_len, MLA_ROPE), F32)], axis=0)
    return ca, sna, _pad_lanes(cb), _pad_lanes(snb)


def kernel(x, c, ctx, c_ctx, mod_w, mod_b, norm_mix_w, norm_ffn_w, mix_out_w, ffn_gate_w, ffn_up_w, ffn_down_w, attn_in_w, gqa_q_norm_w, gqa_k_norm_w, mla_q_norm_w, mla_uq_w, mla_kv_norm_w, mla_ukv_w, rec_in_w, lru_conv_w, lru_conv_b, lru_ra_w, lru_ra_b, lru_ix_w, lru_ix_b, lru_lambda, hgrn_lb_logits, hgrn_norm_w, final_norm_w):
    batch, seq, d = x.shape
    ctx_len = ctx.shape[1]
    depth = mod_w.shape[0]
    assert depth == 2 and batch < 8 and seq % ctx_len == 0 and ctx_len == HG_BLOCK
    nl = batch * seq
    rows_all = nl + batch * ctx_len

    cc = jnp.zeros((8, d), F32).at[:batch].set(c).at[batch].set(c_ctx)
    mods = _modulation(cc, mod_w, mod_b).reshape(depth * 8 * MOD_CHUNKS, 1, d)
    x_rows, ctx_rows = x.reshape(nl, d), ctx.reshape(batch * ctx_len, d)
    nm = functools.partial(_norm_mod_matmul, seq=seq, batch=batch)
    um = functools.partial(_u_matmul, seq=seq)
    mrn = functools.partial(_matmul_residual_norm, seq=seq, batch=batch)
    mix_w, down_w = mix_out_w.astype(BF16), ffn_down_w.astype(BF16)
    base1 = 8 * MOD_CHUNKS

    def ffn_hidden(u, rows, l):
        return um(u, rows, [ffn_gate_w, ffn_up_w], l, "swiglu", BF16, tn_prefs=(512, 256, 128), name=f"ffn_up{l}")

    w_in_p, uq_p, ukv_p = _attn_weights(attn_in_w[0], mla_uq_w[0], mla_ukv_w[0])
    y = nm([x_rows, ctx_rows], rows_all, norm_mix_w[0], mods, 0, [w_in_p[None]], 0, "plain", BF16,
           tn_prefs=(1408, 704, 256, 128), name="attn_in")
    qg, kg, vg, qm, km, vm = _attn_prep(
        y, gqa_q_norm_w[0].reshape(1, -1), gqa_k_norm_w[0].reshape(1, -1), mla_q_norm_w[0].reshape(1, -1),
        mla_kv_norm_w[0].reshape(1, -1), uq_p, ukv_p, _rope_tables(seq, ctx_len), batch, seq, ctx_len)
    og = _attention(qg, kg, vg, GQA_GROUP, 1, batch, seq, ctx_len, "gqa")
    om = _attention(qm, km, vm, 4, 4, batch, seq, ctx_len, "mla")
    h, u = mrn([og, om], [(mix_w, 0, 0), (mix_w, 0, 1)], [x_rows, ctx_rows], rows_all, mods, 2,
               norm_ffn_w[0], 3, tm_prefs=(512, 256), name="mix_out0", n_split=4)
    h, u = mrn([ffn_hidden(u, rows_all, 0)], [(down_w, 0, 0)], [h], rows_all, mods, 5,
               norm_mix_w[1], base1, tm_prefs=(256,), name="ffn_down0")

    y = um(u, rows_all, [rec_in_w], 0, "plain", F32, tn_prefs=(1024, 512, 256, 128), name="rec_in")
    w_cat = jnp.concatenate([lru_ra_w[0, 0], lru_ix_w[0, 0], lru_ra_w[0, 1], lru_ix_w[0, 1]], axis=-1).astype(BF16)
    b_cat = jnp.concatenate([lru_ra_b[0, 0], lru_ix_b[0, 0], lru_ra_b[0, 1], lru_ix_b[0, 1]], axis=-1)[:, None, :]
    y_lru = _lru(y, lru_conv_w[0], lru_conv_b[0].reshape(1, -1), w_cat, b_cat, lru_lambda[0], batch, seq, ctx_len)
    lb_all = jnp.cumsum(jax.nn.softmax(hgrn_lb_logits.astype(F32), axis=1), axis=1)
    lb = (lb_all - lb_all[:, :1])[:, 1].reshape(2, 1, HALF)
    o_fwd = _hgrn(y, lb, 0, batch, seq, ctx_len)
    y_hg = _hgrn(y, lb, 1, batch, seq, ctx_len, o_prev=o_fwd, norm_w=hgrn_norm_w[0])
    h, u = mrn([y_lru, y_hg], [(mix_w, 1, 0), (mix_w, 1, 1)], [h], nl, mods, base1 + 2,
               norm_ffn_w[1], base1 + 3, tm_prefs=(512, 256), name="mix_out1", n_split=4)
    out = mrn([ffn_hidden(u, nl, 1)], [(down_w, 1, 0)], [h], nl, mods, base1 + 5,
              final_norm_w, None, tm_prefs=(256,), name="ffn_down1")
    return out.reshape(batch, seq, d)
```
